```python
import math
import jax, jax.numpy as jnp
from jax import lax
import numpy as np

D_MODEL = 2048
BATCH = 8
SEQ = 2048
DEPTH = 2

N_SUBLAYERS = 3
FFN_DIM = 5632
FFN_RES_WEIGHT = 0.5
MIXER_RES_WEIGHT = 1.0
NORM_EPS = 1e-6

SSD_HEADS = 32
SSD_HEAD_DIM = 64
SSD_WIDTH = SSD_HEADS * SSD_HEAD_DIM
SSD_GROUPS = 4
SSD_STATE = 128
SSD_CONV = 4
SSD_CHUNK = 128
SSD_BC_WIDTH = SSD_GROUPS * SSD_STATE
SSD_CONV_CH = SSD_WIDTH + 2 * SSD_BC_WIDTH

ATT_HEADS = 16
ATT_HEAD_DIM = 128
ATT_WIDTH = ATT_HEADS * ATT_HEAD_DIM
DILATED_PATTERNS = ((128, 1), (512, 4), (2048, 16))
ATT_BLOCK = 128
ROPE_THETA = 10000.0

HYB_SPLIT_SIZES = (SSD_WIDTH, SSD_CONV_CH, SSD_HEADS, ATT_WIDTH, ATT_WIDTH, ATT_WIDTH)
HYB_IN_WIDTH = SSD_WIDTH + SSD_CONV_CH + SSD_HEADS + 3 * ATT_WIDTH
HYB_OUT_WIDTH = SSD_WIDTH + ATT_WIDTH

SGU_WIDTH = 4096
SGU_GROUPS = 8
SGU_CHUNK = 128

N_HYB_LAYERS = (DEPTH + 1) // 2
N_SGU_LAYERS = DEPTH // 2

kernel_name = "hybrid_ssd_dilated_attn_sgu_macaron_trunk"


def rms_norm(x, g):
    x32 = x.astype(jnp.float32)
    y = x32 * lax.rsqrt(jnp.mean(x32 * x32, axis=-1, keepdims=True) + NORM_EPS)
    return (y * g.astype(jnp.float32)).astype(x.dtype)


def layer_norm(x, g, b):
    x32 = x.astype(jnp.float32)
    mu = jnp.mean(x32, axis=-1, keepdims=True)
    var = jnp.mean(jnp.square(x32 - mu), axis=-1, keepdims=True)
    y = (x32 - mu) * lax.rsqrt(var + NORM_EPS)
    return (y * g.astype(jnp.float32) + b.astype(jnp.float32)).astype(x.dtype)


def swiglu_ffn(h, w_gate, w_up, w_down):
    return (jax.nn.silu(h @ w_gate) * (h @ w_up)) @ w_down


def modulated_sublayer(x, mod, g_pre, g_post, res_weight, fn):
    shift, scale, gate = mod[:, 0, None, :], mod[:, 1, None, :], mod[:, 2, None, :]
    h = rms_norm(x, g_pre) * (1 + scale) + shift
    return x + res_weight * (1 + gate) * rms_norm(fn(h), g_post)


def apply_rope(t, positions):
    half = t.shape[-1] // 2
    inv_freq = ROPE_THETA ** (-jnp.arange(half, dtype=jnp.float32) / half)
    ang = positions.astype(jnp.float32)[..., None] * inv_freq
    cos = jnp.cos(ang)[:, :, None, :]
    sin = jnp.sin(ang)[:, :, None, :]
    t32 = t.astype(jnp.float32)
    t1, t2 = t32[..., :half], t32[..., half:]
    return jnp.concatenate([t1 * cos - t2 * sin, t2 * cos + t1 * sin], axis=-1).astype(t.dtype)


def causal_depthwise_conv(x, w, b):
    k = w.shape[0]
    out = lax.conv_general_dilated(
        x, w[:, None, :], window_strides=(1,), padding=((k - 1, 0),),
        dimension_numbers=("NWC", "WIO", "NWC"), feature_group_count=x.shape[-1])
    return out + b


def ssd_chunked_scan(x, dt, a, bmat, cmat):
    b, s, h, p = x.shape
    g, n = bmat.shape[2], bmat.shape[3]
    k = h // g
    cl = min(SSD_CHUNK, s)
    nc = s // cl
    xc = (x * dt[..., None]).reshape(b, nc, cl, g, k, p)
    adt = (a * dt).reshape(b, nc, cl, g, k).transpose(0, 3, 4, 1, 2)
    bc = bmat.reshape(b, nc, cl, g, n)
    cc = cmat.reshape(b, nc, cl, g, n)
    acs = jnp.cumsum(adt, axis=-1)
    causal = jnp.tril(jnp.ones((cl, cl), dtype=bool))
    seg = acs[..., :, None] - acs[..., None, :]
    decay = jnp.exp(jnp.where(causal, seg, -jnp.inf))
    cb = jnp.einsum("bclgn,bcsgn->bcgls", cc, bc)
    y_diag = jnp.einsum("bcgls,bgkcls,bcsgkp->bclgkp", cb, decay, xc)
    state_decay = jnp.exp(acs[..., -1:] - acs)
    states = jnp.einsum("bclgn,bgkcl,bclgkp->bcgkpn", bc, state_decay, xc)
    chunk_decay = jnp.exp(acs[..., -1])

    def step(carry, inp):
        st, dec = inp
        return carry * dec[..., None, None] + st, carry

    h0 = jnp.zeros((b, g, k, p, n), dtype=jnp.float32)
    _, prev = lax.scan(step, h0, (states.transpose(1, 0, 2, 3, 4, 5), chunk_decay.transpose(3, 0, 1, 2)))
    prev = prev.transpose(1, 0, 2, 3, 4, 5)
    y_off = jnp.einsum("bclgn,bcgkpn,bgkcl->bclgkp", cc, prev, jnp.exp(acs))
    return (y_diag + y_off).reshape(b, s, h, p)


def dilated_window_attention(q, k, v, window, dilation):
    b, s, h, dh = q.shape
    sub_len = s // dilation
    span = window // dilation
    blk = min(ATT_BLOCK, sub_len)
    nb = sub_len // blk

    def to_blocks(t):
        t = t.reshape(b, sub_len, dilation, h, dh).transpose(0, 2, 3, 1, 4)
        return t.reshape(b, dilation, h, nb, blk, dh)

    def with_prev(t):
        tp = jnp.concatenate([jnp.zeros_like(t[:, :, :, :1]), t], axis=3)
        return jnp.concatenate([tp[:, :, :, :-1], tp[:, :, :, 1:]], axis=4)

    qb = to_blocks(q)
    kw = with_prev(to_blocks(k))
    vw = with_prev(to_blocks(v))
    scores = jnp.einsum("brhnqd,brhnkd->brhnqk", qb, kw,
                        preferred_element_type=jnp.float32) * (dh ** -0.5)
    q_idx = jnp.arange(blk)[:, None] + blk
    k_idx = jnp.arange(2 * blk)[None, :]
    dist = q_idx - k_idx
    k_pos = jnp.arange(nb)[:, None, None] * blk - blk + k_idx[None]
    mask = (dist >= 0)[None] & (dist <= span)[None] & (k_pos >= 0)
    scores = jnp.where(mask, scores, -jnp.inf)
    lse = jax.nn.logsumexp(scores, axis=-1)
    probs = jnp.exp(scores - lse[..., None])
    out = jnp.einsum("brhnqk,brhnkd->brhnqd", probs, vw.astype(jnp.float32))
    out = out.reshape(b, dilation, h, sub_len, dh).transpose(0, 3, 1, 2, 4).reshape(b, s, h, dh)
    lse = lse.reshape(b, dilation, h, sub_len).transpose(0, 3, 1, 2).reshape(b, s, h)
    return out, lse


def hybrid_ssd_attention(h, positions, w_in, conv_w, conv_b, dt_bias, a_log, d_skip, ssd_norm_g, w_out):
    b, s, _ = h.shape
    proj = h @ w_in
    split_at = [int(i) for i in np.cumsum(HYB_SPLIT_SIZES)[:-1]]
    z, xbc, dt_raw, q, k, v = jnp.split(proj, split_at, axis=-1)
    xbc = jax.nn.silu(causal_depthwise_conv(xbc, conv_w, conv_b)).astype(jnp.float32)
    xs = xbc[..., :SSD_WIDTH].reshape(b, s, SSD_HEADS, SSD_HEAD_DIM)
    bm = xbc[..., SSD_WIDTH:SSD_WIDTH + SSD_BC_WIDTH].reshape(b, s, SSD_GROUPS, SSD_STATE)
    cm = xbc[..., SSD_WIDTH + SSD_BC_WIDTH:].reshape(b, s, SSD_GROUPS, SSD_STATE)
    dt = jax.nn.softplus(dt_raw.astype(jnp.float32) + dt_bias.astype(jnp.float32))
    a = -jnp.exp(a_log.astype(jnp.float32))
    y = ssd_chunked_scan(xs, dt, a, bm, cm) + xs * d_skip.astype(jnp.float32)[:, None]
    y = y.reshape(b, s, SSD_WIDTH) * jax.nn.silu(z.astype(jnp.float32))
    y_a = rms_norm(y, ssd_norm_g).astype(h.dtype)
    q = apply_rope(q.reshape(b, s, ATT_HEADS, ATT_HEAD_DIM), positions)
    k = apply_rope(k.reshape(b, s, ATT_HEADS, ATT_HEAD_DIM), positions)
    v = v.reshape(b, s, ATT_HEADS, ATT_HEAD_DIM)
    outs, lses = [], []
    for window, dilation in DILATED_PATTERNS:
        o, l = dilated_window_attention(q, k, v, window, dilation)
        outs.append(o)
        lses.append(l)
    weights = jax.nn.softmax(jnp.stack(lses, axis=0), axis=0)
    y_b = jnp.sum(weights[..., None] * jnp.stack(outs, axis=0), axis=0)
    y_b = y_b.reshape(b, s, ATT_WIDTH).astype(h.dtype)
    return jnp.concatenate([y_a, y_b], axis=-1) @ w_out


def chunked_sgu_mixer(h, w_in, b_in, ln_g, ln_b, w_spatial, b_spatial, w_out):
    b, s, _ = h.shape
    zz = jax.nn.gelu(h @ w_in + b_in)
    u, v = zz[..., :SGU_WIDTH], zz[..., SGU_WIDTH:]
    v = layer_norm(v, ln_g, ln_b)
    nc = s // SGU_CHUNK
    vc = v.reshape(b, nc, SGU_CHUNK, SGU_GROUPS, SGU_WIDTH // SGU_GROUPS)
    causal = jnp.tril(jnp.ones((SGU_CHUNK, SGU_CHUNK), dtype=bool))
    ws = jnp.where(causal, w_spatial, 0)
    mixed = jnp.einsum("gts,bcsgd->bctgd", ws, vc) + b_spatial.T[None, None, :, :, None]
    return (u * mixed.reshape(b, s, SGU_WIDTH)) @ w_out


def setup_inputs(seed: int = 0) -> dict:
    key = jax.random.key(seed)
    ks = iter(jax.random.split(key, 32))
    f32 = jnp.float32
    nrm = lambda shape, scale: jax.random.normal(next(ks), shape, f32) * scale
    x = nrm((BATCH, SEQ, D_MODEL), 1.0)
    c = nrm((BATCH, D_MODEL), 1.0)
    offsets = jax.random.randint(next(ks), (BATCH, 1), 0, 4096, dtype=jnp.int32)
    positions = offsets + jnp.arange(SEQ, dtype=jnp.int32)[None, :]
    w_mod = nrm((DEPTH, D_MODEL, N_SUBLAYERS * 3 * D_MODEL), 0.1 * D_MODEL ** -0.5)
    b_mod = nrm((DEPTH, N_SUBLAYERS * 3 * D_MODEL), 0.02)
    norm_pre = 1.0 + nrm((DEPTH, N_SUBLAYERS, D_MODEL), 0.02)
    norm_post = 1.0 + nrm((DEPTH, N_SUBLAYERS, D_MODEL), 0.02)
    ffn_w_gate = nrm((DEPTH, 2, D_MODEL, FFN_DIM), D_MODEL ** -0.5)
    ffn_w_up = nrm((DEPTH, 2, D_MODEL, FFN_DIM), D_MODEL ** -0.5)
    ffn_w_down = nrm((DEPTH, 2, FFN_DIM, D_MODEL), FFN_DIM ** -0.5)
    hyb_w_in = nrm((N_HYB_LAYERS, D_MODEL, HYB_IN_WIDTH), D_MODEL ** -0.5)
    hyb_conv_w = nrm((N_HYB_LAYERS, SSD_CONV, SSD_CONV_CH), SSD_CONV ** -0.5)
    hyb_conv_b = nrm((N_HYB_LAYERS, SSD_CONV_CH), 0.02)
    dt0 = jnp.exp(jax.random.uniform(next(ks), (N_HYB_LAYERS, SSD_HEADS), f32,
                                     minval=math.log(1e-3), maxval=math.log(1e-1)))
    hyb_dt_bias = dt0 + jnp.log(-jnp.expm1(-dt0))
    hyb_a_log = jnp.log(jax.random.uniform(next(ks), (N_HYB_LAYERS, SSD_HEADS), f32, minval=1.0, maxval=16.0))
    hyb_d_skip = 1.0 + nrm((N_HYB_LAYERS, SSD_HEADS), 0.1)
    hyb_norm_g = 1.0 + nrm((N_HYB_LAYERS, SSD_WIDTH), 0.02)
    hyb_w_out = nrm((N_HYB_LAYERS, HYB_OUT_WIDTH, D_MODEL), HYB_OUT_WIDTH ** -0.5)
    sgu_w_in = nrm((N_SGU_LAYERS, D_MODEL, 2 * SGU_WIDTH), D_MODEL ** -0.5)
    sgu_b_in = nrm((N_SGU_LAYERS, 2 * SGU_WIDTH), 0.02)
    sgu_ln_g = 1.0 + nrm((N_SGU_LAYERS, SGU_WIDTH), 0.02)
    sgu_ln_b = nrm((N_SGU_LAYERS, SGU_WIDTH), 0.02)
    sgu_w_spatial = nrm((N_SGU_LAYERS, SGU_GROUPS, SGU_CHUNK, SGU_CHUNK), SGU_CHUNK ** -0.5)
    sgu_b_spatial = 1.0 + nrm((N_SGU_LAYERS, SGU_GROUPS, SGU_CHUNK), 0.02)
    sgu_w_out = nrm((N_SGU_LAYERS, SGU_WIDTH, D_MODEL), SGU_WIDTH ** -0.5)
    return {"x": x, "c": c, "positions": positions, "w_mod": w_mod, "b_mod": b_mod,
            "norm_pre": norm_pre, "norm_post": norm_post,
            "ffn_w_gate": ffn_w_gate, "ffn_w_up": ffn_w_up, "ffn_w_down": ffn_w_down,
            "hyb_w_in": hyb_w_in, "hyb_conv_w": hyb_conv_w, "hyb_conv_b": hyb_conv_b,
            "hyb_dt_bias": hyb_dt_bias, "hyb_a_log": hyb_a_log, "hyb_d_skip": hyb_d_skip,
            "hyb_norm_g": hyb_norm_g, "hyb_w_out": hyb_w_out,
            "sgu_w_in": sgu_w_in, "sgu_b_in": sgu_b_in, "sgu_ln_g": sgu_ln_g, "sgu_ln_b": sgu_ln_b,
            "sgu_w_spatial": sgu_w_spatial, "sgu_b_spatial": sgu_b_spatial, "sgu_w_out": sgu_w_out}


def reference(x, c, positions, w_mod, b_mod, norm_pre, norm_post, ffn_w_gate, ffn_w_up, ffn_w_down,
              hyb_w_in, hyb_conv_w, hyb_conv_b, hyb_dt_bias, hyb_a_log, hyb_d_skip, hyb_norm_g, hyb_w_out,
              sgu_w_in, sgu_b_in, sgu_ln_g, sgu_ln_b, sgu_w_spatial, sgu_b_spatial, sgu_w_out):
    c_act = jax.nn.silu(c)
    for layer in range(DEPTH):
        mod = (c_act @ w_mod[layer] + b_mod[layer]).reshape(-1, N_SUBLAYERS, 3, D_MODEL)
        x = modulated_sublayer(
            x, mod[:, 0], norm_pre[layer, 0], norm_post[layer, 0], FFN_RES_WEIGHT,
            lambda h: swiglu_ffn(h, ffn_w_gate[layer, 0], ffn_w_up[layer, 0], ffn_w_down[layer, 0]))
        i = layer // 2
        if layer % 2 == 0:
            mixer = lambda h: hybrid_ssd_attention(
                h, positions, hyb_w_in[i], hyb_conv_w[i], hyb_conv_b[i], hyb_dt_bias[i],
                hyb_a_log[i], hyb_d_skip[i], hyb_norm_g[i], hyb_w_out[i])
        else:
            mixer = lambda h: chunked_sgu_mixer(
                h, sgu_w_in[i], sgu_b_in[i], sgu_ln_g[i], sgu_ln_b[i],
                sgu_w_spatial[i], sgu_b_spatial[i], sgu_w_out[i])
        x = modulated_sublayer(x, mod[:, 1], norm_pre[layer, 1], norm_post[layer, 1], MIXER_RES_WEIGHT, mixer)
        x = modulated_sublayer(
            x, mod[:, 2], norm_pre[layer, 2], norm_post[layer, 2], FFN_RES_WEIGHT,
            lambda h: swiglu_ffn(h, ffn_w_gate[layer, 1], ffn_w_up[layer, 1], ffn_w_down[layer, 1]))
    return x
```

```python
import functools

import jax
import jax.numpy as jnp
from jax import lax
from jax.experimental import pallas as pl
from jax.experimental.pallas import tpu as pltpu

NORM_EPS = 1e-6
FFN_RES_WEIGHT = 0.5
MIXER_RES_WEIGHT = 1.0

SSD_HEADS = 32
SSD_HEAD_DIM = 64
SSD_WIDTH = SSD_HEADS * SSD_HEAD_DIM
SSD_GROUPS = 4
SSD_STATE = 128
SSD_CONV = 4
SSD_CHUNK = 128
SSD_BC_WIDTH = SSD_GROUPS * SSD_STATE
SSD_CONV_CH = SSD_WIDTH + 2 * SSD_BC_WIDTH
SSD_HEADS_PER_GROUP = SSD_HEADS // SSD_GROUPS
SSD_GROUP_WIDTH = SSD_HEADS_PER_GROUP * SSD_HEAD_DIM

ATT_HEADS = 16
ATT_HEAD_DIM = 128
ATT_WIDTH = ATT_HEADS * ATT_HEAD_DIM
ATT_BLOCK = 128
DILATED_PATTERNS = ((128, 1), (512, 4), (2048, 16))
ROPE_THETA = 10000.0

SGU_WIDTH = 4096
SGU_GROUPS = 8
SGU_CHUNK = 128
SGU_GROUP_WIDTH = SGU_WIDTH // SGU_GROUPS

LANES = 128
SUBLANES = 8
VMEM_LIMIT_BYTES = 56 * 1024 * 1024

ROW_TILE = 512
FFN_TILE = 512
PROJ_TILE = 512
MOD_TILE = 1024

BF16 = jnp.bfloat16
F32 = jnp.float32


def _params(*semantics):
    return pltpu.CompilerParams(dimension_semantics=semantics, vmem_limit_bytes=VMEM_LIMIT_BYTES)


def _rms(x):
    return x * lax.rsqrt(jnp.mean(x * x, axis=-1, keepdims=True) + NORM_EPS)


def _silu(x):
    return x * jax.nn.sigmoid(x)


def _dot(a, b):
    return jnp.dot(a, b, preferred_element_type=F32)


def _dot_exact(a, b):
    return jnp.dot(a, b, preferred_element_type=F32, precision=lax.Precision.HIGHEST)


def _dot_nt(a, b):
    return lax.dot_general(a, b, (((1,), (1,)), ((), ())), preferred_element_type=F32)


def _pre_norm(x_ref, mod_ref, gpre_ref):
    return _rms(x_ref[...]) * gpre_ref[...] * (1.0 + mod_ref[1:2, :]) + mod_ref[0:1, :]


def _post_residual(x_ref, y, mod_ref, gpost_ref, res_weight):
    return x_ref[...] + (res_weight * (1.0 + mod_ref[2:3, :])) * (_rms(y) * gpost_ref[...])


def _mod_body(c_ref, w_ref, b_ref, o_ref):
    ca = _silu(c_ref[...]).astype(BF16)
    o_ref[...] = _dot(ca, w_ref[...].astype(BF16)) + b_ref[...]


def _modulation(c, w_mod, b_mod):
    depth, d, n = w_mod.shape
    b = c.shape[0]
    return pl.pallas_call(
        _mod_body,
        grid=(depth, n // MOD_TILE),
        in_specs=[
            pl.BlockSpec((b, d), lambda l, j: (0, 0)),
            pl.BlockSpec((None, d, MOD_TILE), lambda l, j: (l, 0, j)),
            pl.BlockSpec((None, 1, MOD_TILE), lambda l, j: (l, 0, j)),
        ],
        out_specs=pl.BlockSpec((None, b, MOD_TILE), lambda l, j: (l, 0, j)),
        out_shape=jax.ShapeDtypeStruct((depth, b, n), F32),
        compiler_params=_params("parallel", "parallel"),
        name="modulation",
    )(c, w_mod, b_mod.reshape(depth, 1, n))


def _ffn_body(x_ref, mod_ref, gpre_ref, gpost_ref, wg_ref, wu_ref, wd_ref, o_ref, h_ref, acc_ref, *, n_f):
    f = pl.program_id(2)

    @pl.when(f == 0)
    def _():
        h_ref[...] = _pre_norm(x_ref, mod_ref, gpre_ref).astype(BF16)
        acc_ref[...] = jnp.zeros_like(acc_ref)

    h = h_ref[...]
    a = (_silu(_dot(h, wg_ref[...])) * _dot(h, wu_ref[...])).astype(BF16)
    acc_ref[...] += _dot(a, wd_ref[...])

    @pl.when(f == n_f - 1)
    def _():
        o_ref[...] = _post_residual(x_ref, acc_ref[...], mod_ref, gpost_ref, FFN_RES_WEIGHT)


def _ffn_sublayer(x, mod, g_pre, g_post, w_gate, w_up, w_down, layer, sub, idx):
    b, s, d = x.shape
    f_dim = w_gate.shape[-1]
    n_f = f_dim // FFN_TILE
    row = lambda bi, i, f: (bi, i, 0)
    return pl.pallas_call(
        functools.partial(_ffn_body, n_f=n_f),
        grid=(b, s // ROW_TILE, n_f),
        in_specs=[
            pl.BlockSpec((None, ROW_TILE, d), row),
            pl.BlockSpec((None, None, None, 3, d), lambda bi, i, f: (layer, bi, sub, 0, 0)),
            pl.BlockSpec((None, None, 1, d), lambda bi, i, f: (layer, sub, 0, 0)),
            pl.BlockSpec((None, None, 1, d), lambda bi, i, f: (layer, sub, 0, 0)),
            pl.BlockSpec((None, None, d, FFN_TILE), lambda bi, i, f: (layer, idx, 0, f)),
            pl.BlockSpec((None, None, d, FFN_TILE), lambda bi, i, f: (layer, idx, 0, f)),
            pl.BlockSpec((None, None, FFN_TILE, d), lambda bi, i, f: (layer, idx, f, 0)),
        ],
        out_specs=pl.BlockSpec((None, ROW_TILE, d), row),
        out_shape=jax.ShapeDtypeStruct(x.shape, F32),
        scratch_shapes=[pltpu.VMEM((ROW_TILE, d), BF16), pltpu.VMEM((ROW_TILE, d), F32)],
        compiler_params=_params("parallel", "parallel", "arbitrary"),
        name="ffn_sublayer",
    )(x, mod, g_pre, g_post, w_gate, w_up, w_down)


def _gelu_tanh(x):
    return 0.5 * x * (1.0 + jnp.tanh(0.7978845608028654 * (x + 0.044715 * (x * x * x))))


def _proj_body(x_ref, mod_ref, gpre_ref, w_ref, *rest, gelu):
    if gelu:
        b_ref, o_ref, h_ref = rest
    else:
        o_ref, h_ref = rest

    @pl.when(pl.program_id(2) == 0)
    def _():
        h_ref[...] = _pre_norm(x_ref, mod_ref, gpre_ref).astype(BF16)

    y = _dot(h_ref[...], w_ref[...])
    if gelu:
        y = _gelu_tanh(y + b_ref[...])
    o_ref[...] = y.astype(o_ref.dtype)


def _pre_norm_proj(x, mod, g_pre, w, bias, layer, sub, tile, gelu=False):
    b, s, d = x.shape
    n = w.shape[-1]
    in_specs = [
        pl.BlockSpec((None, ROW_TILE, d), lambda bi, i, j: (bi, i, 0)),
        pl.BlockSpec((None, None, None, 3, d), lambda bi, i, j: (layer, bi, sub, 0, 0)),
        pl.BlockSpec((None, None, 1, d), lambda bi, i, j: (layer, sub, 0, 0)),
        pl.BlockSpec((d, tile), lambda bi, i, j: (0, j)),
    ]
    args = [x, mod, g_pre, w]
    if gelu:
        in_specs.append(pl.BlockSpec((1, tile), lambda bi, i, j: (0, j)))
        args.append(bias)
    return pl.pallas_call(
        functools.partial(_proj_body, gelu=gelu),
        grid=(b, s // ROW_TILE, n // tile),
        in_specs=in_specs,
        out_specs=pl.BlockSpec((None, ROW_TILE, tile), lambda bi, i, j: (bi, i, j)),
        out_shape=jax.ShapeDtypeStruct((b, s, n), F32),
        scratch_shapes=[pltpu.VMEM((ROW_TILE, d), BF16)],
        compiler_params=_params("parallel", "parallel", "arbitrary"),
        name="pre_norm_proj",
    )(*args)


def _out_body(a_ref, b_ref, w_ref, x_ref, mod_ref, gpost_ref, o_ref, acc_ref):
    k = pl.program_id(2)

    @pl.when(k == 0)
    def _():
        acc_ref[...] = _dot(a_ref[...], w_ref[...])

    @pl.when(k == 1)
    def _():
        y = acc_ref[...] + _dot(b_ref[...], w_ref[...])
        o_ref[...] = _post_residual(x_ref, y, mod_ref, gpost_ref, MIXER_RES_WEIGHT)


def _out_proj_residual(lhs_a, col_a, lhs_b, col_b, w, x, mod, g_post, layer, sub):
    b, s, d = x.shape
    half = w.shape[0] // 2
    row = lambda bi, i, k: (bi, i, 0)
    return pl.pallas_call(
        _out_body,
        grid=(b, s // ROW_TILE, 2),
        in_specs=[
            pl.BlockSpec((None, ROW_TILE, half), lambda bi, i, k: (bi, i, col_a)),
            pl.BlockSpec((None, ROW_TILE, half), lambda bi, i, k: (bi, i, col_b)),
            pl.BlockSpec((half, d), lambda bi, i, k: (k, 0)),
            pl.BlockSpec((None, ROW_TILE, d), row),
            pl.BlockSpec((None, None, None, 3, d), lambda bi, i, k: (layer, bi, sub, 0, 0)),
            pl.BlockSpec((None, None, 1, d), lambda bi, i, k: (layer, sub, 0, 0)),
        ],
        out_specs=pl.BlockSpec((None, ROW_TILE, d), row),
        out_shape=jax.ShapeDtypeStruct(x.shape, F32),
        scratch_shapes=[pltpu.VMEM((ROW_TILE, d), F32)],
        compiler_params=_params("parallel", "parallel", "arbitrary"),
        name="out_proj_residual",
    )(lhs_a, lhs_b, w, x, mod, g_post)


def _rope_body(pos_ref, freq_ref, cos_ref, sin_ref):
    ang = pos_ref[...].astype(F32) * freq_ref[...]
    lane = lax.broadcasted_iota(jnp.int32, ang.shape, 1)
    cos_ref[...] = jnp.cos(ang)
    sin_ref[...] = jnp.where(lane < ATT_HEAD_DIM // 2, -1.0, 1.0) * jnp.sin(ang)


def _rope_tables(positions):
    b, s = positions.shape
    half = ATT_HEAD_DIM // 2
    inv_freq = ROPE_THETA ** (-jnp.arange(half, dtype=F32) / half)
    freq = jnp.concatenate([inv_freq, inv_freq]).reshape(1, ATT_HEAD_DIM)
    tile = 512
    spec = pl.BlockSpec((None, tile, ATT_HEAD_DIM), lambda bi, i: (bi, i, 0))
    return pl.pallas_call(
        _rope_body,
        grid=(b, s // tile),
        in_specs=[
            pl.BlockSpec((None, tile, 1), lambda bi, i: (bi, i, 0)),
            pl.BlockSpec((1, ATT_HEAD_DIM), lambda bi, i: (0, 0)),
        ],
        out_specs=[spec, spec],
        out_shape=[jax.ShapeDtypeStruct((b, s, ATT_HEAD_DIM), F32)] * 2,
        compiler_params=_params("parallel", "parallel"),
        name="rope_tables",
    )(positions.reshape(b, s, 1), freq)


def _ssd_body(xs_ref, bm_ref, cm_ref, z_ref, dt_ref, cw_ref, cb_ref, dtb_ref, alog_ref, dskip_ref, ng_ref,
              o_ref, tail_ref, state_ref):
    L = SSD_CHUNK

    @pl.when(pl.program_id(1) == 0)
    def _():
        tail_ref[...] = jnp.zeros_like(tail_ref)
        state_ref[...] = jnp.zeros_like(state_ref)

    def conv_silu(raw, lo, hi):
        tail = tail_ref[:, lo:hi]
        row8 = lax.broadcasted_iota(jnp.int32, tail.shape, 0)
        acc = raw * cw_ref[SSD_CONV - 1:SSD_CONV, lo:hi] + cb_ref[:, lo:hi]
        for k in range(1, SSD_CONV):
            rolled = pltpu.roll(raw, k, 0)
            top = jnp.where(row8 < k, pltpu.roll(tail, k, 0), rolled[0:SUBLANES])
            shifted = jnp.concatenate([top, rolled[SUBLANES:]], axis=0)
            acc = acc + shifted * cw_ref[SSD_CONV - 1 - k:SSD_CONV - k, lo:hi]
        tail_ref[:, lo:hi] = raw[L - SUBLANES:L]
        return _silu(acc)

    xs = conv_silu(xs_ref[...], 0, SSD_WIDTH)
    bm = conv_silu(bm_ref[...], SSD_WIDTH, SSD_WIDTH + SSD_BC_WIDTH)
    cm = conv_silu(cm_ref[...], SSD_WIDTH + SSD_BC_WIDTH, SSD_CONV_CH)

    dt_in = dt_ref[...] + dtb_ref[...]
    dt = jnp.maximum(dt_in, 0.0) + jnp.log1p(jnp.exp(-jnp.abs(dt_in)))
    adt = dt * (-jnp.exp(alog_ref[...]))
    row = lax.broadcasted_iota(jnp.int32, (L, L), 0)
    col = lax.broadcasted_iota(jnp.int32, (L, L), 1)
    causal = row >= col
    acs = _dot_exact(causal.astype(F32), adt)
    acs_t = acs.T
    acs_last = acs[L - 1:L, :]

    hrow = lax.broadcasted_iota(jnp.int32, (LANES, SSD_WIDTH), 0)
    hcol = lax.broadcasted_iota(jnp.int32, (LANES, SSD_WIDTH), 1)
    expand = (hrow == (hcol >> (SSD_HEAD_DIM.bit_length() - 1))).astype(F32)
    stacked = jnp.concatenate([dt, jnp.exp(acs), jnp.exp(acs_last - acs)], axis=0)
    wide = _dot_exact(stacked, expand)
    dt_w, decay_in_w, decay_out_w = wide[0:L], wide[L:2 * L], wide[2 * L:3 * L]
    chunk_decay_w = decay_in_w[L - 1:L, :]

    xdt = xs * dt_w
    lane = lax.broadcasted_iota(jnp.int32, (L, LANES), 1)
    first_head = lane < SSD_HEAD_DIM

    y_parts = []
    for g in range(SSD_GROUPS):
        gs = slice(g * SSD_GROUP_WIDTH, (g + 1) * SSD_GROUP_WIDTH)
        bg = bm[:, g * SSD_STATE:(g + 1) * SSD_STATE]
        cg = cm[:, g * SSD_STATE:(g + 1) * SSD_STATE].astype(BF16)
        cb = _dot_nt(cg, bg.astype(BF16))
        state = state_ref[g]
        y_off = _dot(cg, state.astype(BF16)) * decay_in_w[:, gs]
        y_diag = []
        for j in range(SSD_HEADS_PER_GROUP // 2):
            h0 = g * SSD_HEADS_PER_GROUP + 2 * j
            ms = []
            for h in (h0, h0 + 1):
                seg = acs[:, h:h + 1] - acs_t[h:h + 1, :]
                ms.append(cb * jnp.exp(jnp.where(causal, seg, -jnp.inf)))
            lhs = jnp.concatenate(ms, axis=1).astype(BF16)
            xp = xdt[:, h0 * SSD_HEAD_DIM:(h0 + 2) * SSD_HEAD_DIM]
            rhs = jnp.concatenate([jnp.where(first_head, xp, 0.0), jnp.where(first_head, 0.0, xp)], axis=0)
            y_diag.append(_dot(lhs, rhs.astype(BF16)))
        y_parts.append(jnp.concatenate(y_diag, axis=1) + y_off)
        contrib = _dot(bg.T.astype(BF16), (xdt[:, gs] * decay_out_w[:, gs]).astype(BF16))
        state_ref[g] = state * chunk_decay_w[:, gs] + contrib

    y = jnp.concatenate(y_parts, axis=1) + xs * dskip_ref[...]
    y = y * _silu(z_ref[...])
    o_ref[...] = (_rms(y) * ng_ref[...]).astype(o_ref.dtype)


def _ssd_mixer(proj, dt_raw, conv_w, conv_b, dt_bias, a_log, d_skip, norm_g):
    b, s, _ = proj.shape
    L = SSD_CHUNK
    z_blk = 0
    xs_blk = SSD_WIDTH // SSD_WIDTH
    bm_blk = (2 * SSD_WIDTH) // SSD_BC_WIDTH
    cm_blk = bm_blk + 1
    pad = LANES - SSD_HEADS
    small = lambda a: pl.BlockSpec(a.shape, lambda bi, c: (0, 0))
    dt_bias_p = jnp.pad(dt_bias, (0, pad)).reshape(1, LANES)
    a_log_p = jnp.pad(a_log, (0, pad)).reshape(1, LANES)
    d_skip_w = jnp.repeat(d_skip, SSD_HEAD_DIM).reshape(1, SSD_WIDTH)
    conv_b2 = conv_b.reshape(1, SSD_CONV_CH)
    norm_g2 = norm_g.reshape(1, SSD_WIDTH)
    return pl.pallas_call(
        _ssd_body,
        grid=(b, s // L),
        in_specs=[
            pl.BlockSpec((None, L, SSD_WIDTH), lambda bi, c: (bi, c, xs_blk)),
            pl.BlockSpec((None, L, SSD_BC_WIDTH), lambda bi, c: (bi, c, bm_blk)),
            pl.BlockSpec((None, L, SSD_BC_WIDTH), lambda bi, c: (bi, c, cm_blk)),
            pl.BlockSpec((None, L, SSD_WIDTH), lambda bi, c: (bi, c, z_blk)),
            pl.BlockSpec((None, L, LANES), lambda bi, c: (bi, c, 0)),
            small(conv_w), small(conv_b2), small(dt_bias_p), small(a_log_p), small(d_skip_w), small(norm_g2),
        ],
        out_specs=pl.BlockSpec((None, L, SSD_WIDTH), lambda bi, c: (bi, c, 0)),
        out_shape=jax.ShapeDtypeStruct((b, s, SSD_WIDTH), BF16),
        scratch_shapes=[
            pltpu.VMEM((SUBLANES, SSD_CONV_CH), F32),
            pltpu.VMEM((SSD_GROUPS, SSD_STATE, SSD_GROUP_WIDTH), F32),
        ],
        compiler_params=_params("parallel", "arbitrary"),
        name="ssd_mixer",
    )(proj, proj, proj, proj, dt_raw, conv_w, conv_b2, dt_bias_p, a_log_p, d_skip_w, norm_g2)


def _attn_body(q_ref, k_ref, v_ref, cos_ref, sin_ref, o_ref, qr_ref, kr_ref, op_ref, lp_ref, *, seq):
    blk = ATT_BLOCK
    half = ATT_HEAD_DIM // 2
    scale = ATT_HEAD_DIM ** -0.5
    cos = cos_ref[...]
    sin = sin_ref[...]
    q = q_ref[...]
    k = k_ref[...]
    qr_ref[...] = q * cos + pltpu.roll(q, half, 1) * sin
    kr_ref[...] = k * cos + pltpu.roll(k, half, 1) * sin

    row = lax.broadcasted_iota(jnp.int32, (blk, blk), 0)
    col = lax.broadcasted_iota(jnp.int32, (blk, blk), 1)
    cur_ok = col <= row
    prev_ok = col >= row

    def attend(p, q_rows, cur_rows, prev_rows):
        qb = qr_ref[q_rows, :].astype(BF16)
        s = jnp.where(cur_ok, _dot_nt(qb, kr_ref[cur_rows, :].astype(BF16)) * scale, -jnp.inf)
        vv = v_ref[cur_rows, :].astype(BF16)
        if prev_rows is not None:
            sp = jnp.where(prev_ok, _dot_nt(qb, kr_ref[prev_rows, :].astype(BF16)) * scale, -jnp.inf)
            s = jnp.concatenate([sp, s], axis=1)
            vv = jnp.concatenate([v_ref[prev_rows, :].astype(BF16), vv], axis=0)
        m = jnp.max(s, axis=1, keepdims=True)
        e = jnp.exp(s - m)
        l = jnp.sum(e, axis=1, keepdims=True)
        op_ref[p, q_rows, :] = _dot(e.astype(BF16), vv) / l
        lp_ref[p, q_rows, :] = jnp.broadcast_to(m + jnp.log(l), (blk, LANES))

    for p, (window, dil) in enumerate(DILATED_PATTERNS):
        assert window // dil == blk
        sub_blocks = seq // dil // blk

        def rows(r, n, dil=dil):
            start = r + n * (blk * dil)
            if dil > 1:
                return pl.ds(start, blk, stride=dil)
            return pl.ds(start if isinstance(start, int) else pl.multiple_of(start, blk), blk)

        def residue(r, carry, p=p, rows=rows, sub_blocks=sub_blocks):
            attend(p, rows(r, 0), rows(r, 0), None)

            def later(n, c):
                attend(p, rows(r, n), rows(r, n), rows(r, n - 1))
                return c

            if sub_blocks > 1:
                lax.fori_loop(1, sub_blocks, later, 0)
            return carry

        if dil == 1:
            residue(0, 0)
        else:
            lax.fori_loop(0, dil, residue, 0)

    lses = [lp_ref[p] for p in range(len(DILATED_PATTERNS))]
    m = functools.reduce(jnp.maximum, lses)
    ws = [jnp.exp(l - m) for l in lses]
    num = functools.reduce(lambda a, b: a + b, [w * op_ref[p] for p, w in enumerate(ws)])
    den = functools.reduce(lambda a, b: a + b, ws)
    o_ref[...] = (num / den).astype(o_ref.dtype)


def _dilated_attention(proj, cos2, sin2):
    b, s, _ = proj.shape
    q_blk = (2 * SSD_WIDTH + 2 * SSD_BC_WIDTH) // ATT_HEAD_DIM
    k_blk = q_blk + ATT_HEADS
    v_blk = k_blk + ATT_HEADS
    n_pat = len(DILATED_PATTERNS)
    head = lambda base: pl.BlockSpec((None, s, ATT_HEAD_DIM), lambda bi, h: (bi, 0, base + h))
    table = pl.BlockSpec((None, s, ATT_HEAD_DIM), lambda bi, h: (bi, 0, 0))
    return pl.pallas_call(
        functools.partial(_attn_body, seq=s),
        grid=(b, ATT_HEADS),
        in_specs=[head(q_blk), head(k_blk), head(v_blk), table, table],
        out_specs=pl.BlockSpec((None, s, ATT_HEAD_DIM), lambda bi, h: (bi, 0, h)),
        out_shape=jax.ShapeDtypeStruct((b, s, ATT_WIDTH), BF16),
        scratch_shapes=[
            pltpu.VMEM((s, ATT_HEAD_DIM), F32),
            pltpu.VMEM((s, ATT_HEAD_DIM), F32),
            pltpu.VMEM((n_pat, s, ATT_HEAD_DIM), F32),
            pltpu.VMEM((n_pat, s, LANES), F32),
        ],
        compiler_params=_params("parallel", "arbitrary"),
        name="dilated_attention",
    )(proj, proj, proj, cos2, sin2)


def _sgu_body(u_ref, v_ref, g_ref, b_ref, ws_ref, bs_ref, o_ref):
    L = SGU_CHUNK
    v = v_ref[...]
    mu = jnp.mean(v, axis=-1, keepdims=True)
    vc = v - mu
    var = jnp.mean(vc * vc, axis=-1, keepdims=True)
    vn = (vc * lax.rsqrt(var + NORM_EPS) * g_ref[...] + b_ref[...]).astype(BF16)
    row = lax.broadcasted_iota(jnp.int32, (L, L), 0)
    col = lax.broadcasted_iota(jnp.int32, (L, L), 1)
    causal = row >= col
    for g in range(SGU_GROUPS):
        gs = slice(g * SGU_GROUP_WIDTH, (g + 1) * SGU_GROUP_WIDTH)
        w = jnp.where(causal, ws_ref[g], 0.0).astype(BF16)
        mixed = _dot(w, vn[:, gs]) + bs_ref[:, g:g + 1]
        o_ref[:, gs] = (u_ref[:, gs] * mixed).astype(o_ref.dtype)


def _sgu_gate(zz, ln_g, ln_b, w_spatial, b_spatial):
    b, s, _ = zz.shape
    L = SGU_CHUNK
    vec = pl.BlockSpec((1, SGU_WIDTH), lambda bi, c: (0, 0))
    return pl.pallas_call(
        _sgu_body,
        grid=(b, s // L),
        in_specs=[
            pl.BlockSpec((None, L, SGU_WIDTH), lambda bi, c: (bi, c, 0)),
            pl.BlockSpec((None, L, SGU_WIDTH), lambda bi, c: (bi, c, 1)),
            vec, vec,
            pl.BlockSpec((SGU_GROUPS, L, L), lambda bi, c: (0, 0, 0)),
            pl.BlockSpec((L, SGU_GROUPS), lambda bi, c: (0, 0)),
        ],
        out_specs=pl.BlockSpec((None, L, SGU_WIDTH), lambda bi, c: (bi, c, 0)),
        out_shape=jax.ShapeDtypeStruct((b, s, SGU_WIDTH), BF16),
        compiler_params=_params("parallel", "parallel"),
        name="sgu_gate",
    )(zz, zz, ln_g.reshape(1, SGU_WIDTH), ln_b.reshape(1, SGU_WIDTH), w_spatial, b_spatial.T)


def kernel(x, c, positions, w_mod, b_mod, norm_pre, norm_post, ffn_w_gate, ffn_w_up, ffn_w_down, hyb_w_in, hyb_conv_w, hyb_conv_b, hyb_dt_bias, hyb_a_log, hyb_d_skip, hyb_norm_g, hyb_w_out, sgu_w_in, sgu_b_in, sgu_ln_g, sgu_ln_b, sgu_w_spatial, sgu_b_spatial, sgu_w_out):
    depth = w_mod.shape[0]
    b, s, d = x.shape
    n_sub = norm_pre.shape[1]

    mod = _modulation(c, w_mod, b_mod).reshape(depth, b, n_sub, 3, d)
    g_pre = norm_pre.reshape(depth, n_sub, 1, d)
    g_post = norm_post.reshape(depth, n_sub, 1, d)
    w_gate = ffn_w_gate.astype(BF16)
    w_up = ffn_w_up.astype(BF16)
    w_down = ffn_w_down.astype(BF16)

    for layer in range(depth):
        i = layer // 2
        x = _ffn_sublayer(x, mod, g_pre, g_post, w_gate, w_up, w_down, layer, 0, 0)
        if layer % 2 == 0:
            w_in = hyb_w_in[i]
            dt_lo = SSD_WIDTH + SSD_CONV_CH
            dt_hi = dt_lo + SSD_HEADS
            w_main = jnp.concatenate([w_in[:, :dt_lo], w_in[:, dt_hi:]], axis=1).astype(BF16)
            w_dt = jnp.pad(w_in[:, dt_lo:dt_hi], ((0, 0), (0, LANES - SSD_HEADS))).astype(BF16)
            proj = _pre_norm_proj(x, mod, g_pre, w_main, None, layer, 1, PROJ_TILE)
            dt_raw = _pre_norm_proj(x, mod, g_pre, w_dt, None, layer, 1, LANES)
            cos2, sin2 = _rope_tables(positions)
            y_a = _ssd_mixer(proj, dt_raw, hyb_conv_w[i], hyb_conv_b[i], hyb_dt_bias[i], hyb_a_log[i],
                             hyb_d_skip[i], hyb_norm_g[i])
            y_b = _dilated_attention(proj, cos2, sin2)
            x = _out_proj_residual(y_a, 0, y_b, 0, hyb_w_out[i].astype(BF16), x, mod, g_post, layer, 1)
        else:
            zz = _pre_norm_proj(x, mod, g_pre, sgu_w_in[i].astype(BF16), sgu_b_in[i].reshape(1, -1), layer, 1,
                                PROJ_TILE, gelu=True)
            gated = _sgu_gate(zz, sgu_ln_g[i], sgu_ln_b[i], sgu_w_spatial[i], sgu_b_spatial[i])
            x = _out_proj_residual(gated, 0, gated, 1, sgu_w_out[i].astype(BF16), x, mod, g_post, layer, 1)
        x = _ffn_sublayer(x, mod, g_pre, g_post, w_gate, w_up, w_down, layer, 2, 1)
    return x
```

```python
import functools

import jax
import jax.numpy as jnp
from jax import lax
from jax.experimental import pallas as pl
from jax.experimental.pallas import tpu as pltpu

NORM_EPS = 1e-6
FFN_RES_WEIGHT = 0.5
MIXER_RES_WEIGHT = 1.0

SSD_HEADS = 32
SSD_HEAD_DIM = 64
SSD_WIDTH = SSD_HEADS * SSD_HEAD_DIM
SSD_GROUPS = 4
SSD_STATE = 128
SSD_CONV = 4
SSD_CHUNK = 128
SSD_BC_WIDTH = SSD_GROUPS * SSD_STATE
SSD_CONV_CH = SSD_WIDTH + 2 * SSD_BC_WIDTH
SSD_HEADS_PER_GROUP = SSD_HEADS // SSD_GROUPS
SSD_GROUP_WIDTH = SSD_HEADS_PER_GROUP * SSD_HEAD_DIM

ATT_HEADS = 16
ATT_HEAD_DIM = 128
ATT_WIDTH = ATT_HEADS * ATT_HEAD_DIM
ATT_BLOCK = 128
DILATED_PATTERNS = ((128, 1), (512, 4), (2048, 16))
ROPE_THETA = 10000.0

SGU_WIDTH = 4096
SGU_GROUPS = 8
SGU_CHUNK = 128
SGU_GROUP_WIDTH = SGU_WIDTH // SGU_GROUPS

LANES = 128
SUBLANES = 8
VMEM_LIMIT_BYTES = 56 * 1024 * 1024

ROW_TILE = 512
FFN_TILE = 512
PROJ_ROWS = 1024
PROJ_TILE = 1024
MOD_TILE = 1024

BF16 = jnp.bfloat16
F32 = jnp.float32


def _params(*semantics):
    return pltpu.CompilerParams(dimension_semantics=semantics, vmem_limit_bytes=VMEM_LIMIT_BYTES)


def _rms(x):
    return x * lax.rsqrt(jnp.mean(x * x, axis=-1, keepdims=True) + NORM_EPS)


def _silu(x):
    return x * jax.nn.sigmoid(x)


def _dot(a, b):
    return jnp.dot(a, b, preferred_element_type=F32)


def _dot_exact(a, b):
    return jnp.dot(a, b, preferred_element_type=F32, precision=lax.Precision.HIGHEST)


def _dot_nt(a, b):
    return lax.dot_general(a, b, (((1,), (1,)), ((), ())), preferred_element_type=F32)


def _pre_norm(x_ref, mod_ref, gpre_ref):
    return _rms(x_ref[...]) * gpre_ref[...] * (1.0 + mod_ref[1:2, :]) + mod_ref[0:1, :]


def _post_residual(x_ref, y, mod_ref, gpost_ref, res_weight):
    return x_ref[...] + (res_weight * (1.0 + mod_ref[2:3, :])) * (_rms(y) * gpost_ref[...])


def _mod_body(c_ref, w_ref, b_ref, o_ref):
    ca = _silu(c_ref[...]).astype(BF16)
    o_ref[...] = _dot(ca, w_ref[...].astype(BF16)) + b_ref[...]


def _modulation(c, w_mod, b_mod):
    depth, d, n = w_mod.shape
    b = c.shape[0]
    return pl.pallas_call(
        _mod_body,
        grid=(depth, n // MOD_TILE),
        in_specs=[
            pl.BlockSpec((b, d), lambda l, j: (0, 0)),
            pl.BlockSpec((None, d, MOD_TILE), lambda l, j: (l, 0, j)),
            pl.BlockSpec((None, 1, MOD_TILE), lambda l, j: (l, 0, j)),
        ],
        out_specs=pl.BlockSpec((None, b, MOD_TILE), lambda l, j: (l, 0, j)),
        out_shape=jax.ShapeDtypeStruct((depth, b, n), F32),
        compiler_params=_params("parallel", "parallel"),
        name="modulation",
    )(c, w_mod, b_mod.reshape(depth, 1, n))


def _ffn_body(x_ref, mod_ref, gpre_ref, gpost_ref, wg_ref, wu_ref, wd_ref, o_ref, h_ref, acc_ref, *, n_f):
    f = pl.program_id(2)

    @pl.when(f == 0)
    def _():
        h_ref[...] = _pre_norm(x_ref, mod_ref, gpre_ref).astype(BF16)
        acc_ref[...] = jnp.zeros_like(acc_ref)

    h = h_ref[...]
    a = (_silu(_dot(h, wg_ref[...])) * _dot(h, wu_ref[...])).astype(BF16)
    acc_ref[...] += _dot(a, wd_ref[...])

    @pl.when(f == n_f - 1)
    def _():
        o_ref[...] = _post_residual(x_ref, acc_ref[...], mod_ref, gpost_ref, FFN_RES_WEIGHT)


def _ffn_sublayer(x, mod, g_pre, g_post, w_gate, w_up, w_down, layer, sub, idx):
    b, s, d = x.shape
    f_dim = w_gate.shape[-1]
    n_f = f_dim // FFN_TILE
    row = lambda bi, i, f: (bi, i, 0)
    return pl.pallas_call(
        functools.partial(_ffn_body, n_f=n_f),
        grid=(b, s // ROW_TILE, n_f),
        in_specs=[
            pl.BlockSpec((None, ROW_TILE, d), row),
            pl.BlockSpec((None, None, None, 3, d), lambda bi, i, f: (layer, bi, sub, 0, 0)),
            pl.BlockSpec((None, None, 1, d), lambda bi, i, f: (layer, sub, 0, 0)),
            pl.BlockSpec((None, None, 1, d), lambda bi, i, f: (layer, sub, 0, 0)),
            pl.BlockSpec((None, None, d, FFN_TILE), lambda bi, i, f: (layer, idx, 0, f)),
            pl.BlockSpec((None, None, d, FFN_TILE), lambda bi, i, f: (layer, idx, 0, f)),
            pl.BlockSpec((None, None, FFN_TILE, d), lambda bi, i, f: (layer, idx, f, 0)),
        ],
        out_specs=pl.BlockSpec((None, ROW_TILE, d), row),
        out_shape=jax.ShapeDtypeStruct(x.shape, F32),
        scratch_shapes=[pltpu.VMEM((ROW_TILE, d), BF16), pltpu.VMEM((ROW_TILE, d), F32)],
        compiler_params=_params("parallel", "parallel", "arbitrary"),
        name="ffn_sublayer",
    )(x, mod, g_pre, g_post, w_gate, w_up, w_down)


def _gelu_tanh(x):
    return 0.5 * x * (1.0 + jnp.tanh(0.7978845608028654 * (x + 0.044715 * (x * x * x))))


def _proj_body(x_ref, mod_ref, gpre_ref, w_ref, *rest, gelu):
    if gelu:
        b_ref, o_ref, h_ref = rest
    else:
        o_ref, h_ref = rest

    @pl.when(pl.program_id(2) == 0)
    def _():
        h_ref[...] = _pre_norm(x_ref, mod_ref, gpre_ref).astype(BF16)

    y = _dot(h_ref[...], w_ref[...])
    if gelu:
        y = _gelu_tanh(y + b_ref[...])
    o_ref[...] = y.astype(o_ref.dtype)


def _pre_norm_proj(x, mod, g_pre, w, bias, layer, sub, tile, out_dtype, gelu=False):
    b, s, d = x.shape
    n = w.shape[-1]
    in_specs = [
        pl.BlockSpec((None, PROJ_ROWS, d), lambda bi, i, j: (bi, i, 0)),
        pl.BlockSpec((None, None, None, 3, d), lambda bi, i, j: (layer, bi, sub, 0, 0)),
        pl.BlockSpec((None, None, 1, d), lambda bi, i, j: (layer, sub, 0, 0)),
        pl.BlockSpec((d, tile), lambda bi, i, j: (0, j)),
    ]
    args = [x, mod, g_pre, w]
    if gelu:
        in_specs.append(pl.BlockSpec((1, tile), lambda bi, i, j: (0, j)))
        args.append(bias)
    return pl.pallas_call(
        functools.partial(_proj_body, gelu=gelu),
        grid=(b, s // PROJ_ROWS, n // tile),
        in_specs=in_specs,
        out_specs=pl.BlockSpec((None, PROJ_ROWS, tile), lambda bi, i, j: (bi, i, j)),
        out_shape=jax.ShapeDtypeStruct((b, s, n), out_dtype),
        scratch_shapes=[pltpu.VMEM((PROJ_ROWS, d), BF16)],
        compiler_params=_params("parallel", "parallel", "arbitrary"),
        name="pre_norm_proj",
    )(*args)


def _out_body(a_ref, b_ref, w_ref, x_ref, mod_ref, gpost_ref, o_ref, acc_ref):
    k = pl.program_id(2)

    @pl.when(k == 0)
    def _():
        acc_ref[...] = _dot(a_ref[...], w_ref[...])

    @pl.when(k == 1)
    def _():
        y = acc_ref[...] + _dot(b_ref[...], w_ref[...])
        o_ref[...] = _post_residual(x_ref, y, mod_ref, gpost_ref, MIXER_RES_WEIGHT)


def _out_proj_residual(lhs_a, col_a, lhs_b, col_b, w, x, mod, g_post, layer, sub):
    b, s, d = x.shape
    half = w.shape[0] // 2
    row = lambda bi, i, k: (bi, i, 0)
    return pl.pallas_call(
        _out_body,
        grid=(b, s // ROW_TILE, 2),
        in_specs=[
            pl.BlockSpec((None, ROW_TILE, half), lambda bi, i, k: (bi, i, col_a)),
            pl.BlockSpec((None, ROW_TILE, half), lambda bi, i, k: (bi, i, col_b)),
            pl.BlockSpec((half, d), lambda bi, i, k: (k, 0)),
            pl.BlockSpec((None, ROW_TILE, d), row),
            pl.BlockSpec((None, None, None, 3, d), lambda bi, i, k: (layer, bi, sub, 0, 0)),
            pl.BlockSpec((None, None, 1, d), lambda bi, i, k: (layer, sub, 0, 0)),
        ],
        out_specs=pl.BlockSpec((None, ROW_TILE, d), row),
        out_shape=jax.ShapeDtypeStruct(x.shape, F32),
        scratch_shapes=[pltpu.VMEM((ROW_TILE, d), F32)],
        compiler_params=_params("parallel", "parallel", "arbitrary"),
        name="out_proj_residual",
    )(lhs_a, lhs_b, w, x, mod, g_post)


def _rope_body(pos_ref, freq_ref, cos_ref, sin_ref):
    ang = pos_ref[...].astype(F32) * freq_ref[...]
    lane = lax.broadcasted_iota(jnp.int32, ang.shape, 1)
    cos_ref[...] = jnp.cos(ang)
    sin_ref[...] = jnp.where(lane < ATT_HEAD_DIM // 2, -1.0, 1.0) * jnp.sin(ang)


def _rope_tables(positions):
    b, s = positions.shape
    half = ATT_HEAD_DIM // 2
    inv_freq = ROPE_THETA ** (-jnp.arange(half, dtype=F32) / half)
    freq = jnp.concatenate([inv_freq, inv_freq]).reshape(1, ATT_HEAD_DIM)
    tile = 512
    spec = pl.BlockSpec((None, tile, ATT_HEAD_DIM), lambda bi, i: (bi, i, 0))
    return pl.pallas_call(
        _rope_body,
        grid=(b, s // tile),
        in_specs=[
            pl.BlockSpec((None, tile, 1), lambda bi, i: (bi, i, 0)),
            pl.BlockSpec((1, ATT_HEAD_DIM), lambda bi, i: (0, 0)),
        ],
        out_specs=[spec, spec],
        out_shape=[jax.ShapeDtypeStruct((b, s, ATT_HEAD_DIM), F32)] * 2,
        compiler_params=_params("parallel", "parallel"),
        name="rope_tables",
    )(positions.reshape(b, s, 1), freq)


def _ssd_body(xs_ref, bm_ref, cm_ref, z_ref, dt_ref, cw_ref, cb_ref, dtb_ref, alog_ref, dskip_ref, ng_ref,
              o_ref, tail_ref, state_ref):
    L = SSD_CHUNK

    @pl.when(pl.program_id(1) == 0)
    def _():
        tail_ref[...] = jnp.zeros_like(tail_ref)
        state_ref[...] = jnp.zeros_like(state_ref)

    def conv_silu(raw, lo, hi):
        tail = tail_ref[:, lo:hi]
        row8 = lax.broadcasted_iota(jnp.int32, tail.shape, 0)
        acc = raw * cw_ref[SSD_CONV - 1:SSD_CONV, lo:hi] + cb_ref[:, lo:hi]
        for k in range(1, SSD_CONV):
            rolled = pltpu.roll(raw, k, 0)
            top = jnp.where(row8 < k, pltpu.roll(tail, k, 0), rolled[0:SUBLANES])
            shifted = jnp.concatenate([top, rolled[SUBLANES:]], axis=0)
            acc = acc + shifted * cw_ref[SSD_CONV - 1 - k:SSD_CONV - k, lo:hi]
        tail_ref[:, lo:hi] = raw[L - SUBLANES:L]
        return _silu(acc)

    xs = conv_silu(xs_ref[...].astype(F32), 0, SSD_WIDTH)
    bm = conv_silu(bm_ref[...].astype(F32), SSD_WIDTH, SSD_WIDTH + SSD_BC_WIDTH)
    cm = conv_silu(cm_ref[...].astype(F32), SSD_WIDTH + SSD_BC_WIDTH, SSD_CONV_CH)

    dt_in = dt_ref[...] + dtb_ref[...]
    dt = jnp.maximum(dt_in, 0.0) + jnp.log1p(jnp.exp(-jnp.abs(dt_in)))
    adt = dt * (-jnp.exp(alog_ref[...]))
    row = lax.broadcasted_iota(jnp.int32, (L, L), 0)
    col = lax.broadcasted_iota(jnp.int32, (L, L), 1)
    causal = row >= col
    acs = _dot_exact(causal.astype(F32), adt)
    acs_t = acs.T
    acs_last = acs[L - 1:L, :]

    hrow = lax.broadcasted_iota(jnp.int32, (LANES, SSD_WIDTH), 0)
    hcol = lax.broadcasted_iota(jnp.int32, (LANES, SSD_WIDTH), 1)
    expand = (hrow == (hcol >> (SSD_HEAD_DIM.bit_length() - 1))).astype(F32)
    stacked = jnp.concatenate([dt, jnp.exp(acs), jnp.exp(acs_last - acs)], axis=0)
    wide = _dot_exact(stacked, expand)
    dt_w, decay_in_w, decay_out_w = wide[0:L], wide[L:2 * L], wide[2 * L:3 * L]
    chunk_decay_w = decay_in_w[L - 1:L, :]

    xdt = xs * dt_w
    lane = lax.broadcasted_iota(jnp.int32, (L, LANES), 1)
    first_head = lane < SSD_HEAD_DIM

    y_parts = []
    for g in range(SSD_GROUPS):
        gs = slice(g * SSD_GROUP_WIDTH, (g + 1) * SSD_GROUP_WIDTH)
        bg = bm[:, g * SSD_STATE:(g + 1) * SSD_STATE]
        cg = cm[:, g * SSD_STATE:(g + 1) * SSD_STATE].astype(BF16)
        cb = _dot_nt(cg, bg.astype(BF16))
        state = state_ref[g]
        y_off = _dot(cg, state.astype(BF16)) * decay_in_w[:, gs]
        y_diag = []
        for j in range(SSD_HEADS_PER_GROUP // 2):
            h0 = g * SSD_HEADS_PER_GROUP + 2 * j
            ms = []
            for h in (h0, h0 + 1):
                seg = acs[:, h:h + 1] - acs_t[h:h + 1, :]
                ms.append(cb * jnp.exp(jnp.where(causal, seg, -jnp.inf)))
            lhs = jnp.concatenate(ms, axis=1).astype(BF16)
            xp = xdt[:, h0 * SSD_HEAD_DIM:(h0 + 2) * SSD_HEAD_DIM]
            rhs = jnp.concatenate([jnp.where(first_head, xp, 0.0), jnp.where(first_head, 0.0, xp)], axis=0)
            y_diag.append(_dot(lhs, rhs.astype(BF16)))
        y_parts.append(jnp.concatenate(y_diag, axis=1) + y_off)
        contrib = _dot(bg.T.astype(BF16), (xdt[:, gs] * decay_out_w[:, gs]).astype(BF16))
        state_ref[g] = state * chunk_decay_w[:, gs] + contrib

    y = jnp.concatenate(y_parts, axis=1) + xs * dskip_ref[...]
    y = y * _silu(z_ref[...].astype(F32))
    o_ref[...] = (_rms(y) * ng_ref[...]).astype(o_ref.dtype)


def _ssd_mixer(proj, dt_raw, conv_w, conv_b, dt_bias, a_log, d_skip, norm_g):
    b, s, _ = proj.shape
    L = SSD_CHUNK
    z_blk = 0
    xs_blk = SSD_WIDTH // SSD_WIDTH
    bm_blk = (2 * SSD_WIDTH) // SSD_BC_WIDTH
    cm_blk = bm_blk + 1
    pad = LANES - SSD_HEADS
    small = lambda a: pl.BlockSpec(a.shape, lambda bi, c: (0, 0))
    dt_bias_p = jnp.pad(dt_bias, (0, pad)).reshape(1, LANES)
    a_log_p = jnp.pad(a_log, (0, pad)).reshape(1, LANES)
    d_skip_w = jnp.repeat(d_skip, SSD_HEAD_DIM).reshape(1, SSD_WIDTH)
    conv_b2 = conv_b.reshape(1, SSD_CONV_CH)
    norm_g2 = norm_g.reshape(1, SSD_WIDTH)
    return pl.pallas_call(
        _ssd_body,
        grid=(b, s // L),
        in_specs=[
            pl.BlockSpec((None, L, SSD_WIDTH), lambda bi, c: (bi, c, xs_blk)),
            pl.BlockSpec((None, L, SSD_BC_WIDTH), lambda bi, c: (bi, c, bm_blk)),
            pl.BlockSpec((None, L, SSD_BC_WIDTH), lambda bi, c: (bi, c, cm_blk)),
            pl.BlockSpec((None, L, SSD_WIDTH), lambda bi, c: (bi, c, z_blk)),
            pl.BlockSpec((None, L, LANES), lambda bi, c: (bi, c, 0)),
            small(conv_w), small(conv_b2), small(dt_bias_p), small(a_log_p), small(d_skip_w), small(norm_g2),
        ],
        out_specs=pl.BlockSpec((None, L, SSD_WIDTH), lambda bi, c: (bi, c, 0)),
        out_shape=jax.ShapeDtypeStruct((b, s, SSD_WIDTH), BF16),
        scratch_shapes=[
            pltpu.VMEM((SUBLANES, SSD_CONV_CH), F32),
            pltpu.VMEM((SSD_GROUPS, SSD_STATE, SSD_GROUP_WIDTH), F32),
        ],
        compiler_params=_params("parallel", "arbitrary"),
        name="ssd_mixer",
    )(proj, proj, proj, proj, dt_raw, conv_w, conv_b2, dt_bias_p, a_log_p, d_skip_w, norm_g2)


def _attn_body(q_ref, k_ref, v_ref, cos_ref, sin_ref, o_ref, qkv_ref, acc_ref, m_ref, l_ref, *, seq):
    blk = ATT_BLOCK
    half = ATT_HEAD_DIM // 2
    dils = [d for _, d in DILATED_PATTERNS]
    step = dils[1]
    assert dils == [1, step, step * step] and all(w // d == blk for w, d in DILATED_PATTERNS)
    sub = seq // step
    assert seq // dils[2] == blk

    cos = cos_ref[...]
    sin = sin_ref[...]
    q = q_ref[...].astype(F32)
    k = k_ref[...].astype(F32)
    qkv_ref[0, 0] = (q * cos + pltpu.roll(q, half, 1) * sin) * (ATT_HEAD_DIM ** -0.5)
    qkv_ref[0, 1] = k * cos + pltpu.roll(k, half, 1) * sin
    qkv_ref[0, 2] = v_ref[...].astype(F32)
    for t in range(3):
        for r in range(step):
            qkv_ref[1, t, pl.ds(r * sub, sub), :] = qkv_ref[0, t, pl.ds(r, sub, stride=step), :]
    for t in range(3):
        for r in range(step):
            for a in range(step):
                qkv_ref[2, t, pl.ds((r + step * a) * blk, blk), :] = qkv_ref[1, t, pl.ds(r * sub + a, blk, stride=step), :]

    row = lax.broadcasted_iota(jnp.int32, (blk, blk), 0)
    col = lax.broadcasted_iota(jnp.int32, (blk, blk), 1)
    cur_ok = col <= row
    prev_ok = col >= row

    def attend(p, starts, with_prev):
        def load(t, st):
            return qkv_ref[p, t, pl.ds(st, blk), :].astype(BF16)

        def window(t, st):
            return jnp.concatenate([load(t, st - blk), load(t, st)], axis=0) if with_prev else load(t, st)

        qb = jnp.stack([load(0, st) for st in starts])
        kk = jnp.stack([window(1, st) for st in starts])
        vv = jnp.stack([window(2, st) for st in starts])
        ok = jnp.concatenate([prev_ok, cur_ok], axis=1) if with_prev else cur_ok
        s = lax.dot_general(qb, kk, (((2,), (2,)), ((0,), (0,))), preferred_element_type=F32)
        s = jnp.where(ok[None], s, -jnp.inf)
        m = jnp.max(s, axis=2, keepdims=True)
        e = jnp.exp(s - m)
        l = jnp.sum(e, axis=2, keepdims=True)
        acc = lax.dot_general(e.astype(BF16), vv, (((2,), (1,)), ((0,), (0,))), preferred_element_type=F32)
        for i, st in enumerate(starts):
            rows = pl.ds(st, blk)
            acc_ref[p, rows, :] = acc[i]
            m_ref[p, rows, :] = jnp.broadcast_to(m[i], (blk, LANES))
            l_ref[p, rows, :] = jnp.broadcast_to(l[i], (blk, LANES))

    def groups(starts, size):
        return [starts[i:i + size] for i in range(0, len(starts), size)]

    for p, dil in enumerate(dils):
        class_rows = seq // dil
        firsts = [r * class_rows for r in range(dil)]
        laters = [r * class_rows + n * blk for r in range(dil) for n in range(1, class_rows // blk)]
        for g in groups(firsts, 4):
            attend(p, g, False)
        for g in groups(laters, 3):
            attend(p, g, True)

    def merged(dst, dst_rows, src, src_rows):
        m_a, m_b = m_ref[dst, dst_rows, :], m_ref[src, src_rows, :]
        top = jnp.maximum(m_a, m_b)
        w_a, w_b = jnp.exp(m_a - top), jnp.exp(m_b - top)
        acc = w_a * acc_ref[dst, dst_rows, :] + w_b * acc_ref[src, src_rows, :]
        return top, acc, w_a * l_ref[dst, dst_rows, :] + w_b * l_ref[src, src_rows, :]

    for r in range(step):
        for a in range(step):
            mid_rows = pl.ds(r * sub + a, blk, stride=step)
            top, acc, l = merged(1, mid_rows, 2, pl.ds((r + step * a) * blk, blk))
            m_ref[1, mid_rows, :] = top
            acc_ref[1, mid_rows, :] = acc
            l_ref[1, mid_rows, :] = l
    for r in range(step):
        for n in range(sub // blk):
            nat_rows = pl.ds(r + n * blk * step, blk, stride=step)
            _, acc, l = merged(0, nat_rows, 1, pl.ds(r * sub + n * blk, blk))
            acc_ref[0, nat_rows, :] = acc / l
    o_ref[...] = acc_ref[0].astype(o_ref.dtype)


def _dilated_attention(proj, cos2, sin2):
    b, s, _ = proj.shape
    q_blk = (2 * SSD_WIDTH + 2 * SSD_BC_WIDTH) // ATT_HEAD_DIM
    k_blk = q_blk + ATT_HEADS
    v_blk = k_blk + ATT_HEADS
    n_pat = len(DILATED_PATTERNS)
    head = lambda base: pl.BlockSpec((None, s, ATT_HEAD_DIM), lambda bi, h: (bi, 0, base + h))
    table = pl.BlockSpec((None, s, ATT_HEAD_DIM), lambda bi, h: (bi, 0, 0))
    return pl.pallas_call(
        functools.partial(_attn_body, seq=s),
        grid=(b, ATT_HEADS),
        in_specs=[head(q_blk), head(k_blk), head(v_blk), table, table],
        out_specs=pl.BlockSpec((None, s, ATT_HEAD_DIM), lambda bi, h: (bi, 0, h)),
        out_shape=jax.ShapeDtypeStruct((b, s, ATT_WIDTH), BF16),
        scratch_shapes=[
            pltpu.VMEM((n_pat, 3, s, ATT_HEAD_DIM), F32),
            pltpu.VMEM((n_pat, s, ATT_HEAD_DIM), F32),
            pltpu.VMEM((n_pat, s, LANES), F32),
            pltpu.VMEM((n_pat, s, LANES), F32),
        ],
        compiler_params=_params("parallel", "arbitrary"),
        name="dilated_attention",
    )(proj, proj, proj, cos2, sin2)


def _sgu_body(u_ref, v_ref, g_ref, b_ref, ws_ref, bs_ref, o_ref):
    L = SGU_CHUNK
    v = v_ref[...].astype(F32)
    mu = jnp.mean(v, axis=-1, keepdims=True)
    vc = v - mu
    var = jnp.mean(vc * vc, axis=-1, keepdims=True)
    vn = (vc * lax.rsqrt(var + NORM_EPS) * g_ref[...] + b_ref[...]).astype(BF16)
    row = lax.broadcasted_iota(jnp.int32, (L, L), 0)
    col = lax.broadcasted_iota(jnp.int32, (L, L), 1)
    causal = row >= col
    for g in range(SGU_GROUPS):
        gs = slice(g * SGU_GROUP_WIDTH, (g + 1) * SGU_GROUP_WIDTH)
        w = jnp.where(causal, ws_ref[g], 0.0).astype(BF16)
        mixed = _dot(w, vn[:, gs]) + bs_ref[:, g:g + 1]
        o_ref[:, gs] = (u_ref[:, gs].astype(F32) * mixed).astype(o_ref.dtype)


def _sgu_gate(zz, ln_g, ln_b, w_spatial, b_spatial):
    b, s, _ = zz.shape
    L = SGU_CHUNK
    vec = pl.BlockSpec((1, SGU_WIDTH), lambda bi, c: (0, 0))
    return pl.pallas_call(
        _sgu_body,
        grid=(b, s // L),
        in_specs=[
            pl.BlockSpec((None, L, SGU_WIDTH), lambda bi, c: (bi, c, 0)),
            pl.BlockSpec((None, L, SGU_WIDTH), lambda bi, c: (bi, c, 1)),
            vec, vec,
            pl.BlockSpec((SGU_GROUPS, L, L), lambda bi, c: (0, 0, 0)),
            pl.BlockSpec((L, SGU_GROUPS), lambda bi, c: (0, 0)),
        ],
        out_specs=pl.BlockSpec((None, L, SGU_WIDTH), lambda bi, c: (bi, c, 0)),
        out_shape=jax.ShapeDtypeStruct((b, s, SGU_WIDTH), BF16),
        compiler_params=_params("parallel", "parallel"),
        name="sgu_gate",
    )(zz, zz, ln_g.reshape(1, SGU_WIDTH), ln_b.reshape(1, SGU_WIDTH), w_spatial, b_spatial.T)


def kernel(x, c, positions, w_mod, b_mod, norm_pre, norm_post, ffn_w_gate, ffn_w_up, ffn_w_down, hyb_w_in, hyb_conv_w, hyb_conv_b, hyb_dt_bias, hyb_a_log, hyb_d_skip, hyb_norm_g, hyb_w_out, sgu_w_in, sgu_b_in, sgu_ln_g, sgu_ln_b, sgu_w_spatial, sgu_b_spatial, sgu_w_out):
    depth = w_mod.shape[0]
    b, s, d = x.shape
    n_sub = norm_pre.shape[1]

    mod = _modulation(c, w_mod, b_mod).reshape(depth, b, n_sub, 3, d)
    g_pre = norm_pre.reshape(depth, n_sub, 1, d)
    g_post = norm_post.reshape(depth, n_sub, 1, d)
    w_gate = ffn_w_gate.astype(BF16)
    w_up = ffn_w_up.astype(BF16)
    w_down = ffn_w_down.astype(BF16)

    for layer in range(depth):
        i = layer // 2
        x = _ffn_sublayer(x, mod, g_pre, g_post, w_gate, w_up, w_down, layer, 0, 0)
        if layer % 2 == 0:
            w_in = hyb_w_in[i]
            dt_lo = SSD_WIDTH + SSD_CONV_CH
            dt_hi = dt_lo + SSD_HEADS
            w_main = jnp.concatenate([w_in[:, :dt_lo].astype(BF16), w_in[:, dt_hi:].astype(BF16)], axis=1)
            w_dt = jnp.pad(w_in[:, dt_lo:dt_hi], ((0, 0), (0, LANES - SSD_HEADS))).astype(BF16)
            proj = _pre_norm_proj(x, mod, g_pre, w_main, None, layer, 1, PROJ_TILE, BF16)
            dt_raw = _pre_norm_proj(x, mod, g_pre, w_dt, None, layer, 1, LANES, F32)
            cos2, sin2 = _rope_tables(positions)
            y_a = _ssd_mixer(proj, dt_raw, hyb_conv_w[i], hyb_conv_b[i], hyb_dt_bias[i], hyb_a_log[i],
                             hyb_d_skip[i], hyb_norm_g[i])
            y_b = _dilated_attention(proj, cos2, sin2)
            x = _out_proj_residual(y_a, 0, y_b, 0, hyb_w_out[i].astype(BF16), x, mod, g_post, layer, 1)
        else:
            zz = _pre_norm_proj(x, mod, g_pre, sgu_w_in[i].astype(BF16), sgu_b_in[i].reshape(1, -1), layer, 1,
                                PROJ_TILE, BF16, gelu=True)
            gated = _sgu_gate(zz, sgu_ln_g[i], sgu_ln_b[i], sgu_w_spatial[i], sgu_b_spatial[i])
            x = _out_proj_residual(gated, 0, gated, 1, sgu_w_out[i].astype(BF16), x, mod, g_post, layer, 1)
        x = _ffn_sublayer(x, mod, g_pre, g_post, w_gate, w_up, w_down, layer, 2, 1)
    return x
```

```python
import functools

import jax
import jax.numpy as jnp
from jax import lax
from jax.experimental import pallas as pl
from jax.experimental.pallas import tpu as pltpu

NORM_EPS = 1e-6
LOG2_E = 1.4426950408889634
FFN_RES_WEIGHT = 0.5
MIXER_RES_WEIGHT = 1.0

SSD_HEADS = 32
SSD_HEAD_DIM = 64
SSD_WIDTH = SSD_HEADS * SSD_HEAD_DIM
SSD_GROUPS = 4
SSD_STATE = 128
SSD_CONV = 4
SSD_CHUNK = 128
SSD_TAIL = 16
SSD_BC_WIDTH = SSD_GROUPS * SSD_STATE
SSD_CONV_CH = SSD_WIDTH + 2 * SSD_BC_WIDTH
SSD_HEADS_PER_GROUP = SSD_HEADS // SSD_GROUPS
SSD_GROUP_WIDTH = SSD_HEADS_PER_GROUP * SSD_HEAD_DIM

ATT_HEADS = 16
ATT_HEAD_DIM = 128
ATT_WIDTH = ATT_HEADS * ATT_HEAD_DIM
ATT_BLOCK = 128
DILATED_PATTERNS = ((128, 1), (512, 4), (2048, 16))
ROPE_THETA = 10000.0

SGU_WIDTH = 4096
SGU_GROUPS = 8
SGU_CHUNK = 128
SGU_GROUP_WIDTH = SGU_WIDTH // SGU_GROUPS

LANES = 128
SUBLANES = 8
VMEM_LIMIT_BYTES = 56 * 1024 * 1024

ROW_TILE = 512
FFN_TILE = 512
PROJ_ROWS = 1024
PROJ_TILE = 1024
MOD_TILE = 1024
ATT_GROUP_FIRST = 8
ATT_GROUP_LATER = 5

BF16 = jnp.bfloat16
F32 = jnp.float32


def _params(*semantics):
    return pltpu.CompilerParams(dimension_semantics=semantics, vmem_limit_bytes=VMEM_LIMIT_BYTES)


def _rms(x):
    return x * lax.rsqrt(jnp.mean(x * x, axis=-1, keepdims=True) + NORM_EPS)


def _silu(x):
    h = 0.5 * x
    return h + h * jnp.tanh(h)


def _dot(a, b):
    return jnp.dot(a, b, preferred_element_type=F32)


def _dot_exact(a, b):
    return jnp.dot(a, b, preferred_element_type=F32, precision=lax.Precision.HIGHEST)


def _dot_nt(a, b):
    return lax.dot_general(a, b, (((1,), (1,)), ((), ())), preferred_element_type=F32)


def _pre_norm(x_ref, mod_ref, gpre_ref):
    return _rms(x_ref[...]) * gpre_ref[...] * (1.0 + mod_ref[1:2, :]) + mod_ref[0:1, :]


def _post_residual(x_ref, y, mod_ref, gpost_ref, res_weight):
    return x_ref[...] + (res_weight * (1.0 + mod_ref[2:3, :])) * (_rms(y) * gpost_ref[...])


def _mod_body(c_ref, w_ref, b_ref, o_ref):
    ca = _silu(c_ref[...]).astype(BF16)
    o_ref[...] = _dot(ca, w_ref[...].astype(BF16)) + b_ref[...]


def _modulation(c, w_mod, b_mod):
    depth, d, n = w_mod.shape
    b = c.shape[0]
    return pl.pallas_call(
        _mod_body,
        grid=(depth, n // MOD_TILE),
        in_specs=[
            pl.BlockSpec((b, d), lambda l, j: (0, 0)),
            pl.BlockSpec((None, d, MOD_TILE), lambda l, j: (l, 0, j)),
            pl.BlockSpec((None, 1, MOD_TILE), lambda l, j: (l, 0, j)),
        ],
        out_specs=pl.BlockSpec((None, b, MOD_TILE), lambda l, j: (l, 0, j)),
        out_shape=jax.ShapeDtypeStruct((depth, b, n), F32),
        compiler_params=_params("parallel", "parallel"),
        name="modulation",
    )(c, w_mod, b_mod.reshape(depth, 1, n))


def _ffn_body(x_ref, mod_ref, gpre_ref, gpost_ref, wg_ref, wu_ref, wd_ref, o_ref, h_ref, acc_ref, *, n_f):
    f = pl.program_id(2)

    @pl.when(f == 0)
    def _():
        h_ref[...] = _pre_norm(x_ref, mod_ref, gpre_ref).astype(BF16)
        acc_ref[...] = jnp.zeros_like(acc_ref)

    h = h_ref[...]
    a = (_silu(_dot(h, wg_ref[...])) * _dot(h, wu_ref[...])).astype(BF16)
    acc_ref[...] += _dot(a, wd_ref[...])

    @pl.when(f == n_f - 1)
    def _():
        o_ref[...] = _post_residual(x_ref, acc_ref[...], mod_ref, gpost_ref, FFN_RES_WEIGHT)


def _ffn_sublayer(x, mod, g_pre, g_post, w_gate, w_up, w_down, layer, sub, idx):
    b, s, d = x.shape
    f_dim = w_gate.shape[-1]
    n_f = f_dim // FFN_TILE
    row = lambda bi, i, f: (bi, i, 0)
    return pl.pallas_call(
        functools.partial(_ffn_body, n_f=n_f),
        grid=(b, s // ROW_TILE, n_f),
        in_specs=[
            pl.BlockSpec((None, ROW_TILE, d), row),
            pl.BlockSpec((None, None, None, 3, d), lambda bi, i, f: (layer, bi, sub, 0, 0)),
            pl.BlockSpec((None, None, 1, d), lambda bi, i, f: (layer, sub, 0, 0)),
            pl.BlockSpec((None, None, 1, d), lambda bi, i, f: (layer, sub, 0, 0)),
            pl.BlockSpec((None, None, d, FFN_TILE), lambda bi, i, f: (layer, idx, 0, f)),
            pl.BlockSpec((None, None, d, FFN_TILE), lambda bi, i, f: (layer, idx, 0, f)),
            pl.BlockSpec((None, None, FFN_TILE, d), lambda bi, i, f: (layer, idx, f, 0)),
        ],
        out_specs=pl.BlockSpec((None, ROW_TILE, d), row),
        out_shape=jax.ShapeDtypeStruct(x.shape, F32),
        scratch_shapes=[pltpu.VMEM((ROW_TILE, d), BF16), pltpu.VMEM((ROW_TILE, d), F32)],
        compiler_params=_params("parallel", "parallel", "arbitrary"),
        name="ffn_sublayer",
    )(x, mod, g_pre, g_post, w_gate, w_up, w_down)


def _gelu_tanh(x):
    return 0.5 * x * (1.0 + jnp.tanh(0.7978845608028654 * (x + 0.044715 * (x * x * x))))


def _proj_body(x_ref, mod_ref, gpre_ref, w_ref, *rest, gelu):
    if gelu:
        b_ref, o_ref, h_ref = rest
    else:
        o_ref, h_ref = rest

    @pl.when(pl.program_id(2) == 0)
    def _():
        h_ref[...] = _pre_norm(x_ref, mod_ref, gpre_ref).astype(BF16)

    y = _dot(h_ref[...], w_ref[...])
    if gelu:
        y = _gelu_tanh(y + b_ref[...])
    o_ref[...] = y.astype(o_ref.dtype)


def _pre_norm_proj(x, mod, g_pre, w, bias, layer, sub, tile, out_dtype, gelu=False):
    b, s, d = x.shape
    n = w.shape[-1]
    in_specs = [
        pl.BlockSpec((None, PROJ_ROWS, d), lambda bi, i, j: (bi, i, 0)),
        pl.BlockSpec((None, None, None, 3, d), lambda bi, i, j: (layer, bi, sub, 0, 0)),
        pl.BlockSpec((None, None, 1, d), lambda bi, i, j: (layer, sub, 0, 0)),
        pl.BlockSpec((d, tile), lambda bi, i, j: (0, j)),
    ]
    args = [x, mod, g_pre, w]
    if gelu:
        in_specs.append(pl.BlockSpec((1, tile), lambda bi, i, j: (0, j)))
        args.append(bias)
    return pl.pallas_call(
        functools.partial(_proj_body, gelu=gelu),
        grid=(b, s // PROJ_ROWS, n // tile),
        in_specs=in_specs,
        out_specs=pl.BlockSpec((None, PROJ_ROWS, tile), lambda bi, i, j: (bi, i, j)),
        out_shape=jax.ShapeDtypeStruct((b, s, n), out_dtype),
        scratch_shapes=[pltpu.VMEM((PROJ_ROWS, d), BF16)],
        compiler_params=_params("parallel", "parallel", "arbitrary"),
        name="pre_norm_proj",
    )(*args)


def _hyb_proj_body(x_ref, mod_ref, gpre_ref, wa_ref, wb_ref, wdt_ref, o_ref, dt_ref, h_ref, *, n_a):
    j = pl.program_id(2)

    @pl.when(j == 0)
    def _():
        h_ref[...] = _pre_norm(x_ref, mod_ref, gpre_ref).astype(BF16)
        dt_ref[...] = _dot(h_ref[...], wdt_ref[...])

    @pl.when(j < n_a)
    def _():
        o_ref[...] = _dot(h_ref[...], wa_ref[...]).astype(o_ref.dtype)

    @pl.when(j >= n_a)
    def _():
        o_ref[...] = _dot(h_ref[...], wb_ref[...]).astype(o_ref.dtype)


def _hyb_in_proj(x, mod, g_pre, w_a, w_b, w_dt, layer, sub):
    b, s, d = x.shape
    n_a, n_b = w_a.shape[1] // PROJ_TILE, w_b.shape[1] // PROJ_TILE
    return pl.pallas_call(
        functools.partial(_hyb_proj_body, n_a=n_a),
        grid=(b, s // PROJ_ROWS, n_a + n_b),
        in_specs=[
            pl.BlockSpec((None, PROJ_ROWS, d), lambda bi, i, j: (bi, i, 0)),
            pl.BlockSpec((None, None, None, 3, d), lambda bi, i, j: (layer, bi, sub, 0, 0)),
            pl.BlockSpec((None, None, 1, d), lambda bi, i, j: (layer, sub, 0, 0)),
            pl.BlockSpec((d, PROJ_TILE), lambda bi, i, j: (0, jnp.minimum(j, n_a - 1))),
            pl.BlockSpec((d, PROJ_TILE), lambda bi, i, j: (0, jnp.maximum(j - n_a, 0))),
            pl.BlockSpec((d, LANES), lambda bi, i, j: (0, 0)),
        ],
        out_specs=[
            pl.BlockSpec((None, PROJ_ROWS, PROJ_TILE), lambda bi, i, j: (bi, i, j)),
            pl.BlockSpec((None, PROJ_ROWS, LANES), lambda bi, i, j: (bi, i, 0)),
        ],
        out_shape=[
            jax.ShapeDtypeStruct((b, s, (n_a + n_b) * PROJ_TILE), BF16),
            jax.ShapeDtypeStruct((b, s, LANES), F32),
        ],
        scratch_shapes=[pltpu.VMEM((PROJ_ROWS, d), BF16)],
        compiler_params=_params("parallel", "parallel", "arbitrary"),
        name="hyb_in_proj",
    )(x, mod, g_pre, w_a, w_b, w_dt)


def _out_body(a_ref, b_ref, w_ref, x_ref, mod_ref, gpost_ref, o_ref, acc_ref):
    k = pl.program_id(2)

    @pl.when(k == 0)
    def _():
        acc_ref[...] = _dot(a_ref[...], w_ref[...])

    @pl.when(k == 1)
    def _():
        y = acc_ref[...] + _dot(b_ref[...], w_ref[...])
        o_ref[...] = _post_residual(x_ref, y, mod_ref, gpost_ref, MIXER_RES_WEIGHT)


def _out_proj_residual(lhs_a, col_a, lhs_b, col_b, w, x, mod, g_post, layer, sub):
    b, s, d = x.shape
    half = w.shape[0] // 2
    row = lambda bi, i, k: (bi, i, 0)
    return pl.pallas_call(
        _out_body,
        grid=(b, s // ROW_TILE, 2),
        in_specs=[
            pl.BlockSpec((None, ROW_TILE, half), lambda bi, i, k: (bi, i, col_a)),
            pl.BlockSpec((None, ROW_TILE, half), lambda bi, i, k: (bi, i, col_b)),
            pl.BlockSpec((half, d), lambda bi, i, k: (k, 0)),
            pl.BlockSpec((None, ROW_TILE, d), row),
            pl.BlockSpec((None, None, None, 3, d), lambda bi, i, k: (layer, bi, sub, 0, 0)),
            pl.BlockSpec((None, None, 1, d), lambda bi, i, k: (layer, sub, 0, 0)),
        ],
        out_specs=pl.BlockSpec((None, ROW_TILE, d), row),
        out_shape=jax.ShapeDtypeStruct(x.shape, F32),
        scratch_shapes=[pltpu.VMEM((ROW_TILE, d), F32)],
        compiler_params=_params("parallel", "parallel", "arbitrary"),
        name="out_proj_residual",
    )(lhs_a, lhs_b, w, x, mod, g_post)


def _rope_body(pos_ref, freq_ref, cos_ref, sin_ref):
    ang = pos_ref[...].astype(F32) * freq_ref[...]
    lane = lax.broadcasted_iota(jnp.int32, ang.shape, 1)
    cos_ref[...] = jnp.cos(ang)
    sin_ref[...] = jnp.where(lane < ATT_HEAD_DIM // 2, -1.0, 1.0) * jnp.sin(ang)


def _rope_tables(positions):
    b, s = positions.shape
    half = ATT_HEAD_DIM // 2
    inv_freq = ROPE_THETA ** (-jnp.arange(half, dtype=F32) / half)
    freq = jnp.concatenate([inv_freq, inv_freq]).reshape(1, ATT_HEAD_DIM)
    tile = 512
    spec = pl.BlockSpec((None, tile, ATT_HEAD_DIM), lambda bi, i: (bi, i, 0))
    return pl.pallas_call(
        _rope_body,
        grid=(b, s // tile),
        in_specs=[
            pl.BlockSpec((None, tile, 1), lambda bi, i: (bi, i, 0)),
            pl.BlockSpec((1, ATT_HEAD_DIM), lambda bi, i: (0, 0)),
        ],
        out_specs=[spec, spec],
        out_shape=[jax.ShapeDtypeStruct((b, s, ATT_HEAD_DIM), F32)] * 2,
        compiler_params=_params("parallel", "parallel"),
        name="rope_tables",
    )(positions.reshape(b, s, 1), freq)


def _ssd_body(xs_ref, bm_ref, cm_ref, z_ref, dt_ref, cw_ref, cb_ref, dtb_ref, alog_ref, dskip_ref, ng_ref,
              o_ref, tail_ref, state_ref):
    L = SSD_CHUNK
    T = SSD_TAIL

    @pl.when(pl.program_id(1) == 0)
    def _():
        tail_ref[...] = jnp.zeros_like(tail_ref)
        state_ref[...] = jnp.zeros_like(state_ref)

    srow = lax.broadcasted_iota(jnp.int32, ((SSD_CONV - 1) * L, L + T), 0)
    scol = lax.broadcasted_iota(jnp.int32, ((SSD_CONV - 1) * L, L + T), 1)
    lag = (srow >> (L.bit_length() - 1)) + 1
    t_in = srow & (L - 1)
    shift = (scol == jnp.where(t_in >= lag, t_in - lag, t_in - lag + (L + T))).astype(BF16)

    def conv_silu(raw_ref, lo, hi):
        raw = raw_ref[...]
        lagged = _dot(shift, jnp.concatenate([raw, tail_ref[:, lo:hi]], axis=0))
        acc = cb_ref[:, lo:hi] + raw.astype(F32) * cw_ref[SSD_CONV - 1:SSD_CONV, lo:hi]
        for k in range(1, SSD_CONV):
            acc = acc + lagged[(k - 1) * L:k * L] * cw_ref[SSD_CONV - 1 - k:SSD_CONV - k, lo:hi]
        tail_ref[:, lo:hi] = raw[L - T:L]
        return _silu(acc)

    xs = conv_silu(xs_ref, 0, SSD_WIDTH)
    bm = conv_silu(bm_ref, SSD_WIDTH, SSD_WIDTH + SSD_BC_WIDTH)
    cm = conv_silu(cm_ref, SSD_WIDTH + SSD_BC_WIDTH, SSD_CONV_CH)

    dt_in = dt_ref[...] + dtb_ref[...]
    dt = jnp.maximum(dt_in, 0.0) + jnp.log1p(jnp.exp(-jnp.abs(dt_in)))
    adt = dt * (-jnp.exp(alog_ref[...]))
    row = lax.broadcasted_iota(jnp.int32, (L, L), 0)
    col = lax.broadcasted_iota(jnp.int32, (L, L), 1)
    causal = row >= col
    acs = _dot_exact(causal.astype(F32), adt)
    acs_t = acs.T
    dt_t = dt.T
    acs_last = acs[L - 1:L, :]

    hrow = lax.broadcasted_iota(jnp.int32, (LANES, SSD_WIDTH), 0)
    hcol = lax.broadcasted_iota(jnp.int32, (LANES, SSD_WIDTH), 1)
    expand = (hrow == (hcol >> (SSD_HEAD_DIM.bit_length() - 1))).astype(BF16)
    stacked = jnp.concatenate([jnp.exp(acs), jnp.exp(acs_last - acs) * dt], axis=0)
    high = stacked.astype(BF16)
    rest = (stacked - high.astype(F32)).astype(BF16)
    wide = _dot(high, expand) + _dot(rest, expand)
    decay_in_w, dt_decay_out_w = wide[0:L], wide[L:2 * L]
    chunk_decay_w = decay_in_w[L - 1:L, :]

    lane = lax.broadcasted_iota(jnp.int32, (L, LANES), 1)
    first_head = lane < SSD_HEAD_DIM

    y_parts = []
    for g in range(SSD_GROUPS):
        gs = slice(g * SSD_GROUP_WIDTH, (g + 1) * SSD_GROUP_WIDTH)
        bg = bm[:, g * SSD_STATE:(g + 1) * SSD_STATE]
        cg = cm[:, g * SSD_STATE:(g + 1) * SSD_STATE].astype(BF16)
        cb = _dot_nt(cg, bg.astype(BF16))
        state = state_ref[g]
        y_off = _dot(cg, state.astype(BF16)) * decay_in_w[:, gs]
        y_diag = []
        for j in range(SSD_HEADS_PER_GROUP // 2):
            h0 = g * SSD_HEADS_PER_GROUP + 2 * j
            ms = []
            for h in (h0, h0 + 1):
                seg = acs[:, h:h + 1] - acs_t[h:h + 1, :]
                ms.append(cb * jnp.exp(jnp.where(causal, seg, -jnp.inf)) * dt_t[h:h + 1, :])
            lhs = jnp.concatenate(ms, axis=1).astype(BF16)
            xp = xs[:, h0 * SSD_HEAD_DIM:(h0 + 2) * SSD_HEAD_DIM]
            rhs = jnp.concatenate([jnp.where(first_head, xp, 0.0), jnp.where(first_head, 0.0, xp)], axis=0)
            y_diag.append(_dot(lhs, rhs.astype(BF16)))
        y_parts.append(jnp.concatenate(y_diag, axis=1) + y_off)
        contrib = _dot(bg.T.astype(BF16), (xs[:, gs] * dt_decay_out_w[:, gs]).astype(BF16))
        state_ref[g] = state * chunk_decay_w[:, gs] + contrib

    y = jnp.concatenate(y_parts, axis=1) + xs * dskip_ref[...]
    y = y * _silu(z_ref[...].astype(F32))
    o_ref[...] = (_rms(y) * ng_ref[...]).astype(o_ref.dtype)


def _ssd_mixer(proj, dt_raw, conv_w, conv_b, dt_bias, a_log, d_skip, norm_g):
    b, s, _ = proj.shape
    assert proj.dtype == BF16
    L = SSD_CHUNK
    z_blk = 0
    xs_blk = SSD_WIDTH // SSD_WIDTH
    bm_blk = (2 * SSD_WIDTH) // SSD_BC_WIDTH
    cm_blk = bm_blk + 1
    pad = LANES - SSD_HEADS
    small = lambda a: pl.BlockSpec(a.shape, lambda bi, c: (0, 0))
    dt_bias_p = jnp.pad(dt_bias, (0, pad)).reshape(1, LANES)
    a_log_p = jnp.pad(a_log, (0, pad)).reshape(1, LANES)
    d_skip_w = jnp.repeat(d_skip, SSD_HEAD_DIM).reshape(1, SSD_WIDTH)
    conv_b2 = conv_b.reshape(1, SSD_CONV_CH)
    norm_g2 = norm_g.reshape(1, SSD_WIDTH)
    return pl.pallas_call(
        _ssd_body,
        grid=(b, s // L),
        in_specs=[
            pl.BlockSpec((None, L, SSD_WIDTH), lambda bi, c: (bi, c, xs_blk)),
            pl.BlockSpec((None, L, SSD_BC_WIDTH), lambda bi, c: (bi, c, bm_blk)),
            pl.BlockSpec((None, L, SSD_BC_WIDTH), lambda bi, c: (bi, c, cm_blk)),
            pl.BlockSpec((None, L, SSD_WIDTH), lambda bi, c: (bi, c, z_blk)),
            pl.BlockSpec((None, L, LANES), lambda bi, c: (bi, c, 0)),
            small(conv_w), small(conv_b2), small(dt_bias_p), small(a_log_p), small(d_skip_w), small(norm_g2),
        ],
        out_specs=pl.BlockSpec((None, L, SSD_WIDTH), lambda bi, c: (bi, c, 0)),
        out_shape=jax.ShapeDtypeStruct((b, s, SSD_WIDTH), BF16),
        scratch_shapes=[
            pltpu.VMEM((SSD_TAIL, SSD_CONV_CH), BF16),
            pltpu.VMEM((SSD_GROUPS, SSD_STATE, SSD_GROUP_WIDTH), F32),
        ],
        compiler_params=_params("parallel", "arbitrary"),
        name="ssd_mixer",
    )(proj, proj, proj, proj, dt_raw, conv_w, conv_b2, dt_bias_p, a_log_p, d_skip_w, norm_g2)


def _attn_body(q_ref, k_ref, v_ref, cos_ref, sin_ref, o_ref, qkv_ref, acc_ref, m_ref, l_ref, *, seq):
    blk = ATT_BLOCK
    half = ATT_HEAD_DIM // 2
    dils = [d for _, d in DILATED_PATTERNS]
    step = dils[1]
    assert dils == [1, step, step * step] and all(w // d == blk for w, d in DILATED_PATTERNS)
    sub = seq // step
    assert seq // dils[2] == blk

    cos = cos_ref[...]
    sin = sin_ref[...]
    q = q_ref[...].astype(F32)
    k = k_ref[...].astype(F32)
    qkv_ref[0, 0] = (q * cos + pltpu.roll(q, half, 1) * sin) * (ATT_HEAD_DIM ** -0.5 * LOG2_E)
    qkv_ref[0, 1] = k * cos + pltpu.roll(k, half, 1) * sin
    qkv_ref[0, 2] = v_ref[...].astype(F32)
    for t in range(3):
        for r in range(step):
            qkv_ref[1, t, pl.ds(r * sub, sub), :] = qkv_ref[0, t, pl.ds(r, sub, stride=step), :]
    for t in range(3):
        for r in range(step):
            for a in range(step):
                qkv_ref[2, t, pl.ds((r + step * a) * blk, blk), :] = qkv_ref[1, t, pl.ds(r * sub + a, blk, stride=step), :]

    row = lax.broadcasted_iota(jnp.int32, (blk, blk), 0)
    col = lax.broadcasted_iota(jnp.int32, (blk, blk), 1)
    cur_ok = col <= row
    prev_ok = col >= row

    def attend(p, starts, with_prev):
        def load(t, st):
            return qkv_ref[p, t, pl.ds(st, blk), :].astype(BF16)

        def window(t, st):
            return jnp.concatenate([load(t, st - blk), load(t, st)], axis=0) if with_prev else load(t, st)

        qb = jnp.stack([load(0, st) for st in starts])
        kk = jnp.stack([window(1, st) for st in starts])
        keys = kk.shape[1]
        vv = jnp.stack([jnp.concatenate([window(2, st), jnp.ones((keys, LANES), BF16)], axis=1) for st in starts])
        ok = jnp.concatenate([prev_ok, cur_ok], axis=1) if with_prev else cur_ok
        s = lax.dot_general(qb, kk, (((2,), (2,)), ((0,), (0,))), preferred_element_type=F32)
        s = jnp.where(ok[None], s, -jnp.inf)
        m = jnp.max(s, axis=2, keepdims=True)
        e = jnp.exp2(s - m)
        acc = lax.dot_general(e.astype(BF16), vv, (((2,), (1,)), ((0,), (0,))), preferred_element_type=F32)
        for i, st in enumerate(starts):
            rows = pl.ds(st, blk)
            acc_ref[p, rows, :] = acc[i, :, 0:ATT_HEAD_DIM]
            l_ref[p, rows, :] = acc[i, :, ATT_HEAD_DIM:]
            m_ref[p, rows, :] = jnp.broadcast_to(m[i], (blk, LANES))

    def groups(starts, size):
        return [starts[i:i + size] for i in range(0, len(starts), size)]

    for p, dil in enumerate(dils):
        class_rows = seq // dil
        firsts = [r * class_rows for r in range(dil)]
        laters = [r * class_rows + n * blk for r in range(dil) for n in range(1, class_rows // blk)]
        for g in groups(firsts, ATT_GROUP_FIRST):
            attend(p, g, False)
        for g in groups(laters, ATT_GROUP_LATER):
            attend(p, g, True)

    def merged(dst, dst_rows, src, src_rows):
        m_a, m_b = m_ref[dst, dst_rows, :], m_ref[src, src_rows, :]
        top = jnp.maximum(m_a, m_b)
        w_a, w_b = jnp.exp2(m_a - top), jnp.exp2(m_b - top)
        acc = w_a * acc_ref[dst, dst_rows, :] + w_b * acc_ref[src, src_rows, :]
        return top, acc, w_a * l_ref[dst, dst_rows, :] + w_b * l_ref[src, src_rows, :]

    for r in range(step):
        for a in range(step):
            mid_rows = pl.ds(r * sub + a, blk, stride=step)
            top, acc, l = merged(1, mid_rows, 2, pl.ds((r + step * a) * blk, blk))
            m_ref[1, mid_rows, :] = top
            acc_ref[1, mid_rows, :] = acc
            l_ref[1, mid_rows, :] = l
    for r in range(step):
        for n in range(sub // blk):
            nat_rows = pl.ds(r + n * blk * step, blk, stride=step)
            _, acc, l = merged(0, nat_rows, 1, pl.ds(r * sub + n * blk, blk))
            acc_ref[0, nat_rows, :] = acc / l
    o_ref[...] = acc_ref[0].astype(o_ref.dtype)


def _dilated_attention(proj, cos2, sin2):
    b, s, _ = proj.shape
    q_blk = (2 * SSD_WIDTH + 2 * SSD_BC_WIDTH) // ATT_HEAD_DIM
    k_blk = q_blk + ATT_HEADS
    v_blk = k_blk + ATT_HEADS
    n_pat = len(DILATED_PATTERNS)
    head = lambda base: pl.BlockSpec((None, s, ATT_HEAD_DIM), lambda bi, h: (bi, 0, base + h))
    table = pl.BlockSpec((None, s, ATT_HEAD_DIM), lambda bi, h: (bi, 0, 0))
    return pl.pallas_call(
        functools.partial(_attn_body, seq=s),
        grid=(b, ATT_HEADS),
        in_specs=[head(q_blk), head(k_blk), head(v_blk), table, table],
        out_specs=pl.BlockSpec((None, s, ATT_HEAD_DIM), lambda bi, h: (bi, 0, h)),
        out_shape=jax.ShapeDtypeStruct((b, s, ATT_WIDTH), BF16),
        scratch_shapes=[
            pltpu.VMEM((n_pat, 3, s, ATT_HEAD_DIM), F32),
            pltpu.VMEM((n_pat, s, ATT_HEAD_DIM), F32),
            pltpu.VMEM((n_pat, s, LANES), F32),
            pltpu.VMEM((n_pat, s, LANES), F32),
        ],
        compiler_params=_params("parallel", "arbitrary"),
        name="dilated_attention",
    )(proj, proj, proj, cos2, sin2)


def _sgu_body(u_ref, v_ref, g_ref, b_ref, ws_ref, bs_ref, o_ref):
    L = SGU_CHUNK
    v = v_ref[...].astype(F32)
    mu = jnp.mean(v, axis=-1, keepdims=True)
    vc = v - mu
    var = jnp.mean(vc * vc, axis=-1, keepdims=True)
    vn = (vc * lax.rsqrt(var + NORM_EPS) * g_ref[...] + b_ref[...]).astype(BF16)
    row = lax.broadcasted_iota(jnp.int32, (L, L), 0)
    col = lax.broadcasted_iota(jnp.int32, (L, L), 1)
    causal = row >= col
    for g in range(SGU_GROUPS):
        gs = slice(g * SGU_GROUP_WIDTH, (g + 1) * SGU_GROUP_WIDTH)
        w = jnp.where(causal, ws_ref[g], 0.0).astype(BF16)
        mixed = _dot(w, vn[:, gs]) + bs_ref[:, g:g + 1]
        o_ref[:, gs] = (u_ref[:, gs].astype(F32) * mixed).astype(o_ref.dtype)


def _sgu_gate(zz, ln_g, ln_b, w_spatial, b_spatial):
    b, s, _ = zz.shape
    L = SGU_CHUNK
    vec = pl.BlockSpec((1, SGU_WIDTH), lambda bi, c: (0, 0))
    return pl.pallas_call(
        _sgu_body,
        grid=(b, s // L),
        in_specs=[
            pl.BlockSpec((None, L, SGU_WIDTH), lambda bi, c: (bi, c, 0)),
            pl.BlockSpec((None, L, SGU_WIDTH), lambda bi, c: (bi, c, 1)),
            vec, vec,
            pl.BlockSpec((SGU_GROUPS, L, L), lambda bi, c: (0, 0, 0)),
            pl.BlockSpec((L, SGU_GROUPS), lambda bi, c: (0, 0)),
        ],
        out_specs=pl.BlockSpec((None, L, SGU_WIDTH), lambda bi, c: (bi, c, 0)),
        out_shape=jax.ShapeDtypeStruct((b, s, SGU_WIDTH), BF16),
        compiler_params=_params("parallel", "parallel"),
        name="sgu_gate",
    )(zz, zz, ln_g.reshape(1, SGU_WIDTH), ln_b.reshape(1, SGU_WIDTH), w_spatial, b_spatial.T)


def kernel(x, c, positions, w_mod, b_mod, norm_pre, norm_post, ffn_w_gate, ffn_w_up, ffn_w_down, hyb_w_in, hyb_conv_w, hyb_conv_b, hyb_dt_bias, hyb_a_log, hyb_d_skip, hyb_norm_g, hyb_w_out, sgu_w_in, sgu_b_in, sgu_ln_g, sgu_ln_b, sgu_w_spatial, sgu_b_spatial, sgu_w_out):
    depth = w_mod.shape[0]
    b, s, d = x.shape
    n_sub = norm_pre.shape[1]

    mod = _modulation(c, w_mod, b_mod).reshape(depth, b, n_sub, 3, d)
    g_pre = norm_pre.reshape(depth, n_sub, 1, d)
    g_post = norm_post.reshape(depth, n_sub, 1, d)
    w_gate = ffn_w_gate.astype(BF16)
    w_up = ffn_w_up.astype(BF16)
    w_down = ffn_w_down.astype(BF16)

    for layer in range(depth):
        i = layer // 2
        x = _ffn_sublayer(x, mod, g_pre, g_post, w_gate, w_up, w_down, layer, 0, 0)
        if layer % 2 == 0:
            w_in = hyb_w_in[i]
            dt_lo = SSD_WIDTH + SSD_CONV_CH
            dt_hi = dt_lo + SSD_HEADS
            w_dt = jnp.pad(w_in[:, dt_lo:dt_hi], ((0, 0), (0, LANES - SSD_HEADS))).astype(BF16)
            proj, dt_raw = _hyb_in_proj(x, mod, g_pre, w_in[:, :dt_lo].astype(BF16), w_in[:, dt_hi:].astype(BF16),
                                        w_dt, layer, 1)
            cos2, sin2 = _rope_tables(positions)
            y_a = _ssd_mixer(proj, dt_raw, hyb_conv_w[i], hyb_conv_b[i], hyb_dt_bias[i], hyb_a_log[i],
                             hyb_d_skip[i], hyb_norm_g[i])
            y_b = _dilated_attention(proj, cos2, sin2)
            x = _out_proj_residual(y_a, 0, y_b, 0, hyb_w_out[i].astype(BF16), x, mod, g_post, layer, 1)
        else:
            zz = _pre_norm_proj(x, mod, g_pre, sgu_w_in[i].astype(BF16), sgu_b_in[i].reshape(1, -1), layer, 1,
                                PROJ_TILE, BF16, gelu=True)
            gated = _sgu_gate(zz, sgu_ln_g[i], sgu_ln_b[i], sgu_w_spatial[i], sgu_b_spatial[i])
            x = _out_proj_residual(gated, 0, gated, 1, sgu_w_out[i].astype(BF16), x, mod, g_post, layer, 1)
        x = _ffn_sublayer(x, mod, g_pre, g_post, w_gate, w_up, w_down, layer, 2, 1)
    return x
```

```python
import functools

import jax
import jax.numpy as jnp
from jax import lax
from jax.experimental import pallas as pl
from jax.experimental.pallas import tpu as pltpu

NORM_EPS = 1e-6
LOG2_E = 1.4426950408889634
FFN_RES_WEIGHT = 0.5
MIXER_RES_WEIGHT = 1.0

SSD_HEADS = 32
SSD_HEAD_DIM = 64
SSD_WIDTH = SSD_HEADS * SSD_HEAD_DIM
SSD_GROUPS = 4
SSD_STATE = 128
SSD_CONV = 4
SSD_CHUNK = 128
SSD_TAIL = 16
SSD_BC_WIDTH = SSD_GROUPS * SSD_STATE
SSD_CONV_CH = SSD_WIDTH + 2 * SSD_BC_WIDTH
SSD_HEADS_PER_GROUP = SSD_HEADS // SSD_GROUPS
SSD_GROUP_WIDTH = SSD_HEADS_PER_GROUP * SSD_HEAD_DIM

ATT_HEADS = 16
ATT_HEAD_DIM = 128
ATT_WIDTH = ATT_HEADS * ATT_HEAD_DIM
ATT_BLOCK = 128
DILATED_PATTERNS = ((128, 1), (512, 4), (2048, 16))
ROPE_THETA = 10000.0

SGU_WIDTH = 4096
SGU_GROUPS = 8
SGU_CHUNK = 128
SGU_GROUP_WIDTH = SGU_WIDTH // SGU_GROUPS

LANES = 128
SUBLANES = 8
VMEM_LIMIT_BYTES = 56 * 1024 * 1024

ROW_TILE = 512
FFN_ROWS = 1024
FFN_TILE = 512
FFN_NORM_CHUNKS = 8
FFN_NORM_ROWS = 16
PROJ_ROWS = 1024
PROJ_TILE = 1024
MOD_TILE = 1024
ATT_GROUP_FIRST = 8
ATT_GROUP_LATER = 5

BF16 = jnp.bfloat16
F32 = jnp.float32


def _params(*semantics):
    return pltpu.CompilerParams(dimension_semantics=semantics, vmem_limit_bytes=VMEM_LIMIT_BYTES)


def _rms(x):
    return x * lax.rsqrt(jnp.mean(x * x, axis=-1, keepdims=True) + NORM_EPS)


def _silu(x):
    h = 0.5 * x
    return h + h * jnp.tanh(h)


def _dot(a, b):
    return jnp.dot(a, b, preferred_element_type=F32)


def _dot_exact(a, b):
    return jnp.dot(a, b, preferred_element_type=F32, precision=lax.Precision.HIGHEST)


def _dot_nt(a, b):
    return lax.dot_general(a, b, (((1,), (1,)), ((), ())), preferred_element_type=F32)


def _pre_norm(x_ref, mod_ref, gpre_ref):
    return _rms(x_ref[...]) * gpre_ref[...] * (1.0 + mod_ref[1:2, :]) + mod_ref[0:1, :]


def _post_residual(x_ref, y, mod_ref, gpost_ref, res_weight):
    return x_ref[...] + (res_weight * (1.0 + mod_ref[2:3, :])) * (_rms(y) * gpost_ref[...])


def _mod_body(c_ref, w_ref, b_ref, o_ref):
    ca = _silu(c_ref[...]).astype(BF16)
    o_ref[...] = _dot(ca, w_ref[...].astype(BF16)) + b_ref[...]


def _modulation(c, w_mod, b_mod):
    depth, d, n = w_mod.shape
    b = c.shape[0]
    return pl.pallas_call(
        _mod_body,
        grid=(depth, n // MOD_TILE),
        in_specs=[
            pl.BlockSpec((b, d), lambda l, j: (0, 0)),
            pl.BlockSpec((None, d, MOD_TILE), lambda l, j: (l, 0, j)),
            pl.BlockSpec((None, 1, MOD_TILE), lambda l, j: (l, 0, j)),
        ],
        out_specs=pl.BlockSpec((None, b, MOD_TILE), lambda l, j: (l, 0, j)),
        out_shape=jax.ShapeDtypeStruct((depth, b, n), F32),
        compiler_params=_params("parallel", "parallel"),
        name="modulation",
    )(c, w_mod, b_mod.reshape(depth, 1, n))


def _ffn_body(xn_ref, xp_ref, modn_ref, modp_ref, gpre_ref, gpost_ref, wg_ref, wu_ref, wd_ref, o_ref,
              h0_ref, h1_ref, acc0_ref, acc1_ref, *, n_tiles, n_f):
    g = pl.program_id(0)
    f = pl.program_id(1)
    chunk = FFN_ROWS // FFN_NORM_CHUNKS
    c0 = jnp.minimum(f, FFN_NORM_CHUNKS - 1) * chunk

    @pl.when((g == 0) & (f == 0))
    def _():
        for ref in (h0_ref, h1_ref, acc0_ref, acc1_ref):
            ref[...] = jnp.zeros_like(ref)

    def pre_norm_chunk(h_ref):
        gain = gpre_ref[...] * (1.0 + modn_ref[1:2, :])
        for q in range(chunk // FFN_NORM_ROWS):
            rows = pl.ds(q * FFN_NORM_ROWS, FFN_NORM_ROWS)
            tile_rows = pl.ds(pl.multiple_of(c0 + q * FFN_NORM_ROWS, FFN_NORM_ROWS), FFN_NORM_ROWS)
            h_ref[tile_rows, :] = (_rms(xn_ref[rows, :]) * gain + modn_ref[0:1, :]).astype(BF16)

    def post_norm_chunk(acc_ref):
        gain = (FFN_RES_WEIGHT * (1.0 + modp_ref[2:3, :])) * gpost_ref[...]
        for q in range(chunk // FFN_NORM_ROWS):
            rows = pl.ds(q * FFN_NORM_ROWS, FFN_NORM_ROWS)
            tile_rows = pl.ds(pl.multiple_of(c0 + q * FFN_NORM_ROWS, FFN_NORM_ROWS), FFN_NORM_ROWS)
            o_ref[rows, :] = xp_ref[rows, :] + _rms(acc_ref[tile_rows, :]) * gain

    def swiglu_step(h_ref, acc_ref):
        h = h_ref[...]
        a = (_silu(_dot(h, wg_ref[...])) * _dot(h, wu_ref[...])).astype(BF16)
        acc_ref[...] = jnp.where(f > 0, acc_ref[...], 0.0) + _dot(a, wd_ref[...])

    has_matmul = (g >= 1) & (g <= n_tiles)
    for parity, (h_new, acc_old, h_mid, acc_mid) in enumerate(
            [(h0_ref, acc0_ref, h1_ref, acc1_ref), (h1_ref, acc1_ref, h0_ref, acc0_ref)]):
        @pl.when((lax.rem(g, 2) == parity) & has_matmul)
        def _():
            post_norm_chunk(acc_old)
            swiglu_step(h_mid, acc_mid)
            pre_norm_chunk(h_new)

        @pl.when((lax.rem(g, 2) == parity) & jnp.logical_not(has_matmul))
        def _():
            post_norm_chunk(acc_old)
            pre_norm_chunk(h_new)


def _ffn_sublayer(x, mod, g_pre, g_post, w_gate, w_up, w_down, layer, sub, idx):
    b, s, d = x.shape
    f_dim = w_gate.shape[-1]
    n_f = f_dim // FFN_TILE
    per_batch = s // FFN_ROWS
    n_tiles = b * per_batch
    chunks = FFN_NORM_CHUNKS
    chunk = FFN_ROWS // chunks
    assert n_f >= chunks
    new_tile = lambda g: jnp.minimum(g, n_tiles - 1)
    old_tile = lambda g: jnp.clip(g - 2, 0, n_tiles - 1)
    new_chunk = lambda g, f: (new_tile(g) * chunks + jnp.minimum(f, chunks - 1), 0, 0)
    old_chunk = lambda g, f: (old_tile(g) * chunks + jnp.minimum(f, chunks - 1), 0, 0)
    out_chunk = lambda g, f: (jnp.where(g < 2, 0, old_chunk(g, f)[0]), 0, 0)
    w_step = lambda g, f: jnp.where(g == 0, 0, jnp.where(g == n_tiles + 1, n_f - 1, f))
    x_chunks = x.reshape(n_tiles * chunks, chunk, d)
    out = pl.pallas_call(
        functools.partial(_ffn_body, n_tiles=n_tiles, n_f=n_f),
        grid=(n_tiles + 2, n_f),
        in_specs=[
            pl.BlockSpec((None, chunk, d), new_chunk),
            pl.BlockSpec((None, chunk, d), old_chunk),
            pl.BlockSpec((None, None, None, 3, d), lambda g, f: (layer, new_tile(g) // per_batch, sub, 0, 0)),
            pl.BlockSpec((None, None, None, 3, d), lambda g, f: (layer, old_tile(g) // per_batch, sub, 0, 0)),
            pl.BlockSpec((None, None, 1, d), lambda g, f: (layer, sub, 0, 0)),
            pl.BlockSpec((None, None, 1, d), lambda g, f: (layer, sub, 0, 0)),
            pl.BlockSpec((None, None, d, FFN_TILE), lambda g, f: (layer, idx, 0, w_step(g, f))),
            pl.BlockSpec((None, None, d, FFN_TILE), lambda g, f: (layer, idx, 0, w_step(g, f))),
            pl.BlockSpec((None, None, FFN_TILE, d), lambda g, f: (layer, idx, w_step(g, f), 0)),
        ],
        out_specs=pl.BlockSpec((None, chunk, d), out_chunk),
        out_shape=jax.ShapeDtypeStruct(x_chunks.shape, F32),
        scratch_shapes=[pltpu.VMEM((FFN_ROWS, d), BF16), pltpu.VMEM((FFN_ROWS, d), BF16),
                        pltpu.VMEM((FFN_ROWS, d), F32), pltpu.VMEM((FFN_ROWS, d), F32)],
        compiler_params=_params("arbitrary", "arbitrary"),
        name="ffn_sublayer",
    )(x_chunks, x_chunks, mod, mod, g_pre, g_post, w_gate, w_up, w_down)
    return out.reshape(b, s, d)


def _gelu_tanh(x):
    return 0.5 * x * (1.0 + jnp.tanh(0.7978845608028654 * (x + 0.044715 * (x * x * x))))


def _proj_body(x_ref, mod_ref, gpre_ref, w_ref, *rest, gelu):
    if gelu:
        b_ref, o_ref, h_ref = rest
    else:
        o_ref, h_ref = rest

    @pl.when(pl.program_id(2) == 0)
    def _():
        h_ref[...] = _pre_norm(x_ref, mod_ref, gpre_ref).astype(BF16)

    y = _dot(h_ref[...], w_ref[...])
    if gelu:
        y = _gelu_tanh(y + b_ref[...])
    o_ref[...] = y.astype(o_ref.dtype)


def _pre_norm_proj(x, mod, g_pre, w, bias, layer, sub, tile, out_dtype, gelu=False):
    b, s, d = x.shape
    n = w.shape[-1]
    in_specs = [
        pl.BlockSpec((None, PROJ_ROWS, d), lambda bi, i, j: (bi, i, 0)),
        pl.BlockSpec((None, None, None, 3, d), lambda bi, i, j: (layer, bi, sub, 0, 0)),
        pl.BlockSpec((None, None, 1, d), lambda bi, i, j: (layer, sub, 0, 0)),
        pl.BlockSpec((d, tile), lambda bi, i, j: (0, j)),
    ]
    args = [x, mod, g_pre, w]
    if gelu:
        in_specs.append(pl.BlockSpec((1, tile), lambda bi, i, j: (0, j)))
        args.append(bias)
    return pl.pallas_call(
        functools.partial(_proj_body, gelu=gelu),
        grid=(b, s // PROJ_ROWS, n // tile),
        in_specs=in_specs,
        out_specs=pl.BlockSpec((None, PROJ_ROWS, tile), lambda bi, i, j: (bi, i, j)),
        out_shape=jax.ShapeDtypeStruct((b, s, n), out_dtype),
        scratch_shapes=[pltpu.VMEM((PROJ_ROWS, d), BF16)],
        compiler_params=_params("parallel", "parallel", "arbitrary"),
        name="pre_norm_proj",
    )(*args)


def _hyb_proj_body(x_ref, mod_ref, gpre_ref, wa_ref, wb_ref, wdt_ref, o_ref, dt_ref, h_ref, *, n_a):
    j = pl.program_id(2)

    @pl.when(j == 0)
    def _():
        h_ref[...] = _pre_norm(x_ref, mod_ref, gpre_ref).astype(BF16)
        dt_ref[...] = _dot(h_ref[...], wdt_ref[...])

    @pl.when(j < n_a)
    def _():
        o_ref[...] = _dot(h_ref[...], wa_ref[...]).astype(o_ref.dtype)

    @pl.when(j >= n_a)
    def _():
        o_ref[...] = _dot(h_ref[...], wb_ref[...]).astype(o_ref.dtype)


def _hyb_in_proj(x, mod, g_pre, w_a, w_b, w_dt, layer, sub):
    b, s, d = x.shape
    n_a, n_b = w_a.shape[1] // PROJ_TILE, w_b.shape[1] // PROJ_TILE
    return pl.pallas_call(
        functools.partial(_hyb_proj_body, n_a=n_a),
        grid=(b, s // PROJ_ROWS, n_a + n_b),
        in_specs=[
            pl.BlockSpec((None, PROJ_ROWS, d), lambda bi, i, j: (bi, i, 0)),
            pl.BlockSpec((None, None, None, 3, d), lambda bi, i, j: (layer, bi, sub, 0, 0)),
            pl.BlockSpec((None, None, 1, d), lambda bi, i, j: (layer, sub, 0, 0)),
            pl.BlockSpec((d, PROJ_TILE), lambda bi, i, j: (0, jnp.minimum(j, n_a - 1))),
            pl.BlockSpec((d, PROJ_TILE), lambda bi, i, j: (0, jnp.maximum(j - n_a, 0))),
            pl.BlockSpec((d, LANES), lambda bi, i, j: (0, 0)),
        ],
        out_specs=[
            pl.BlockSpec((None, PROJ_ROWS, PROJ_TILE), lambda bi, i, j: (bi, i, j)),
            pl.BlockSpec((None, PROJ_ROWS, LANES), lambda bi, i, j: (bi, i, 0)),
        ],
        out_shape=[
            jax.ShapeDtypeStruct((b, s, (n_a + n_b) * PROJ_TILE), BF16),
            jax.ShapeDtypeStruct((b, s, LANES), F32),
        ],
        scratch_shapes=[pltpu.VMEM((PROJ_ROWS, d), BF16)],
        compiler_params=_params("parallel", "parallel", "arbitrary"),
        name="hyb_in_proj",
    )(x, mod, g_pre, w_a, w_b, w_dt)


def _out_body(a_ref, b_ref, w_ref, x_ref, mod_ref, gpost_ref, o_ref, acc_ref):
    k = pl.program_id(2)

    @pl.when(k == 0)
    def _():
        acc_ref[...] = _dot(a_ref[...], w_ref[...])

    @pl.when(k == 1)
    def _():
        y = acc_ref[...] + _dot(b_ref[...], w_ref[...])
        o_ref[...] = _post_residual(x_ref, y, mod_ref, gpost_ref, MIXER_RES_WEIGHT)


def _out_proj_residual(lhs_a, col_a, lhs_b, col_b, w, x, mod, g_post, layer, sub):
    b, s, d = x.shape
    half = w.shape[0] // 2
    row = lambda bi, i, k: (bi, i, 0)
    return pl.pallas_call(
        _out_body,
        grid=(b, s // ROW_TILE, 2),
        in_specs=[
            pl.BlockSpec((None, ROW_TILE, half), lambda bi, i, k: (bi, i, col_a)),
            pl.BlockSpec((None, ROW_TILE, half), lambda bi, i, k: (bi, i, col_b)),
            pl.BlockSpec((half, d), lambda bi, i, k: (k, 0)),
            pl.BlockSpec((None, ROW_TILE, d), row),
            pl.BlockSpec((None, None, None, 3, d), lambda bi, i, k: (layer, bi, sub, 0, 0)),
            pl.BlockSpec((None, None, 1, d), lambda bi, i, k: (layer, sub, 0, 0)),
        ],
        out_specs=pl.BlockSpec((None, ROW_TILE, d), row),
        out_shape=jax.ShapeDtypeStruct(x.shape, F32),
        scratch_shapes=[pltpu.VMEM((ROW_TILE, d), F32)],
        compiler_params=_params("parallel", "parallel", "arbitrary"),
        name="out_proj_residual",
    )(lhs_a, lhs_b, w, x, mod, g_post)


def _rope_body(pos_ref, freq_ref, cos_ref, sin_ref):
    ang = pos_ref[...].astype(F32) * freq_ref[...]
    lane = lax.broadcasted_iota(jnp.int32, ang.shape, 1)
    cos_ref[...] = jnp.cos(ang)
    sin_ref[...] = jnp.where(lane < ATT_HEAD_DIM // 2, -1.0, 1.0) * jnp.sin(ang)


def _rope_tables(positions):
    b, s = positions.shape
    half = ATT_HEAD_DIM // 2
    inv_freq = ROPE_THETA ** (-jnp.arange(half, dtype=F32) / half)
    freq = jnp.concatenate([inv_freq, inv_freq]).reshape(1, ATT_HEAD_DIM)
    tile = 512
    spec = pl.BlockSpec((None, tile, ATT_HEAD_DIM), lambda bi, i: (bi, i, 0))
    return pl.pallas_call(
        _rope_body,
        grid=(b, s // tile),
        in_specs=[
            pl.BlockSpec((None, tile, 1), lambda bi, i: (bi, i, 0)),
            pl.BlockSpec((1, ATT_HEAD_DIM), lambda bi, i: (0, 0)),
        ],
        out_specs=[spec, spec],
        out_shape=[jax.ShapeDtypeStruct((b, s, ATT_HEAD_DIM), F32)] * 2,
        compiler_params=_params("parallel", "parallel"),
        name="rope_tables",
    )(positions.reshape(b, s, 1), freq)


def _ssd_body(xs_ref, bm_ref, cm_ref, z_ref, dt_ref, cw_ref, cb_ref, dtb_ref, alog_ref, dskip_ref, ng_ref,
              o_ref, tail_ref, state_ref):
    L = SSD_CHUNK
    T = SSD_TAIL

    @pl.when(pl.program_id(1) == 0)
    def _():
        tail_ref[...] = jnp.zeros_like(tail_ref)
        state_ref[...] = jnp.zeros_like(state_ref)

    srow = lax.broadcasted_iota(jnp.int32, ((SSD_CONV - 1) * L, L + T), 0)
    scol = lax.broadcasted_iota(jnp.int32, ((SSD_CONV - 1) * L, L + T), 1)
    lag = (srow >> (L.bit_length() - 1)) + 1
    t_in = srow & (L - 1)
    shift = (scol == jnp.where(t_in >= lag, t_in - lag, t_in - lag + (L + T))).astype(BF16)

    def conv_silu(raw_ref, lo, hi):
        raw = raw_ref[...]
        lagged = _dot(shift, jnp.concatenate([raw, tail_ref[:, lo:hi]], axis=0))
        acc = cb_ref[:, lo:hi] + raw.astype(F32) * cw_ref[SSD_CONV - 1:SSD_CONV, lo:hi]
        for k in range(1, SSD_CONV):
            acc = acc + lagged[(k - 1) * L:k * L] * cw_ref[SSD_CONV - 1 - k:SSD_CONV - k, lo:hi]
        tail_ref[:, lo:hi] = raw[L - T:L]
        return _silu(acc)

    xs = conv_silu(xs_ref, 0, SSD_WIDTH)
    bm = conv_silu(bm_ref, SSD_WIDTH, SSD_WIDTH + SSD_BC_WIDTH)
    cm = conv_silu(cm_ref, SSD_WIDTH + SSD_BC_WIDTH, SSD_CONV_CH)

    dt_in = dt_ref[...] + dtb_ref[...]
    dt = jnp.maximum(dt_in, 0.0) + jnp.log1p(jnp.exp(-jnp.abs(dt_in)))
    adt = dt * (-jnp.exp(alog_ref[...]))
    row = lax.broadcasted_iota(jnp.int32, (L, L), 0)
    col = lax.broadcasted_iota(jnp.int32, (L, L), 1)
    causal = row >= col
    acs = _dot_exact(causal.astype(F32), adt)
    acs_t = acs.T
    dt_t = dt.T
    acs_last = acs[L - 1:L, :]

    hrow = lax.broadcasted_iota(jnp.int32, (LANES, SSD_WIDTH), 0)
    hcol = lax.broadcasted_iota(jnp.int32, (LANES, SSD_WIDTH), 1)
    expand = (hrow == (hcol >> (SSD_HEAD_DIM.bit_length() - 1))).astype(BF16)
    stacked = jnp.concatenate([jnp.exp(acs), jnp.exp(acs_last - acs) * dt], axis=0)
    high = stacked.astype(BF16)
    rest = (stacked - high.astype(F32)).astype(BF16)
    wide = _dot(high, expand) + _dot(rest, expand)
    decay_in_w, dt_decay_out_w = wide[0:L], wide[L:2 * L]
    chunk_decay_w = decay_in_w[L - 1:L, :]

    lane = lax.broadcasted_iota(jnp.int32, (L, LANES), 1)
    first_head = lane < SSD_HEAD_DIM

    y_parts = []
    for g in range(SSD_GROUPS):
        gs = slice(g * SSD_GROUP_WIDTH, (g + 1) * SSD_GROUP_WIDTH)
        bg = bm[:, g * SSD_STATE:(g + 1) * SSD_STATE]
        cg = cm[:, g * SSD_STATE:(g + 1) * SSD_STATE].astype(BF16)
        cb = _dot_nt(cg, bg.astype(BF16))
        state = state_ref[g]
        y_off = _dot(cg, state.astype(BF16)) * decay_in_w[:, gs]
        y_diag = []
        for j in range(SSD_HEADS_PER_GROUP // 2):
            h0 = g * SSD_HEADS_PER_GROUP + 2 * j
            ms = []
            for h in (h0, h0 + 1):
                seg = acs[:, h:h + 1] - acs_t[h:h + 1, :]
                ms.append(cb * jnp.exp(jnp.where(causal, seg, -jnp.inf)) * dt_t[h:h + 1, :])
            lhs = jnp.concatenate(ms, axis=1).astype(BF16)
            xp = xs[:, h0 * SSD_HEAD_DIM:(h0 + 2) * SSD_HEAD_DIM]
            rhs = jnp.concatenate([jnp.where(first_head, xp, 0.0), jnp.where(first_head, 0.0, xp)], axis=0)
            y_diag.append(_dot(lhs, rhs.astype(BF16)))
        y_parts.append(jnp.concatenate(y_diag, axis=1) + y_off)
        contrib = _dot(bg.T.astype(BF16), (xs[:, gs] * dt_decay_out_w[:, gs]).astype(BF16))
        state_ref[g] = state * chunk_decay_w[:, gs] + contrib

    y = jnp.concatenate(y_parts, axis=1) + xs * dskip_ref[...]
    y = y * _silu(z_ref[...].astype(F32))
    o_ref[...] = (_rms(y) * ng_ref[...]).astype(o_ref.dtype)


def _ssd_mixer(proj, dt_raw, conv_w, conv_b, dt_bias, a_log, d_skip, norm_g):
    b, s, _ = proj.shape
    assert proj.dtype == BF16
    L = SSD_CHUNK
    z_blk = 0
    xs_blk = SSD_WIDTH // SSD_WIDTH
    bm_blk = (2 * SSD_WIDTH) // SSD_BC_WIDTH
    cm_blk = bm_blk + 1
    pad = LANES - SSD_HEADS
    small = lambda a: pl.BlockSpec(a.shape, lambda bi, c: (0, 0))
    dt_bias_p = jnp.pad(dt_bias, (0, pad)).reshape(1, LANES)
    a_log_p = jnp.pad(a_log, (0, pad)).reshape(1, LANES)
    d_skip_w = jnp.repeat(d_skip, SSD_HEAD_DIM).reshape(1, SSD_WIDTH)
    conv_b2 = conv_b.reshape(1, SSD_CONV_CH)
    norm_g2 = norm_g.reshape(1, SSD_WIDTH)
    return pl.pallas_call(
        _ssd_body,
        grid=(b, s // L),
        in_specs=[
            pl.BlockSpec((None, L, SSD_WIDTH), lambda bi, c: (bi, c, xs_blk)),
            pl.BlockSpec((None, L, SSD_BC_WIDTH), lambda bi, c: (bi, c, bm_blk)),
            pl.BlockSpec((None, L, SSD_BC_WIDTH), lambda bi, c: (bi, c, cm_blk)),
            pl.BlockSpec((None, L, SSD_WIDTH), lambda bi, c: (bi, c, z_blk)),
            pl.BlockSpec((None, L, LANES), lambda bi, c: (bi, c, 0)),
            small(conv_w), small(conv_b2), small(dt_bias_p), small(a_log_p), small(d_skip_w), small(norm_g2),
        ],
        out_specs=pl.BlockSpec((None, L, SSD_WIDTH), lambda bi, c: (bi, c, 0)),
        out_shape=jax.ShapeDtypeStruct((b, s, SSD_WIDTH), BF16),
        scratch_shapes=[
            pltpu.VMEM((SSD_TAIL, SSD_CONV_CH), BF16),
            pltpu.VMEM((SSD_GROUPS, SSD_STATE, SSD_GROUP_WIDTH), F32),
        ],
        compiler_params=_params("parallel", "arbitrary"),
        name="ssd_mixer",
    )(proj, proj, proj, proj, dt_raw, conv_w, conv_b2, dt_bias_p, a_log_p, d_skip_w, norm_g2)


def _attn_body(q_ref, k_ref, v_ref, cos_ref, sin_ref, o_ref, qkv_ref, acc_ref, m_ref, l_ref, *, seq):
    blk = ATT_BLOCK
    half = ATT_HEAD_DIM // 2
    dils = [d for _, d in DILATED_PATTERNS]
    step = dils[1]
    assert dils == [1, step, step * step] and all(w // d == blk for w, d in DILATED_PATTERNS)
    sub = seq // step
    assert seq // dils[2] == blk

    cos = cos_ref[...]
    sin = sin_ref[...]
    q = q_ref[...].astype(F32)
    k = k_ref[...].astype(F32)
    qkv_ref[0, 0] = (q * cos + pltpu.roll(q, half, 1) * sin) * (ATT_HEAD_DIM ** -0.5 * LOG2_E)
    qkv_ref[0, 1] = k * cos + pltpu.roll(k, half, 1) * sin
    qkv_ref[0, 2] = v_ref[...].astype(F32)
    for t in range(3):
        for r in range(step):
            qkv_ref[1, t, pl.ds(r * sub, sub), :] = qkv_ref[0, t, pl.ds(r, sub, stride=step), :]
    for t in range(3):
        for r in range(step):
            for a in range(step):
                qkv_ref[2, t, pl.ds((r + step * a) * blk, blk), :] = qkv_ref[1, t, pl.ds(r * sub + a, blk, stride=step), :]

    row = lax.broadcasted_iota(jnp.int32, (blk, blk), 0)
    col = lax.broadcasted_iota(jnp.int32, (blk, blk), 1)
    cur_ok = col <= row
    prev_ok = col >= row

    def attend(p, starts, with_prev):
        def load(t, st):
            return qkv_ref[p, t, pl.ds(st, blk), :].astype(BF16)

        def window(t, st):
            return jnp.concatenate([load(t, st - blk), load(t, st)], axis=0) if with_prev else load(t, st)

        qb = jnp.stack([load(0, st) for st in starts])
        kk = jnp.stack([window(1, st) for st in starts])
        keys = kk.shape[1]
        vv = jnp.stack([jnp.concatenate([window(2, st), jnp.ones((keys, LANES), BF16)], axis=1) for st in starts])
        ok = jnp.concatenate([prev_ok, cur_ok], axis=1) if with_prev else cur_ok
        s = lax.dot_general(qb, kk, (((2,), (2,)), ((0,), (0,))), preferred_element_type=F32)
        s = jnp.where(ok[None], s, -jnp.inf)
        m = jnp.max(s, axis=2, keepdims=True)
        e = jnp.exp2(s - m)
        acc = lax.dot_general(e.astype(BF16), vv, (((2,), (1,)), ((0,), (0,))), preferred_element_type=F32)
        for i, st in enumerate(starts):
            rows = pl.ds(st, blk)
            acc_ref[p, rows, :] = acc[i, :, 0:ATT_HEAD_DIM]
            l_ref[p, rows, :] = acc[i, :, ATT_HEAD_DIM:]
            m_ref[p, rows, :] = jnp.broadcast_to(m[i], (blk, LANES))

    def groups(starts, size):
        return [starts[i:i + size] for i in range(0, len(starts), size)]

    for p, dil in enumerate(dils):
        class_rows = seq // dil
        firsts = [r * class_rows for r in range(dil)]
        laters = [r * class_rows + n * blk for r in range(dil) for n in range(1, class_rows // blk)]
        for g in groups(firsts, ATT_GROUP_FIRST):
            attend(p, g, False)
        for g in groups(laters, ATT_GROUP_LATER):
            attend(p, g, True)

    def merged(dst, dst_rows, src, src_rows):
        m_a, m_b = m_ref[dst, dst_rows, :], m_ref[src, src_rows, :]
        top = jnp.maximum(m_a, m_b)
        w_a, w_b = jnp.exp2(m_a - top), jnp.exp2(m_b - top)
        acc = w_a * acc_ref[dst, dst_rows, :] + w_b * acc_ref[src, src_rows, :]
        return top, acc, w_a * l_ref[dst, dst_rows, :] + w_b * l_ref[src, src_rows, :]

    for r in range(step):
        for a in range(step):
            mid_rows = pl.ds(r * sub + a, blk, stride=step)
            top, acc, l = merged(1, mid_rows, 2, pl.ds((r + step * a) * blk, blk))
            m_ref[1, mid_rows, :] = top
            acc_ref[1, mid_rows, :] = acc
            l_ref[1, mid_rows, :] = l
    for r in range(step):
        for n in range(sub // blk):
            nat_rows = pl.ds(r + n * blk * step, blk, stride=step)
            _, acc, l = merged(0, nat_rows, 1, pl.ds(r * sub + n * blk, blk))
            acc_ref[0, nat_rows, :] = acc / l
    o_ref[...] = acc_ref[0].astype(o_ref.dtype)


def _dilated_attention(proj, cos2, sin2):
    b, s, _ = proj.shape
    q_blk = (2 * SSD_WIDTH + 2 * SSD_BC_WIDTH) // ATT_HEAD_DIM
    k_blk = q_blk + ATT_HEADS
    v_blk = k_blk + ATT_HEADS
    n_pat = len(DILATED_PATTERNS)
    head = lambda base: pl.BlockSpec((None, s, ATT_HEAD_DIM), lambda bi, h: (bi, 0, base + h))
    table = pl.BlockSpec((None, s, ATT_HEAD_DIM), lambda bi, h: (bi, 0, 0))
    return pl.pallas_call(
        functools.partial(_attn_body, seq=s),
        grid=(b, ATT_HEADS),
        in_specs=[head(q_blk), head(k_blk), head(v_blk), table, table],
        out_specs=pl.BlockSpec((None, s, ATT_HEAD_DIM), lambda bi, h: (bi, 0, h)),
        out_shape=jax.ShapeDtypeStruct((b, s, ATT_WIDTH), BF16),
        scratch_shapes=[
            pltpu.VMEM((n_pat, 3, s, ATT_HEAD_DIM), F32),
            pltpu.VMEM((n_pat, s, ATT_HEAD_DIM), F32),
            pltpu.VMEM((n_pat, s, LANES), F32),
            pltpu.VMEM((n_pat, s, LANES), F32),
        ],
        compiler_params=_params("parallel", "arbitrary"),
        name="dilated_attention",
    )(proj, proj, proj, cos2, sin2)


def _sgu_body(u_ref, v_ref, g_ref, b_ref, ws_ref, bs_ref, o_ref):
    L = SGU_CHUNK
    v = v_ref[...].astype(F32)
    mu = jnp.mean(v, axis=-1, keepdims=True)
    vc = v - mu
    var = jnp.mean(vc * vc, axis=-1, keepdims=True)
    vn = (vc * lax.rsqrt(var + NORM_EPS) * g_ref[...] + b_ref[...]).astype(BF16)
    row = lax.broadcasted_iota(jnp.int32, (L, L), 0)
    col = lax.broadcasted_iota(jnp.int32, (L, L), 1)
    causal = row >= col
    for g in range(SGU_GROUPS):
        gs = slice(g * SGU_GROUP_WIDTH, (g + 1) * SGU_GROUP_WIDTH)
        w = jnp.where(causal, ws_ref[g], 0.0).astype(BF16)
        mixed = _dot(w, vn[:, gs]) + bs_ref[:, g:g + 1]
        o_ref[:, gs] = (u_ref[:, gs].astype(F32) * mixed).astype(o_ref.dtype)


def _sgu_gate(zz, ln_g, ln_b, w_spatial, b_spatial):
    b, s, _ = zz.shape
    L = SGU_CHUNK
    vec = pl.BlockSpec((1, SGU_WIDTH), lambda bi, c: (0, 0))
    return pl.pallas_call(
        _sgu_body,
        grid=(b, s // L),
        in_specs=[
            pl.BlockSpec((None, L, SGU_WIDTH), lambda bi, c: (bi, c, 0)),
            pl.BlockSpec((None, L, SGU_WIDTH), lambda bi, c: (bi, c, 1)),
            vec, vec,
            pl.BlockSpec((SGU_GROUPS, L, L), lambda bi, c: (0, 0, 0)),
            pl.BlockSpec((L, SGU_GROUPS), lambda bi, c: (0, 0)),
        ],
        out_specs=pl.BlockSpec((None, L, SGU_WIDTH), lambda bi, c: (bi, c, 0)),
        out_shape=jax.ShapeDtypeStruct((b, s, SGU_WIDTH), BF16),
        compiler_params=_params("parallel", "parallel"),
        name="sgu_gate",
    )(zz, zz, ln_g.reshape(1, SGU_WIDTH), ln_b.reshape(1, SGU_WIDTH), w_spatial, b_spatial.T)


def kernel(x, c, positions, w_mod, b_mod, norm_pre, norm_post, ffn_w_gate, ffn_w_up, ffn_w_down, hyb_w_in, hyb_conv_w, hyb_conv_b, hyb_dt_bias, hyb_a_log, hyb_d_skip, hyb_norm_g, hyb_w_out, sgu_w_in, sgu_b_in, sgu_ln_g, sgu_ln_b, sgu_w_spatial, sgu_b_spatial, sgu_w_out):
    depth = w_mod.shape[0]
    b, s, d = x.shape
    n_sub = norm_pre.shape[1]

    mod = _modulation(c, w_mod, b_mod).reshape(depth, b, n_sub, 3, d)
    g_pre = norm_pre.reshape(depth, n_sub, 1, d)
    g_post = norm_post.reshape(depth, n_sub, 1, d)
    w_gate = ffn_w_gate.astype(BF16)
    w_up = ffn_w_up.astype(BF16)
    w_down = ffn_w_down.astype(BF16)

    for layer in range(depth):
        i = layer // 2
        x = _ffn_sublayer(x, mod, g_pre, g_post, w_gate, w_up, w_down, layer, 0, 0)
        if layer % 2 == 0:
            w_in = hyb_w_in[i]
            dt_lo = SSD_WIDTH + SSD_CONV_CH
            dt_hi = dt_lo + SSD_HEADS
            w_dt = jnp.pad(w_in[:, dt_lo:dt_hi], ((0, 0), (0, LANES - SSD_HEADS))).astype(BF16)
            proj, dt_raw = _hyb_in_proj(x, mod, g_pre, w_in[:, :dt_lo].astype(BF16), w_in[:, dt_hi:].astype(BF16),
                                        w_dt, layer, 1)
            cos2, sin2 = _rope_tables(positions)
            y_a = _ssd_mixer(proj, dt_raw, hyb_conv_w[i], hyb_conv_b[i], hyb_dt_bias[i], hyb_a_log[i],
                             hyb_d_skip[i], hyb_norm_g[i])
            y_b = _dilated_attention(proj, cos2, sin2)
            x = _out_proj_residual(y_a, 0, y_b, 0, hyb_w_out[i].astype(BF16), x, mod, g_post, layer, 1)
        else:
            zz = _pre_norm_proj(x, mod, g_pre, sgu_w_in[i].astype(BF16), sgu_b_in[i].reshape(1, -1), layer, 1,
                                PROJ_TILE, BF16, gelu=True)
            gated = _sgu_gate(zz, sgu_ln_g[i], sgu_ln_b[i], sgu_w_spatial[i], sgu_b_spatial[i])
            x = _out_proj_residual(gated, 0, gated, 1, sgu_w_out[i].astype(BF16), x, mod, g_post, layer, 1)
        x = _ffn_sublayer(x, mod, g_pre, g_post, w_gate, w_up, w_down, layer, 2, 1)
    return x
```

```python
import functools

import jax
import jax.numpy as jnp
from jax import lax
from jax.experimental import pallas as pl
from jax.experimental.pallas import tpu as pltpu

NORM_EPS = 1e-6
LOG2_E = 1.4426950408889634
FFN_RES_WEIGHT = 0.5
MIXER_RES_WEIGHT = 1.0

SSD_HEADS = 32
SSD_HEAD_DIM = 64
SSD_WIDTH = SSD_HEADS * SSD_HEAD_DIM
SSD_GROUPS = 4
SSD_STATE = 128
SSD_CONV = 4
SSD_CHUNK = 128
SSD_TAIL = 16
SSD_BC_WIDTH = SSD_GROUPS * SSD_STATE
SSD_CONV_CH = SSD_WIDTH + 2 * SSD_BC_WIDTH
SSD_HEADS_PER_GROUP = SSD_HEADS // SSD_GROUPS
SSD_GROUP_WIDTH = SSD_HEADS_PER_GROUP * SSD_HEAD_DIM

ATT_HEADS = 16
ATT_HEAD_DIM = 128
ATT_WIDTH = ATT_HEADS * ATT_HEAD_DIM
ATT_BLOCK = 128
DILATED_PATTERNS = ((128, 1), (512, 4), (2048, 16))
ROPE_THETA = 10000.0

SGU_WIDTH = 4096
SGU_GROUPS = 8
SGU_CHUNK = 128
SGU_GROUP_WIDTH = SGU_WIDTH // SGU_GROUPS

LANES = 128
SUBLANES = 8
VMEM_LIMIT_BYTES = 56 * 1024 * 1024

MIX_ROWS = 512
MIX_STEPS = 2
FFN_ROWS = 1024
FFN_TILE = 512
FFN_NORM_CHUNKS = 8
FFN_NORM_ROWS = 16
PROJ_ROWS = 1024
PROJ_TILE = 1024
PROJ_NORM_CHUNKS = 8
MOD_TILE = 1024
ATT_GROUP_FIRST = 8
ATT_GROUP_LATER = 5

BF16 = jnp.bfloat16
F32 = jnp.float32


def _params(*semantics):
    return pltpu.CompilerParams(dimension_semantics=semantics, vmem_limit_bytes=VMEM_LIMIT_BYTES)


def _rms(x):
    return x * lax.rsqrt(jnp.mean(x * x, axis=-1, keepdims=True) + NORM_EPS)


def _silu(x):
    h = 0.5 * x
    return h + h * jnp.tanh(h)


def _dot(a, b):
    return jnp.dot(a, b, preferred_element_type=F32)


def _dot_exact(a, b):
    return jnp.dot(a, b, preferred_element_type=F32, precision=lax.Precision.HIGHEST)


def _dot_nt(a, b):
    return lax.dot_general(a, b, (((1,), (1,)), ((), ())), preferred_element_type=F32)


def _mod_body(c_ref, w_ref, b_ref, o_ref):
    ca = _silu(c_ref[...]).astype(BF16)
    o_ref[...] = _dot(ca, w_ref[...].astype(BF16)) + b_ref[...]


def _modulation(c, w_mod, b_mod):
    depth, d, n = w_mod.shape
    b = c.shape[0]
    return pl.pallas_call(
        _mod_body,
        grid=(depth, n // MOD_TILE),
        in_specs=[
            pl.BlockSpec((b, d), lambda l, j: (0, 0)),
            pl.BlockSpec((None, d, MOD_TILE), lambda l, j: (l, 0, j)),
            pl.BlockSpec((None, 1, MOD_TILE), lambda l, j: (l, 0, j)),
        ],
        out_specs=pl.BlockSpec((None, b, MOD_TILE), lambda l, j: (l, 0, j)),
        out_shape=jax.ShapeDtypeStruct((depth, b, n), F32),
        compiler_params=_params("parallel", "parallel"),
        name="modulation",
    )(c, w_mod, b_mod.reshape(depth, 1, n))


def _ffn_body(xn_ref, xp_ref, modn_ref, modp_ref, gpre_ref, gpost_ref, wg_ref, wu_ref, wd_ref, o_ref,
              h0_ref, h1_ref, acc0_ref, acc1_ref, *, n_tiles, n_f):
    g = pl.program_id(0)
    f = pl.program_id(1)
    chunk = FFN_ROWS // FFN_NORM_CHUNKS
    c0 = jnp.minimum(f, FFN_NORM_CHUNKS - 1) * chunk

    @pl.when((g == 0) & (f == 0))
    def _():
        for ref in (h0_ref, h1_ref, acc0_ref, acc1_ref):
            ref[...] = jnp.zeros_like(ref)

    def pre_norm_chunk(h_ref):
        gain = gpre_ref[...] * (1.0 + modn_ref[1:2, :])
        for q in range(chunk // FFN_NORM_ROWS):
            rows = pl.ds(q * FFN_NORM_ROWS, FFN_NORM_ROWS)
            tile_rows = pl.ds(pl.multiple_of(c0 + q * FFN_NORM_ROWS, FFN_NORM_ROWS), FFN_NORM_ROWS)
            h_ref[tile_rows, :] = (_rms(xn_ref[rows, :]) * gain + modn_ref[0:1, :]).astype(BF16)

    def post_norm_chunk(acc_ref):
        gain = (FFN_RES_WEIGHT * (1.0 + modp_ref[2:3, :])) * gpost_ref[...]
        for q in range(chunk // FFN_NORM_ROWS):
            rows = pl.ds(q * FFN_NORM_ROWS, FFN_NORM_ROWS)
            tile_rows = pl.ds(pl.multiple_of(c0 + q * FFN_NORM_ROWS, FFN_NORM_ROWS), FFN_NORM_ROWS)
            o_ref[rows, :] = xp_ref[rows, :] + _rms(acc_ref[tile_rows, :]) * gain

    def swiglu_step(h_ref, acc_ref):
        h = h_ref[...]
        a = (_silu(_dot(h, wg_ref[...])) * _dot(h, wu_ref[...])).astype(BF16)
        acc_ref[...] = jnp.where(f > 0, acc_ref[...], 0.0) + _dot(a, wd_ref[...])

    has_matmul = (g >= 1) & (g <= n_tiles)
    for parity, (h_new, acc_old, h_mid, acc_mid) in enumerate(
            [(h0_ref, acc0_ref, h1_ref, acc1_ref), (h1_ref, acc1_ref, h0_ref, acc0_ref)]):
        @pl.when((lax.rem(g, 2) == parity) & has_matmul)
        def _():
            post_norm_chunk(acc_old)
            swiglu_step(h_mid, acc_mid)
            pre_norm_chunk(h_new)

        @pl.when((lax.rem(g, 2) == parity) & jnp.logical_not(has_matmul))
        def _():
            post_norm_chunk(acc_old)
            pre_norm_chunk(h_new)


def _ffn_sublayer(x, mod, g_pre, g_post, w_gate, w_up, w_down, layer, sub, idx):
    b, s, d = x.shape
    f_dim = w_gate.shape[-1]
    n_f = f_dim // FFN_TILE
    per_batch = s // FFN_ROWS
    n_tiles = b * per_batch
    chunks = FFN_NORM_CHUNKS
    chunk = FFN_ROWS // chunks
    assert n_f >= chunks
    new_tile = lambda g: jnp.minimum(g, n_tiles - 1)
    old_tile = lambda g: jnp.clip(g - 2, 0, n_tiles - 1)
    new_chunk = lambda g, f: (new_tile(g) * chunks + jnp.minimum(f, chunks - 1), 0, 0)
    old_chunk = lambda g, f: (old_tile(g) * chunks + jnp.minimum(f, chunks - 1), 0, 0)
    out_chunk = lambda g, f: (jnp.where(g < 2, 0, old_chunk(g, f)[0]), 0, 0)
    w_step = lambda g, f: jnp.where(g == 0, 0, jnp.where(g == n_tiles + 1, n_f - 1, f))
    x_chunks = x.reshape(n_tiles * chunks, chunk, d)
    out = pl.pallas_call(
        functools.partial(_ffn_body, n_tiles=n_tiles, n_f=n_f),
        grid=(n_tiles + 2, n_f),
        in_specs=[
            pl.BlockSpec((None, chunk, d), new_chunk),
            pl.BlockSpec((None, chunk, d), old_chunk),
            pl.BlockSpec((None, None, None, 3, d), lambda g, f: (layer, new_tile(g) // per_batch, sub, 0, 0)),
            pl.BlockSpec((None, None, None, 3, d), lambda g, f: (layer, old_tile(g) // per_batch, sub, 0, 0)),
            pl.BlockSpec((None, None, 1, d), lambda g, f: (layer, sub, 0, 0)),
            pl.BlockSpec((None, None, 1, d), lambda g, f: (layer, sub, 0, 0)),
            pl.BlockSpec((None, None, d, FFN_TILE), lambda g, f: (layer, idx, 0, w_step(g, f))),
            pl.BlockSpec((None, None, d, FFN_TILE), lambda g, f: (layer, idx, 0, w_step(g, f))),
            pl.BlockSpec((None, None, FFN_TILE, d), lambda g, f: (layer, idx, w_step(g, f), 0)),
        ],
        out_specs=pl.BlockSpec((None, chunk, d), out_chunk),
        out_shape=jax.ShapeDtypeStruct(x_chunks.shape, F32),
        scratch_shapes=[pltpu.VMEM((FFN_ROWS, d), BF16), pltpu.VMEM((FFN_ROWS, d), BF16),
                        pltpu.VMEM((FFN_ROWS, d), F32), pltpu.VMEM((FFN_ROWS, d), F32)],
        compiler_params=_params("arbitrary", "arbitrary"),
        name="ffn_sublayer",
    )(x_chunks, x_chunks, mod, mod, g_pre, g_post, w_gate, w_up, w_down)
    return out.reshape(b, s, d)


def _gelu_tanh(x):
    return 0.5 * x * (1.0 + jnp.tanh(0.7978845608028654 * (x + 0.044715 * (x * x * x))))


def _in_proj_body(*refs, hyb, n_a):
    if hyb:
        xn_ref, modn_ref, gpre_ref, wa_ref, wb_ref, wdt_ref, o_ref, dt_ref, h0_ref, h1_ref = refs
    else:
        xn_ref, modn_ref, gpre_ref, wa_ref, bias_ref, o_ref, h0_ref, h1_ref = refs
    g = pl.program_id(0)
    j = pl.program_id(1)
    chunk = PROJ_ROWS // PROJ_NORM_CHUNKS
    c0 = jnp.minimum(j, PROJ_NORM_CHUNKS - 1) * chunk

    @pl.when((g == 0) & (j == 0))
    def _():
        h0_ref[...] = jnp.zeros_like(h0_ref)
        h1_ref[...] = jnp.zeros_like(h1_ref)

    def pre_norm_chunk(h_ref):
        gain = gpre_ref[...] * (1.0 + modn_ref[1:2, :])
        for q in range(chunk // FFN_NORM_ROWS):
            rows = pl.ds(q * FFN_NORM_ROWS, FFN_NORM_ROWS)
            tile_rows = pl.ds(pl.multiple_of(c0 + q * FFN_NORM_ROWS, FFN_NORM_ROWS), FFN_NORM_ROWS)
            h_ref[tile_rows, :] = (_rms(xn_ref[rows, :]) * gain + modn_ref[0:1, :]).astype(BF16)

    def project(h_ref, w_ref):
        y = _dot(h_ref[...], w_ref[...])
        if not hyb:
            y = _gelu_tanh(y + bias_ref[...])
        o_ref[...] = y.astype(o_ref.dtype)

    has_matmul = g >= 1
    for parity, (h_new, h_mid) in enumerate([(h0_ref, h1_ref), (h1_ref, h0_ref)]):
        mine = lax.rem(g, 2) == parity
        if hyb:
            @pl.when(mine & has_matmul & (j == 0))
            def _():
                dt_ref[...] = _dot(h_mid[...], wdt_ref[...])

            @pl.when(mine & has_matmul & (j < n_a))
            def _():
                project(h_mid, wa_ref)
                pre_norm_chunk(h_new)

            @pl.when(mine & has_matmul & (j >= n_a))
            def _():
                project(h_mid, wb_ref)
                pre_norm_chunk(h_new)
        else:
            @pl.when(mine & has_matmul)
            def _():
                project(h_mid, wa_ref)
                pre_norm_chunk(h_new)

        @pl.when(mine & jnp.logical_not(has_matmul))
        def _():
            pre_norm_chunk(h_new)


def _in_proj(x, mod, g_pre, weights, layer, sub, hyb):
    b, s, d = x.shape
    per_batch = s // PROJ_ROWS
    n_tiles = b * per_batch
    chunks = PROJ_NORM_CHUNKS
    chunk = PROJ_ROWS // chunks
    w_a = weights[0]
    n_a = w_a.shape[1] // PROJ_TILE
    n_steps = n_a + (weights[1].shape[1] // PROJ_TILE if hyb else 0)
    assert n_steps >= chunks
    new_tile = lambda g: jnp.minimum(g, n_tiles - 1)
    mid_tile = lambda g: jnp.maximum(g - 1, 0)
    col = lambda g, j: jnp.where(g == 0, 0, j)
    in_specs = [
        pl.BlockSpec((None, chunk, d), lambda g, j: (new_tile(g) * chunks + jnp.minimum(j, chunks - 1), 0, 0)),
        pl.BlockSpec((None, None, None, 3, d), lambda g, j: (layer, new_tile(g) // per_batch, sub, 0, 0)),
        pl.BlockSpec((None, None, 1, d), lambda g, j: (layer, sub, 0, 0)),
        pl.BlockSpec((d, PROJ_TILE), lambda g, j: (0, jnp.minimum(col(g, j), n_a - 1))),
    ]
    out_specs = [pl.BlockSpec((None, PROJ_ROWS, PROJ_TILE), lambda g, j: (mid_tile(g), 0, col(g, j)))]
    out_shape = [jax.ShapeDtypeStruct((n_tiles, PROJ_ROWS, n_steps * PROJ_TILE), BF16)]
    if hyb:
        in_specs += [
            pl.BlockSpec((d, PROJ_TILE), lambda g, j: (0, jnp.maximum(col(g, j) - n_a, 0))),
            pl.BlockSpec((d, LANES), lambda g, j: (0, 0)),
        ]
        out_specs.append(pl.BlockSpec((None, PROJ_ROWS, LANES), lambda g, j: (mid_tile(g), 0, 0)))
        out_shape.append(jax.ShapeDtypeStruct((n_tiles, PROJ_ROWS, LANES), F32))
    else:
        in_specs.append(pl.BlockSpec((1, PROJ_TILE), lambda g, j: (0, col(g, j))))
    outs = pl.pallas_call(
        functools.partial(_in_proj_body, hyb=hyb, n_a=n_a),
        grid=(n_tiles + 1, n_steps),
        in_specs=in_specs,
        out_specs=out_specs,
        out_shape=out_shape,
        scratch_shapes=[pltpu.VMEM((PROJ_ROWS, d), BF16)] * 2,
        compiler_params=_params("arbitrary", "arbitrary"),
        name="in_proj",
    )(x.reshape(n_tiles * chunks, chunk, d), mod, g_pre, *weights)
    return [o.reshape(b, s, o.shape[-1]) for o in outs]


def _sgu_gate_chunk(u, v, lng_ref, lnb_ref, ws_ref, bs_ref):
    L = SGU_CHUNK
    v = v.astype(F32)
    mu = jnp.mean(v, axis=-1, keepdims=True)
    vc = v - mu
    var = jnp.mean(vc * vc, axis=-1, keepdims=True)
    vn = (vc * lax.rsqrt(var + NORM_EPS) * lng_ref[...] + lnb_ref[...]).astype(BF16)
    row = lax.broadcasted_iota(jnp.int32, (L, L), 0)
    col = lax.broadcasted_iota(jnp.int32, (L, L), 1)
    causal = row >= col
    parts = []
    for g in range(SGU_GROUPS):
        gs = slice(g * SGU_GROUP_WIDTH, (g + 1) * SGU_GROUP_WIDTH)
        w = jnp.where(causal, ws_ref[g], 0.0).astype(BF16)
        mixed = _dot(w, vn[:, gs]) + bs_ref[:, g:g + 1]
        parts.append((u[:, gs].astype(F32) * mixed).astype(BF16))
    return jnp.concatenate(parts, axis=1)


def _mixer_out_body(*refs, sgu, n_tiles):
    if sgu:
        (u_ref, v_ref, lng_ref, lnb_ref, ws_ref, bs_ref, w_ref, xp_ref, modp_ref, gpost_ref, o_ref,
         lhs0_ref, lhs1_ref, acc0_ref, acc1_ref) = refs
    else:
        a_ref, b_ref, w_ref, xp_ref, modp_ref, gpost_ref, o_ref, acc0_ref, acc1_ref = refs
        lhs0_ref = lhs1_ref = None
    g = pl.program_id(0)
    k = pl.program_id(1)
    chunk = MIX_ROWS // MIX_STEPS
    half = w_ref.shape[0]

    @pl.when((g == 0) & (k == 0))
    def _():
        for ref in (lhs0_ref, lhs1_ref, acc0_ref, acc1_ref):
            if ref is not None:
                ref[...] = jnp.zeros_like(ref)

    def post_norm_chunk(acc_ref):
        gain = (MIXER_RES_WEIGHT * (1.0 + modp_ref[2:3, :])) * gpost_ref[...]
        for q in range(chunk // FFN_NORM_ROWS):
            rows = pl.ds(q * FFN_NORM_ROWS, FFN_NORM_ROWS)
            tile_rows = pl.ds(pl.multiple_of(k * chunk + q * FFN_NORM_ROWS, FFN_NORM_ROWS), FFN_NORM_ROWS)
            o_ref[rows, :] = xp_ref[rows, :] + _rms(acc_ref[tile_rows, :]) * gain

    def gate_chunk(lhs_ref):
        for j in range(chunk // SGU_CHUNK):
            rows = slice(j * SGU_CHUNK, (j + 1) * SGU_CHUNK)
            gated = _sgu_gate_chunk(u_ref[rows, :], v_ref[rows, :], lng_ref, lnb_ref, ws_ref, bs_ref)
            tile_rows = pl.ds(pl.multiple_of(k * chunk + j * SGU_CHUNK, SGU_CHUNK), SGU_CHUNK)
            for part in range(MIX_STEPS):
                lhs_ref[part, tile_rows, :] = gated[:, part * half:(part + 1) * half]

    def matmul_step(lhs_ref, acc_ref):
        lhs = lhs_ref[k] if sgu else jnp.where(k == 0, a_ref[...], b_ref[...])
        acc_ref[...] = jnp.where(k > 0, acc_ref[...], 0.0) + _dot(lhs, w_ref[...])

    has_matmul = (g >= 1) & (g <= n_tiles)
    for parity, (lhs_new, acc_old, lhs_mid, acc_mid) in enumerate(
            [(lhs0_ref, acc0_ref, lhs1_ref, acc1_ref), (lhs1_ref, acc1_ref, lhs0_ref, acc0_ref)]):
        @pl.when((lax.rem(g, 2) == parity) & has_matmul)
        def _():
            post_norm_chunk(acc_old)
            matmul_step(lhs_mid, acc_mid)
            if sgu:
                gate_chunk(lhs_new)

        @pl.when((lax.rem(g, 2) == parity) & jnp.logical_not(has_matmul))
        def _():
            post_norm_chunk(acc_old)
            if sgu:
                gate_chunk(lhs_new)


def _mixer_out(operands, w, x, mod, g_post, layer, sub, sgu):
    b, s, d = x.shape
    half = w.shape[0] // MIX_STEPS
    per_batch = s // MIX_ROWS
    n_tiles = b * per_batch
    steps = MIX_STEPS
    chunk = MIX_ROWS // steps
    new_tile = lambda g: jnp.minimum(g, n_tiles - 1)
    mid_tile = lambda g: jnp.clip(g - 1, 0, n_tiles - 1)
    old_tile = lambda g: jnp.clip(g - 2, 0, n_tiles - 1)
    new_chunk = lambda g, k: new_tile(g) * steps + k
    old_chunk = lambda g, k: old_tile(g) * steps + k
    out_chunk = lambda g, k: (jnp.where(g < 2, 0, old_chunk(g, k)), 0, 0)
    w_step = lambda g, k: jnp.where(g == 0, 0, jnp.where(g == n_tiles + 1, steps - 1, k))
    x_chunks = x.reshape(n_tiles * steps, chunk, d)
    common_specs = [
        pl.BlockSpec((half, d), lambda g, k: (w_step(g, k), 0)),
        pl.BlockSpec((None, chunk, d), lambda g, k: (old_chunk(g, k), 0, 0)),
        pl.BlockSpec((None, None, None, 3, d), lambda g, k: (layer, old_tile(g) // per_batch, sub, 0, 0)),
        pl.BlockSpec((None, None, 1, d), lambda g, k: (layer, sub, 0, 0)),
    ]
    acc = [pltpu.VMEM((MIX_ROWS, d), F32)] * 2
    if sgu:
        zz, ln_g, ln_b, w_spatial, b_spatial = operands
        zz_chunks = zz.reshape(n_tiles * steps, chunk, 2 * SGU_WIDTH)
        vec = pl.BlockSpec((1, SGU_WIDTH), lambda g, k: (0, 0))
        lhs_specs = [
            pl.BlockSpec((None, chunk, SGU_WIDTH), lambda g, k: (new_chunk(g, k), 0, 0)),
            pl.BlockSpec((None, chunk, SGU_WIDTH), lambda g, k: (new_chunk(g, k), 0, 1)),
            vec, vec,
            pl.BlockSpec((SGU_GROUPS, SGU_CHUNK, SGU_CHUNK), lambda g, k: (0, 0, 0)),
            pl.BlockSpec((SGU_CHUNK, SGU_GROUPS), lambda g, k: (0, 0)),
        ]
        lhs_args = [zz_chunks, zz_chunks, ln_g.reshape(1, SGU_WIDTH), ln_b.reshape(1, SGU_WIDTH), w_spatial,
                    b_spatial.T]
        scratch = [pltpu.VMEM((steps, MIX_ROWS, half), BF16)] * 2 + acc
    else:
        y_a, y_b = operands
        tile = pl.BlockSpec((None, MIX_ROWS, half), lambda g, k: (mid_tile(g), 0, 0))
        lhs_specs = [tile, tile]
        lhs_args = [y_a.reshape(n_tiles, MIX_ROWS, half), y_b.reshape(n_tiles, MIX_ROWS, half)]
        scratch = acc
    out = pl.pallas_call(
        functools.partial(_mixer_out_body, sgu=sgu, n_tiles=n_tiles),
        grid=(n_tiles + 2, steps),
        in_specs=lhs_specs + common_specs,
        out_specs=pl.BlockSpec((None, chunk, d), out_chunk),
        out_shape=jax.ShapeDtypeStruct(x_chunks.shape, F32),
        scratch_shapes=scratch,
        compiler_params=_params("arbitrary", "arbitrary"),
        name="mixer_out",
    )(*lhs_args, w, x_chunks, mod, g_post)
    return out.reshape(b, s, d)


def _rope_body(pos_ref, freq_ref, cos_ref, sin_ref):
    ang = pos_ref[...].astype(F32) * freq_ref[...]
    lane = lax.broadcasted_iota(jnp.int32, ang.shape, 1)
    cos_ref[...] = jnp.cos(ang)
    sin_ref[...] = jnp.where(lane < ATT_HEAD_DIM // 2, -1.0, 1.0) * jnp.sin(ang)


def _rope_tables(positions):
    b, s = positions.shape
    half = ATT_HEAD_DIM // 2
    inv_freq = ROPE_THETA ** (-jnp.arange(half, dtype=F32) / half)
    freq = jnp.concatenate([inv_freq, inv_freq]).reshape(1, ATT_HEAD_DIM)
    tile = 512
    spec = pl.BlockSpec((None, tile, ATT_HEAD_DIM), lambda bi, i: (bi, i, 0))
    return pl.pallas_call(
        _rope_body,
        grid=(b, s // tile),
        in_specs=[
            pl.BlockSpec((None, tile, 1), lambda bi, i: (bi, i, 0)),
            pl.BlockSpec((1, ATT_HEAD_DIM), lambda bi, i: (0, 0)),
        ],
        out_specs=[spec, spec],
        out_shape=[jax.ShapeDtypeStruct((b, s, ATT_HEAD_DIM), F32)] * 2,
        compiler_params=_params("parallel", "parallel"),
        name="rope_tables",
    )(positions.reshape(b, s, 1), freq)


def _ssd_body(xs_ref, bm_ref, cm_ref, z_ref, dt_ref, cw_ref, cb_ref, dtb_ref, alog_ref, dskip_ref, ng_ref,
              o_ref, tail_ref, state_ref):
    L = SSD_CHUNK
    T = SSD_TAIL

    @pl.when(pl.program_id(1) == 0)
    def _():
        tail_ref[...] = jnp.zeros_like(tail_ref)
        state_ref[...] = jnp.zeros_like(state_ref)

    srow = lax.broadcasted_iota(jnp.int32, ((SSD_CONV - 1) * L, L + T), 0)
    scol = lax.broadcasted_iota(jnp.int32, ((SSD_CONV - 1) * L, L + T), 1)
    lag = (srow >> (L.bit_length() - 1)) + 1
    t_in = srow & (L - 1)
    shift = (scol == jnp.where(t_in >= lag, t_in - lag, t_in - lag + (L + T))).astype(BF16)

    def conv_silu(raw_ref, lo, hi):
        raw = raw_ref[...]
        lagged = _dot(shift, jnp.concatenate([raw, tail_ref[:, lo:hi]], axis=0))
        acc = cb_ref[:, lo:hi] + raw.astype(F32) * cw_ref[SSD_CONV - 1:SSD_CONV, lo:hi]
        for k in range(1, SSD_CONV):
            acc = acc + lagged[(k - 1) * L:k * L] * cw_ref[SSD_CONV - 1 - k:SSD_CONV - k, lo:hi]
        tail_ref[:, lo:hi] = raw[L - T:L]
        return _silu(acc)

    xs = conv_silu(xs_ref, 0, SSD_WIDTH)
    bm = conv_silu(bm_ref, SSD_WIDTH, SSD_WIDTH + SSD_BC_WIDTH)
    cm = conv_silu(cm_ref, SSD_WIDTH + SSD_BC_WIDTH, SSD_CONV_CH)

    dt_in = dt_ref[...] + dtb_ref[...]
    dt = jnp.maximum(dt_in, 0.0) + jnp.log1p(jnp.exp(-jnp.abs(dt_in)))
    adt = dt * (-jnp.exp(alog_ref[...]))
    row = lax.broadcasted_iota(jnp.int32, (L, L), 0)
    col = lax.broadcasted_iota(jnp.int32, (L, L), 1)
    causal = row >= col
    acs = _dot_exact(causal.astype(F32), adt)
    acs_t = acs.T
    dt_t = dt.T
    acs_last = acs[L - 1:L, :]

    hrow = lax.broadcasted_iota(jnp.int32, (LANES, SSD_WIDTH), 0)
    hcol = lax.broadcasted_iota(jnp.int32, (LANES, SSD_WIDTH), 1)
    expand = (hrow == (hcol >> (SSD_HEAD_DIM.bit_length() - 1))).astype(BF16)
    stacked = jnp.concatenate([jnp.exp(acs), jnp.exp(acs_last - acs) * dt], axis=0)
    high = stacked.astype(BF16)
    rest = (stacked - high.astype(F32)).astype(BF16)
    wide = _dot(high, expand) + _dot(rest, expand)
    decay_in_w, dt_decay_out_w = wide[0:L], wide[L:2 * L]
    chunk_decay_w = decay_in_w[L - 1:L, :]

    lane = lax.broadcasted_iota(jnp.int32, (L, LANES), 1)
    first_head = lane < SSD_HEAD_DIM

    y_parts = []
    for g in range(SSD_GROUPS):
        gs = slice(g * SSD_GROUP_WIDTH, (g + 1) * SSD_GROUP_WIDTH)
        bg = bm[:, g * SSD_STATE:(g + 1) * SSD_STATE]
        cg = cm[:, g * SSD_STATE:(g + 1) * SSD_STATE].astype(BF16)
        cb = _dot_nt(cg, bg.astype(BF16))
        state = state_ref[g]
        y_off = _dot(cg, state.astype(BF16)) * decay_in_w[:, gs]
        y_diag = []
        for j in range(SSD_HEADS_PER_GROUP // 2):
            h0 = g * SSD_HEADS_PER_GROUP + 2 * j
            ms = []
            for h in (h0, h0 + 1):
                seg = acs[:, h:h + 1] - acs_t[h:h + 1, :]
                ms.append(cb * jnp.exp(jnp.where(causal, seg, -jnp.inf)) * dt_t[h:h + 1, :])
            lhs = jnp.concatenate(ms, axis=1).astype(BF16)
            xp = xs[:, h0 * SSD_HEAD_DIM:(h0 + 2) * SSD_HEAD_DIM]
            rhs = jnp.concatenate([jnp.where(first_head, xp, 0.0), jnp.where(first_head, 0.0, xp)], axis=0)
            y_diag.append(_dot(lhs, rhs.astype(BF16)))
        y_parts.append(jnp.concatenate(y_diag, axis=1) + y_off)
        contrib = _dot(bg.T.astype(BF16), (xs[:, gs] * dt_decay_out_w[:, gs]).astype(BF16))
        state_ref[g] = state * chunk_decay_w[:, gs] + contrib

    y = jnp.concatenate(y_parts, axis=1) + xs * dskip_ref[...]
    y = y * _silu(z_ref[...].astype(F32))
    o_ref[...] = (_rms(y) * ng_ref[...]).astype(o_ref.dtype)


def _ssd_mixer(proj, dt_raw, conv_w, conv_b, dt_bias, a_log, d_skip, norm_g):
    b, s, _ = proj.shape
    assert proj.dtype == BF16
    L = SSD_CHUNK
    z_blk = 0
    xs_blk = SSD_WIDTH // SSD_WIDTH
    bm_blk = (2 * SSD_WIDTH) // SSD_BC_WIDTH
    cm_blk = bm_blk + 1
    pad = LANES - SSD_HEADS
    small = lambda a: pl.BlockSpec(a.shape, lambda bi, c: (0, 0))
    dt_bias_p = jnp.pad(dt_bias, (0, pad)).reshape(1, LANES)
    a_log_p = jnp.pad(a_log, (0, pad)).reshape(1, LANES)
    d_skip_w = jnp.repeat(d_skip, SSD_HEAD_DIM).reshape(1, SSD_WIDTH)
    conv_b2 = conv_b.reshape(1, SSD_CONV_CH)
    norm_g2 = norm_g.reshape(1, SSD_WIDTH)
    return pl.pallas_call(
        _ssd_body,
        grid=(b, s // L),
        in_specs=[
            pl.BlockSpec((None, L, SSD_WIDTH), lambda bi, c: (bi, c, xs_blk)),
            pl.BlockSpec((None, L, SSD_BC_WIDTH), lambda bi, c: (bi, c, bm_blk)),
            pl.BlockSpec((None, L, SSD_BC_WIDTH), lambda bi, c: (bi, c, cm_blk)),
            pl.BlockSpec((None, L, SSD_WIDTH), lambda bi, c: (bi, c, z_blk)),
            pl.BlockSpec((None, L, LANES), lambda bi, c: (bi, c, 0)),
            small(conv_w), small(conv_b2), small(dt_bias_p), small(a_log_p), small(d_skip_w), small(norm_g2),
        ],
        out_specs=pl.BlockSpec((None, L, SSD_WIDTH), lambda bi, c: (bi, c, 0)),
        out_shape=jax.ShapeDtypeStruct((b, s, SSD_WIDTH), BF16),
        scratch_shapes=[
            pltpu.VMEM((SSD_TAIL, SSD_CONV_CH), BF16),
            pltpu.VMEM((SSD_GROUPS, SSD_STATE, SSD_GROUP_WIDTH), F32),
        ],
        compiler_params=_params("parallel", "arbitrary"),
        name="ssd_mixer",
    )(proj, proj, proj, proj, dt_raw, conv_w, conv_b2, dt_bias_p, a_log_p, d_skip_w, norm_g2)


def _attn_body(q_ref, k_ref, v_ref, cos_ref, sin_ref, o_ref, qkv_ref, acc_ref, m_ref, l_ref, *, seq):
    blk = ATT_BLOCK
    half = ATT_HEAD_DIM // 2
    dils = [d for _, d in DILATED_PATTERNS]
    step = dils[1]
    assert dils == [1, step, step * step] and all(w // d == blk for w, d in DILATED_PATTERNS)
    sub = seq // step
    assert seq // dils[2] == blk

    cos = cos_ref[...]
    sin = sin_ref[...]
    q = q_ref[...].astype(F32)
    k = k_ref[...].astype(F32)
    qkv_ref[0, 0] = (q * cos + pltpu.roll(q, half, 1) * sin) * (ATT_HEAD_DIM ** -0.5 * LOG2_E)
    qkv_ref[0, 1] = k * cos + pltpu.roll(k, half, 1) * sin
    qkv_ref[0, 2] = v_ref[...].astype(F32)
    for t in range(3):
        for r in range(step):
            qkv_ref[1, t, pl.ds(r * sub, sub), :] = qkv_ref[0, t, pl.ds(r, sub, stride=step), :]
    for t in range(3):
        for r in range(step):
            for a in range(step):
                qkv_ref[2, t, pl.ds((r + step * a) * blk, blk), :] = qkv_ref[1, t, pl.ds(r * sub + a, blk, stride=step), :]

    row = lax.broadcasted_iota(jnp.int32, (blk, blk), 0)
    col = lax.broadcasted_iota(jnp.int32, (blk, blk), 1)
    cur_ok = col <= row
    prev_ok = col >= row

    def attend(p, starts, with_prev):
        def load(t, st):
            return qkv_ref[p, t, pl.ds(st, blk), :].astype(BF16)

        def window(t, st):
            return jnp.concatenate([load(t, st - blk), load(t, st)], axis=0) if with_prev else load(t, st)

        qb = jnp.stack([load(0, st) for st in starts])
        kk = jnp.stack([window(1, st) for st in starts])
        keys = kk.shape[1]
        vv = jnp.stack([jnp.concatenate([window(2, st), jnp.ones((keys, LANES), BF16)], axis=1) for st in starts])
        ok = jnp.concatenate([prev_ok, cur_ok], axis=1) if with_prev else cur_ok
        s = lax.dot_general(qb, kk, (((2,), (2,)), ((0,), (0,))), preferred_element_type=F32)
        s = jnp.where(ok[None], s, -jnp.inf)
        m = jnp.max(s, axis=2, keepdims=True)
        e = jnp.exp2(s - m)
        acc = lax.dot_general(e.astype(BF16), vv, (((2,), (1,)), ((0,), (0,))), preferred_element_type=F32)
        for i, st in enumerate(starts):
            rows = pl.ds(st, blk)
            acc_ref[p, rows, :] = acc[i, :, 0:ATT_HEAD_DIM]
            l_ref[p, rows, :] = acc[i, :, ATT_HEAD_DIM:]
            m_ref[p, rows, :] = jnp.broadcast_to(m[i], (blk, LANES))

    def groups(starts, size):
        return [starts[i:i + size] for i in range(0, len(starts), size)]

    for p, dil in enumerate(dils):
        class_rows = seq // dil
        firsts = [r * class_rows for r in range(dil)]
        laters = [r * class_rows + n * blk for r in range(dil) for n in range(1, class_rows // blk)]
        for g in groups(firsts, ATT_GROUP_FIRST):
            attend(p, g, False)
        for g in groups(laters, ATT_GROUP_LATER):
            attend(p, g, True)

    def merged(dst, dst_rows, src, src_rows):
        m_a, m_b = m_ref[dst, dst_rows, :], m_ref[src, src_rows, :]
        top = jnp.maximum(m_a, m_b)
        w_a, w_b = jnp.exp2(m_a - top), jnp.exp2(m_b - top)
        acc = w_a * acc_ref[dst, dst_rows, :] + w_b * acc_ref[src, src_rows, :]
        return top, acc, w_a * l_ref[dst, dst_rows, :] + w_b * l_ref[src, src_rows, :]

    for r in range(step):
        for a in range(step):
            mid_rows = pl.ds(r * sub + a, blk, stride=step)
            top, acc, l = merged(1, mid_rows, 2, pl.ds((r + step * a) * blk, blk))
            m_ref[1, mid_rows, :] = top
            acc_ref[1, mid_rows, :] = acc
            l_ref[1, mid_rows, :] = l
    for r in range(step):
        for n in range(sub // blk):
            nat_rows = pl.ds(r + n * blk * step, blk, stride=step)
            _, acc, l = merged(0, nat_rows, 1, pl.ds(r * sub + n * blk, blk))
            acc_ref[0, nat_rows, :] = acc / l
    o_ref[...] = acc_ref[0].astype(o_ref.dtype)


def _dilated_attention(proj, cos2, sin2):
    b, s, _ = proj.shape
    q_blk = (2 * SSD_WIDTH + 2 * SSD_BC_WIDTH) // ATT_HEAD_DIM
    k_blk = q_blk + ATT_HEADS
    v_blk = k_blk + ATT_HEADS
    n_pat = len(DILATED_PATTERNS)
    head = lambda base: pl.BlockSpec((None, s, ATT_HEAD_DIM), lambda bi, h: (bi, 0, base + h))
    table = pl.BlockSpec((None, s, ATT_HEAD_DIM), lambda bi, h: (bi, 0, 0))
    return pl.pallas_call(
        functools.partial(_attn_body, seq=s),
        grid=(b, ATT_HEADS),
        in_specs=[head(q_blk), head(k_blk), head(v_blk), table, table],
        out_specs=pl.BlockSpec((None, s, ATT_HEAD_DIM), lambda bi, h: (bi, 0, h)),
        out_shape=jax.ShapeDtypeStruct((b, s, ATT_WIDTH), BF16),
        scratch_shapes=[
            pltpu.VMEM((n_pat, 3, s, ATT_HEAD_DIM), F32),
            pltpu.VMEM((n_pat, s, ATT_HEAD_DIM), F32),
            pltpu.VMEM((n_pat, s, LANES), F32),
            pltpu.VMEM((n_pat, s, LANES), F32),
        ],
        compiler_params=_params("parallel", "arbitrary"),
        name="dilated_attention",
    )(proj, proj, proj, cos2, sin2)


def kernel(x, c, positions, w_mod, b_mod, norm_pre, norm_post, ffn_w_gate, ffn_w_up, ffn_w_down, hyb_w_in, hyb_conv_w, hyb_conv_b, hyb_dt_bias, hyb_a_log, hyb_d_skip, hyb_norm_g, hyb_w_out, sgu_w_in, sgu_b_in, sgu_ln_g, sgu_ln_b, sgu_w_spatial, sgu_b_spatial, sgu_w_out):
    depth = w_mod.shape[0]
    b, s, d = x.shape
    n_sub = norm_pre.shape[1]

    mod = _modulation(c, w_mod, b_mod).reshape(depth, b, n_sub, 3, d)
    g_pre = norm_pre.reshape(depth, n_sub, 1, d)
    g_post = norm_post.reshape(depth, n_sub, 1, d)
    w_gate = ffn_w_gate.astype(BF16)
    w_up = ffn_w_up.astype(BF16)
    w_down = ffn_w_down.astype(BF16)

    for layer in range(depth):
        i = layer // 2
        x = _ffn_sublayer(x, mod, g_pre, g_post, w_gate, w_up, w_down, layer, 0, 0)
        if layer % 2 == 0:
            w_in = hyb_w_in[i]
            dt_lo = SSD_WIDTH + SSD_CONV_CH
            dt_hi = dt_lo + SSD_HEADS
            w_dt = jnp.pad(w_in[:, dt_lo:dt_hi], ((0, 0), (0, LANES - SSD_HEADS))).astype(BF16)
            proj, dt_raw = _in_proj(x, mod, g_pre, (w_in[:, :dt_lo].astype(BF16), w_in[:, dt_hi:].astype(BF16), w_dt),
                                    layer, 1, hyb=True)
            cos2, sin2 = _rope_tables(positions)
            y_a = _ssd_mixer(proj, dt_raw, hyb_conv_w[i], hyb_conv_b[i], hyb_dt_bias[i], hyb_a_log[i],
                             hyb_d_skip[i], hyb_norm_g[i])
            y_b = _dilated_attention(proj, cos2, sin2)
            x = _mixer_out((y_a, y_b), hyb_w_out[i].astype(BF16), x, mod, g_post, layer, 1, sgu=False)
        else:
            zz, = _in_proj(x, mod, g_pre, (sgu_w_in[i].astype(BF16), sgu_b_in[i].reshape(1, -1)), layer, 1,
                           hyb=False)
            x = _mixer_out((zz, sgu_ln_g[i], sgu_ln_b[i], sgu_w_spatial[i], sgu_b_spatial[i]),
                           sgu_w_out[i].astype(BF16), x, mod, g_post, layer, 1, sgu=True)
        x = _ffn_sublayer(x, mod, g_pre, g_post, w_gate, w_up, w_down, layer, 2, 1)
    return x
```

```python
import functools

import jax
import jax.numpy as jnp
from jax import lax
from jax.experimental import pallas as pl
from jax.experimental.pallas import tpu as pltpu

NORM_EPS = 1e-6
LOG2_E = 1.4426950408889634
FFN_RES_WEIGHT = 0.5
MIXER_RES_WEIGHT = 1.0

SSD_HEADS = 32
SSD_HEAD_DIM = 64
SSD_WIDTH = SSD_HEADS * SSD_HEAD_DIM
SSD_GROUPS = 4
SSD_STATE = 128
SSD_CONV = 4
SSD_CHUNK = 128
SSD_TAIL = 16
SSD_BC_WIDTH = SSD_GROUPS * SSD_STATE
SSD_CONV_CH = SSD_WIDTH + 2 * SSD_BC_WIDTH
SSD_HEADS_PER_GROUP = SSD_HEADS // SSD_GROUPS
SSD_GROUP_WIDTH = SSD_HEADS_PER_GROUP * SSD_HEAD_DIM

ATT_HEADS = 16
ATT_HEAD_DIM = 128
ATT_WIDTH = ATT_HEADS * ATT_HEAD_DIM
ATT_BLOCK = 128
DILATED_PATTERNS = ((128, 1), (512, 4), (2048, 16))
ROPE_THETA = 10000.0

SGU_WIDTH = 4096
SGU_GROUPS = 8
SGU_CHUNK = 128
SGU_GROUP_WIDTH = SGU_WIDTH // SGU_GROUPS

LANES = 128
SUBLANES = 8
VMEM_LIMIT_BYTES = 56 * 1024 * 1024

MIX_ROWS = 512
MIX_STEPS = 2
FFN_ROWS = 1024
FFN_TILE = 512
FFN_NORM_CHUNKS = 8
FFN_NORM_ROWS = 16
PROJ_ROWS = 1024
PROJ_TILE = 1024
PROJ_NORM_CHUNKS = 8
MOD_TILE = 1024
ATT_GROUP_FIRST = 16
ATT_GROUP_LATER = 8

BF16 = jnp.bfloat16
F32 = jnp.float32


def _params(*semantics):
    return pltpu.CompilerParams(dimension_semantics=semantics, vmem_limit_bytes=VMEM_LIMIT_BYTES)


def _rms(x):
    return x * lax.rsqrt(jnp.mean(x * x, axis=-1, keepdims=True) + NORM_EPS)


def _silu(x):
    h = 0.5 * x
    return h + h * jnp.tanh(h)


def _dot(a, b):
    return jnp.dot(a, b, preferred_element_type=F32)


def _dot_exact(a, b):
    return jnp.dot(a, b, preferred_element_type=F32, precision=lax.Precision.HIGHEST)


def _dot_nt(a, b):
    return lax.dot_general(a, b, (((1,), (1,)), ((), ())), preferred_element_type=F32)


def _mod_body(c_ref, w_ref, b_ref, o_ref):
    ca = _silu(c_ref[...]).astype(BF16)
    o_ref[...] = _dot(ca, w_ref[...].astype(BF16)) + b_ref[...]


def _modulation(c, w_mod, b_mod):
    depth, d, n = w_mod.shape
    b = c.shape[0]
    return pl.pallas_call(
        _mod_body,
        grid=(depth, n // MOD_TILE),
        in_specs=[
            pl.BlockSpec((b, d), lambda l, j: (0, 0)),
            pl.BlockSpec((None, d, MOD_TILE), lambda l, j: (l, 0, j)),
            pl.BlockSpec((None, 1, MOD_TILE), lambda l, j: (l, 0, j)),
        ],
        out_specs=pl.BlockSpec((None, b, MOD_TILE), lambda l, j: (l, 0, j)),
        out_shape=jax.ShapeDtypeStruct((depth, b, n), F32),
        compiler_params=_params("parallel", "parallel"),
        name="modulation",
    )(c, w_mod, b_mod.reshape(depth, 1, n))


def _ffn_body(xn_ref, xp_ref, modn_ref, modp_ref, gpre_ref, gpost_ref, wg_ref, wu_ref, wd_ref, o_ref,
              h0_ref, h1_ref, acc0_ref, acc1_ref, *, n_tiles, n_f):
    g = pl.program_id(0)
    f = pl.program_id(1)
    chunk = FFN_ROWS // FFN_NORM_CHUNKS
    c0 = jnp.minimum(f, FFN_NORM_CHUNKS - 1) * chunk

    @pl.when((g == 0) & (f == 0))
    def _():
        for ref in (h0_ref, h1_ref, acc0_ref, acc1_ref):
            ref[...] = jnp.zeros_like(ref)

    def pre_norm_chunk(h_ref):
        gain = gpre_ref[...] * (1.0 + modn_ref[1:2, :])
        for q in range(chunk // FFN_NORM_ROWS):
            rows = pl.ds(q * FFN_NORM_ROWS, FFN_NORM_ROWS)
            tile_rows = pl.ds(pl.multiple_of(c0 + q * FFN_NORM_ROWS, FFN_NORM_ROWS), FFN_NORM_ROWS)
            h_ref[tile_rows, :] = (_rms(xn_ref[rows, :]) * gain + modn_ref[0:1, :]).astype(BF16)

    def post_norm_chunk(acc_ref):
        gain = (FFN_RES_WEIGHT * (1.0 + modp_ref[2:3, :])) * gpost_ref[...]
        for q in range(chunk // FFN_NORM_ROWS):
            rows = pl.ds(q * FFN_NORM_ROWS, FFN_NORM_ROWS)
            tile_rows = pl.ds(pl.multiple_of(c0 + q * FFN_NORM_ROWS, FFN_NORM_ROWS), FFN_NORM_ROWS)
            o_ref[rows, :] = xp_ref[rows, :] + _rms(acc_ref[tile_rows, :]) * gain

    def swiglu_step(h_ref, acc_ref):
        h = h_ref[...]
        a = (_silu(_dot(h, wg_ref[...])) * _dot(h, wu_ref[...])).astype(BF16)
        acc_ref[...] = jnp.where(f > 0, acc_ref[...], 0.0) + _dot(a, wd_ref[...])

    has_matmul = (g >= 1) & (g <= n_tiles)
    for parity, (h_new, acc_old, h_mid, acc_mid) in enumerate(
            [(h0_ref, acc0_ref, h1_ref, acc1_ref), (h1_ref, acc1_ref, h0_ref, acc0_ref)]):
        @pl.when((lax.rem(g, 2) == parity) & has_matmul)
        def _():
            post_norm_chunk(acc_old)
            swiglu_step(h_mid, acc_mid)
            pre_norm_chunk(h_new)

        @pl.when((lax.rem(g, 2) == parity) & (g == 0) & (f < FFN_NORM_CHUNKS))
        def _():
            pre_norm_chunk(h_new)

        @pl.when((lax.rem(g, 2) == parity) & (g == n_tiles + 1) & (f < FFN_NORM_CHUNKS))
        def _():
            post_norm_chunk(acc_old)


def _ffn_sublayer(x, mod, g_pre, g_post, w_gate, w_up, w_down, layer, sub, idx):
    b, s, d = x.shape
    f_dim = w_gate.shape[-1]
    n_f = f_dim // FFN_TILE
    per_batch = s // FFN_ROWS
    n_tiles = b * per_batch
    chunks = FFN_NORM_CHUNKS
    chunk = FFN_ROWS // chunks
    assert n_f >= chunks
    new_tile = lambda g: jnp.minimum(g, n_tiles - 1)
    old_tile = lambda g: jnp.clip(g - 2, 0, n_tiles - 1)
    new_chunk = lambda g, f: (new_tile(g) * chunks + jnp.minimum(f, chunks - 1), 0, 0)
    old_chunk = lambda g, f: (old_tile(g) * chunks + jnp.minimum(f, chunks - 1), 0, 0)
    out_chunk = lambda g, f: (jnp.where(g < 2, 0, old_chunk(g, f)[0]), 0, 0)
    w_step = lambda g, f: jnp.where(g == 0, 0, jnp.where(g == n_tiles + 1, n_f - 1, f))
    x_chunks = x.reshape(n_tiles * chunks, chunk, d)
    out = pl.pallas_call(
        functools.partial(_ffn_body, n_tiles=n_tiles, n_f=n_f),
        grid=(n_tiles + 2, n_f),
        in_specs=[
            pl.BlockSpec((None, chunk, d), new_chunk),
            pl.BlockSpec((None, chunk, d), old_chunk),
            pl.BlockSpec((None, None, None, 3, d), lambda g, f: (layer, new_tile(g) // per_batch, sub, 0, 0)),
            pl.BlockSpec((None, None, None, 3, d), lambda g, f: (layer, old_tile(g) // per_batch, sub, 0, 0)),
            pl.BlockSpec((None, None, 1, d), lambda g, f: (layer, sub, 0, 0)),
            pl.BlockSpec((None, None, 1, d), lambda g, f: (layer, sub, 0, 0)),
            pl.BlockSpec((None, None, d, FFN_TILE), lambda g, f: (layer, idx, 0, w_step(g, f))),
            pl.BlockSpec((None, None, d, FFN_TILE), lambda g, f: (layer, idx, 0, w_step(g, f))),
            pl.BlockSpec((None, None, FFN_TILE, d), lambda g, f: (layer, idx, w_step(g, f), 0)),
        ],
        out_specs=pl.BlockSpec((None, chunk, d), out_chunk),
        out_shape=jax.ShapeDtypeStruct(x_chunks.shape, F32),
        scratch_shapes=[pltpu.VMEM((FFN_ROWS, d), BF16), pltpu.VMEM((FFN_ROWS, d), BF16),
                        pltpu.VMEM((FFN_ROWS, d), F32), pltpu.VMEM((FFN_ROWS, d), F32)],
        compiler_params=_params("arbitrary", "arbitrary"),
        name="ffn_sublayer",
    )(x_chunks, x_chunks, mod, mod, g_pre, g_post, w_gate, w_up, w_down)
    return out.reshape(b, s, d)


def _gelu_tanh(x):
    return 0.5 * x * (1.0 + jnp.tanh(0.7978845608028654 * (x + 0.044715 * (x * x * x))))


def _in_proj_body(*refs, hyb, n_a):
    if hyb:
        xn_ref, modn_ref, gpre_ref, wa_ref, wb_ref, wdt_ref, o_ref, dt_ref, h0_ref, h1_ref = refs
    else:
        xn_ref, modn_ref, gpre_ref, wa_ref, bias_ref, o_ref, h0_ref, h1_ref = refs
    g = pl.program_id(0)
    j = pl.program_id(1)
    chunk = PROJ_ROWS // PROJ_NORM_CHUNKS
    c0 = jnp.minimum(j, PROJ_NORM_CHUNKS - 1) * chunk

    @pl.when((g == 0) & (j == 0))
    def _():
        h0_ref[...] = jnp.zeros_like(h0_ref)
        h1_ref[...] = jnp.zeros_like(h1_ref)

    def pre_norm_chunk(h_ref):
        gain = gpre_ref[...] * (1.0 + modn_ref[1:2, :])
        for q in range(chunk // FFN_NORM_ROWS):
            rows = pl.ds(q * FFN_NORM_ROWS, FFN_NORM_ROWS)
            tile_rows = pl.ds(pl.multiple_of(c0 + q * FFN_NORM_ROWS, FFN_NORM_ROWS), FFN_NORM_ROWS)
            h_ref[tile_rows, :] = (_rms(xn_ref[rows, :]) * gain + modn_ref[0:1, :]).astype(BF16)

    def project(h_ref, w_ref):
        y = _dot(h_ref[...], w_ref[...])
        if not hyb:
            y = _gelu_tanh(y + bias_ref[...])
        o_ref[...] = y.astype(o_ref.dtype)

    has_matmul = g >= 1
    for parity, (h_new, h_mid) in enumerate([(h0_ref, h1_ref), (h1_ref, h0_ref)]):
        mine = lax.rem(g, 2) == parity
        if hyb:
            @pl.when(mine & has_matmul & (j == 0))
            def _():
                dt_ref[...] = _dot(h_mid[...], wdt_ref[...])

            @pl.when(mine & has_matmul & (j < n_a))
            def _():
                project(h_mid, wa_ref)
                pre_norm_chunk(h_new)

            @pl.when(mine & has_matmul & (j >= n_a))
            def _():
                project(h_mid, wb_ref)
                pre_norm_chunk(h_new)
        else:
            @pl.when(mine & has_matmul)
            def _():
                project(h_mid, wa_ref)
                pre_norm_chunk(h_new)

        @pl.when(mine & jnp.logical_not(has_matmul) & (j < PROJ_NORM_CHUNKS))
        def _():
            pre_norm_chunk(h_new)


def _in_proj(x, mod, g_pre, weights, layer, sub, hyb):
    b, s, d = x.shape
    per_batch = s // PROJ_ROWS
    n_tiles = b * per_batch
    chunks = PROJ_NORM_CHUNKS
    chunk = PROJ_ROWS // chunks
    w_a = weights[0]
    n_a = w_a.shape[1] // PROJ_TILE
    n_steps = n_a + (weights[1].shape[1] // PROJ_TILE if hyb else 0)
    assert n_steps >= chunks
    new_tile = lambda g: jnp.minimum(g, n_tiles - 1)
    mid_tile = lambda g: jnp.maximum(g - 1, 0)
    col = lambda g, j: jnp.where(g == 0, 0, j)
    in_specs = [
        pl.BlockSpec((None, chunk, d), lambda g, j: (new_tile(g) * chunks + jnp.minimum(j, chunks - 1), 0, 0)),
        pl.BlockSpec((None, None, None, 3, d), lambda g, j: (layer, new_tile(g) // per_batch, sub, 0, 0)),
        pl.BlockSpec((None, None, 1, d), lambda g, j: (layer, sub, 0, 0)),
        pl.BlockSpec((d, PROJ_TILE), lambda g, j: (0, jnp.minimum(col(g, j), n_a - 1))),
    ]
    out_specs = [pl.BlockSpec((None, PROJ_ROWS, PROJ_TILE), lambda g, j: (mid_tile(g), 0, col(g, j)))]
    out_shape = [jax.ShapeDtypeStruct((n_tiles, PROJ_ROWS, n_steps * PROJ_TILE), BF16)]
    if hyb:
        in_specs += [
            pl.BlockSpec((d, PROJ_TILE), lambda g, j: (0, jnp.maximum(col(g, j) - n_a, 0))),
            pl.BlockSpec((d, LANES), lambda g, j: (0, 0)),
        ]
        out_specs.append(pl.BlockSpec((None, PROJ_ROWS, LANES), lambda g, j: (mid_tile(g), 0, 0)))
        out_shape.append(jax.ShapeDtypeStruct((n_tiles, PROJ_ROWS, LANES), F32))
    else:
        in_specs.append(pl.BlockSpec((1, PROJ_TILE), lambda g, j: (0, col(g, j))))
    outs = pl.pallas_call(
        functools.partial(_in_proj_body, hyb=hyb, n_a=n_a),
        grid=(n_tiles + 1, n_steps),
        in_specs=in_specs,
        out_specs=out_specs,
        out_shape=out_shape,
        scratch_shapes=[pltpu.VMEM((PROJ_ROWS, d), BF16)] * 2,
        compiler_params=_params("arbitrary", "arbitrary"),
        name="in_proj",
    )(x.reshape(n_tiles * chunks, chunk, d), mod, g_pre, *weights)
    return [o.reshape(b, s, o.shape[-1]) for o in outs]


def _sgu_gate_chunk(u, v, lng_ref, lnb_ref, ws_ref, bs_ref):
    L = SGU_CHUNK
    v = v.astype(F32)
    mu = jnp.mean(v, axis=-1, keepdims=True)
    vc = v - mu
    var = jnp.mean(vc * vc, axis=-1, keepdims=True)
    vn = (vc * lax.rsqrt(var + NORM_EPS) * lng_ref[...] + lnb_ref[...]).astype(BF16)
    row = lax.broadcasted_iota(jnp.int32, (L, L), 0)
    col = lax.broadcasted_iota(jnp.int32, (L, L), 1)
    causal = row >= col
    parts = []
    for g in range(SGU_GROUPS):
        gs = slice(g * SGU_GROUP_WIDTH, (g + 1) * SGU_GROUP_WIDTH)
        w = jnp.where(causal, ws_ref[g], 0.0).astype(BF16)
        mixed = _dot(w, vn[:, gs]) + bs_ref[:, g:g + 1]
        parts.append((u[:, gs].astype(F32) * mixed).astype(BF16))
    return jnp.concatenate(parts, axis=1)


def _mixer_out_body(*refs, sgu, n_tiles):
    if sgu:
        (u_ref, v_ref, lng_ref, lnb_ref, ws_ref, bs_ref, w_ref, xp_ref, modp_ref, gpost_ref, o_ref,
         lhs0_ref, lhs1_ref, acc0_ref, acc1_ref) = refs
    else:
        a_ref, b_ref, w_ref, xp_ref, modp_ref, gpost_ref, o_ref, acc0_ref, acc1_ref = refs
        lhs0_ref = lhs1_ref = None
    g = pl.program_id(0)
    k = pl.program_id(1)
    chunk = MIX_ROWS // MIX_STEPS
    half = w_ref.shape[0]

    @pl.when((g == 0) & (k == 0))
    def _():
        for ref in (lhs0_ref, lhs1_ref, acc0_ref, acc1_ref):
            if ref is not None:
                ref[...] = jnp.zeros_like(ref)

    def post_norm_chunk(acc_ref):
        gain = (MIXER_RES_WEIGHT * (1.0 + modp_ref[2:3, :])) * gpost_ref[...]
        for q in range(chunk // FFN_NORM_ROWS):
            rows = pl.ds(q * FFN_NORM_ROWS, FFN_NORM_ROWS)
            tile_rows = pl.ds(pl.multiple_of(k * chunk + q * FFN_NORM_ROWS, FFN_NORM_ROWS), FFN_NORM_ROWS)
            o_ref[rows, :] = xp_ref[rows, :] + _rms(acc_ref[tile_rows, :]) * gain

    def gate_chunk(lhs_ref):
        for j in range(chunk // SGU_CHUNK):
            rows = slice(j * SGU_CHUNK, (j + 1) * SGU_CHUNK)
            gated = _sgu_gate_chunk(u_ref[rows, :], v_ref[rows, :], lng_ref, lnb_ref, ws_ref, bs_ref)
            tile_rows = pl.ds(pl.multiple_of(k * chunk + j * SGU_CHUNK, SGU_CHUNK), SGU_CHUNK)
            for part in range(MIX_STEPS):
                lhs_ref[part, tile_rows, :] = gated[:, part * half:(part + 1) * half]

    def matmul_step(lhs_ref, acc_ref):
        lhs = lhs_ref[k] if sgu else jnp.where(k == 0, a_ref[...], b_ref[...])
        acc_ref[...] = jnp.where(k > 0, acc_ref[...], 0.0) + _dot(lhs, w_ref[...])

    has_matmul = (g >= 1) & (g <= n_tiles)
    for parity, (lhs_new, acc_old, lhs_mid, acc_mid) in enumerate(
            [(lhs0_ref, acc0_ref, lhs1_ref, acc1_ref), (lhs1_ref, acc1_ref, lhs0_ref, acc0_ref)]):
        @pl.when((lax.rem(g, 2) == parity) & has_matmul)
        def _():
            post_norm_chunk(acc_old)
            matmul_step(lhs_mid, acc_mid)
            if sgu:
                gate_chunk(lhs_new)

        if sgu:
            @pl.when((lax.rem(g, 2) == parity) & (g == 0))
            def _():
                gate_chunk(lhs_new)

        @pl.when((lax.rem(g, 2) == parity) & (g == n_tiles + 1))
        def _():
            post_norm_chunk(acc_old)


def _mixer_out(operands, w, x, mod, g_post, layer, sub, sgu):
    b, s, d = x.shape
    half = w.shape[0] // MIX_STEPS
    per_batch = s // MIX_ROWS
    n_tiles = b * per_batch
    steps = MIX_STEPS
    chunk = MIX_ROWS // steps
    new_tile = lambda g: jnp.minimum(g, n_tiles - 1)
    mid_tile = lambda g: jnp.clip(g - 1, 0, n_tiles - 1)
    old_tile = lambda g: jnp.clip(g - 2, 0, n_tiles - 1)
    new_chunk = lambda g, k: new_tile(g) * steps + k
    old_chunk = lambda g, k: old_tile(g) * steps + k
    out_chunk = lambda g, k: (jnp.where(g < 2, 0, old_chunk(g, k)), 0, 0)
    w_step = lambda g, k: jnp.where(g == 0, 0, jnp.where(g == n_tiles + 1, steps - 1, k))
    x_chunks = x.reshape(n_tiles * steps, chunk, d)
    common_specs = [
        pl.BlockSpec((half, d), lambda g, k: (w_step(g, k), 0)),
        pl.BlockSpec((None, chunk, d), lambda g, k: (old_chunk(g, k), 0, 0)),
        pl.BlockSpec((None, None, None, 3, d), lambda g, k: (layer, old_tile(g) // per_batch, sub, 0, 0)),
        pl.BlockSpec((None, None, 1, d), lambda g, k: (layer, sub, 0, 0)),
    ]
    acc = [pltpu.VMEM((MIX_ROWS, d), F32)] * 2
    if sgu:
        zz, ln_g, ln_b, w_spatial, b_spatial = operands
        zz_chunks = zz.reshape(n_tiles * steps, chunk, 2 * SGU_WIDTH)
        vec = pl.BlockSpec((1, SGU_WIDTH), lambda g, k: (0, 0))
        lhs_specs = [
            pl.BlockSpec((None, chunk, SGU_WIDTH), lambda g, k: (new_chunk(g, k), 0, 0)),
            pl.BlockSpec((None, chunk, SGU_WIDTH), lambda g, k: (new_chunk(g, k), 0, 1)),
            vec, vec,
            pl.BlockSpec((SGU_GROUPS, SGU_CHUNK, SGU_CHUNK), lambda g, k: (0, 0, 0)),
            pl.BlockSpec((SGU_CHUNK, SGU_GROUPS), lambda g, k: (0, 0)),
        ]
        lhs_args = [zz_chunks, zz_chunks, ln_g.reshape(1, SGU_WIDTH), ln_b.reshape(1, SGU_WIDTH), w_spatial,
                    b_spatial.T]
        scratch = [pltpu.VMEM((steps, MIX_ROWS, half), BF16)] * 2 + acc
    else:
        y_a, y_b = operands
        tile = pl.BlockSpec((None, MIX_ROWS, half), lambda g, k: (mid_tile(g), 0, 0))
        lhs_specs = [tile, tile]
        lhs_args = [y_a.reshape(n_tiles, MIX_ROWS, half), y_b.reshape(n_tiles, MIX_ROWS, half)]
        scratch = acc
    out = pl.pallas_call(
        functools.partial(_mixer_out_body, sgu=sgu, n_tiles=n_tiles),
        grid=(n_tiles + 2, steps),
        in_specs=lhs_specs + common_specs,
        out_specs=pl.BlockSpec((None, chunk, d), out_chunk),
        out_shape=jax.ShapeDtypeStruct(x_chunks.shape, F32),
        scratch_shapes=scratch,
        compiler_params=_params("arbitrary", "arbitrary"),
        name="mixer_out",
    )(*lhs_args, w, x_chunks, mod, g_post)
    return out.reshape(b, s, d)


def _rope_body(pos_ref, freq_ref, cos_ref, sin_ref):
    ang = pos_ref[...].astype(F32) * freq_ref[...]
    lane = lax.broadcasted_iota(jnp.int32, ang.shape, 1)
    cos_ref[...] = jnp.cos(ang)
    sin_ref[...] = jnp.where(lane < ATT_HEAD_DIM // 2, -1.0, 1.0) * jnp.sin(ang)


def _rope_tables(positions):
    b, s = positions.shape
    half = ATT_HEAD_DIM // 2
    inv_freq = ROPE_THETA ** (-jnp.arange(half, dtype=F32) / half)
    freq = jnp.concatenate([inv_freq, inv_freq]).reshape(1, ATT_HEAD_DIM)
    tile = 512
    spec = pl.BlockSpec((None, tile, ATT_HEAD_DIM), lambda bi, i: (bi, i, 0))
    return pl.pallas_call(
        _rope_body,
        grid=(b, s // tile),
        in_specs=[
            pl.BlockSpec((None, tile, 1), lambda bi, i: (bi, i, 0)),
            pl.BlockSpec((1, ATT_HEAD_DIM), lambda bi, i: (0, 0)),
        ],
        out_specs=[spec, spec],
        out_shape=[jax.ShapeDtypeStruct((b, s, ATT_HEAD_DIM), F32)] * 2,
        compiler_params=_params("parallel", "parallel"),
        name="rope_tables",
    )(positions.reshape(b, s, 1), freq)


def _ssd_body(xs_ref, bm_ref, cm_ref, z_ref, dt_ref, cw_ref, cb_ref, dtb_ref, alog_ref, dskip_ref, ng_ref,
              o_ref, tail_ref, state_ref):
    L = SSD_CHUNK
    T = SSD_TAIL

    @pl.when(pl.program_id(1) == 0)
    def _():
        tail_ref[...] = jnp.zeros_like(tail_ref)
        state_ref[...] = jnp.zeros_like(state_ref)

    srow = lax.broadcasted_iota(jnp.int32, ((SSD_CONV - 1) * L, L + T), 0)
    scol = lax.broadcasted_iota(jnp.int32, ((SSD_CONV - 1) * L, L + T), 1)
    lag = (srow >> (L.bit_length() - 1)) + 1
    t_in = srow & (L - 1)
    shift = (scol == jnp.where(t_in >= lag, t_in - lag, t_in - lag + (L + T))).astype(BF16)

    def conv_silu(raw_ref, lo, hi):
        raw = raw_ref[...]
        lagged = _dot(shift, jnp.concatenate([raw, tail_ref[:, lo:hi]], axis=0))
        acc = cb_ref[:, lo:hi] + raw.astype(F32) * cw_ref[SSD_CONV - 1:SSD_CONV, lo:hi]
        for k in range(1, SSD_CONV):
            acc = acc + lagged[(k - 1) * L:k * L] * cw_ref[SSD_CONV - 1 - k:SSD_CONV - k, lo:hi]
        tail_ref[:, lo:hi] = raw[L - T:L]
        return _silu(acc)

    xs = conv_silu(xs_ref, 0, SSD_WIDTH)
    bm = conv_silu(bm_ref, SSD_WIDTH, SSD_WIDTH + SSD_BC_WIDTH)
    cm = conv_silu(cm_ref, SSD_WIDTH + SSD_BC_WIDTH, SSD_CONV_CH)

    dt_in = dt_ref[...] + dtb_ref[...]
    dt = jnp.maximum(dt_in, 0.0) + jnp.log1p(jnp.exp(-jnp.abs(dt_in)))
    adt = dt * (-jnp.exp(alog_ref[...]))
    row = lax.broadcasted_iota(jnp.int32, (L, L), 0)
    col = lax.broadcasted_iota(jnp.int32, (L, L), 1)
    causal = row >= col
    acs = _dot_exact(causal.astype(F32), adt)
    acs_t = acs.T
    dt_t = dt.T
    acs_last = acs[L - 1:L, :]

    hrow = lax.broadcasted_iota(jnp.int32, (LANES, SSD_WIDTH), 0)
    hcol = lax.broadcasted_iota(jnp.int32, (LANES, SSD_WIDTH), 1)
    expand = (hrow == (hcol >> (SSD_HEAD_DIM.bit_length() - 1))).astype(BF16)
    stacked = jnp.concatenate([jnp.exp(acs), jnp.exp(acs_last - acs) * dt], axis=0)
    high = stacked.astype(BF16)
    rest = (stacked - high.astype(F32)).astype(BF16)
    wide = _dot(high, expand) + _dot(rest, expand)
    decay_in_w, dt_decay_out_w = wide[0:L], wide[L:2 * L]
    chunk_decay_w = decay_in_w[L - 1:L, :]

    lane = lax.broadcasted_iota(jnp.int32, (L, LANES), 1)
    first_head = lane < SSD_HEAD_DIM

    y_parts = []
    for g in range(SSD_GROUPS):
        gs = slice(g * SSD_GROUP_WIDTH, (g + 1) * SSD_GROUP_WIDTH)
        bg = bm[:, g * SSD_STATE:(g + 1) * SSD_STATE]
        cg = cm[:, g * SSD_STATE:(g + 1) * SSD_STATE].astype(BF16)
        cb = _dot_nt(cg, bg.astype(BF16))
        state = state_ref[g]
        y_off = _dot(cg, state.astype(BF16)) * decay_in_w[:, gs]
        y_diag = []
        for j in range(SSD_HEADS_PER_GROUP // 2):
            h0 = g * SSD_HEADS_PER_GROUP + 2 * j
            ms = []
            for h in (h0, h0 + 1):
                seg = acs[:, h:h + 1] - acs_t[h:h + 1, :]
                ms.append(cb * jnp.exp(jnp.where(causal, seg, -jnp.inf)) * dt_t[h:h + 1, :])
            lhs = jnp.concatenate(ms, axis=1).astype(BF16)
            xp = xs[:, h0 * SSD_HEAD_DIM:(h0 + 2) * SSD_HEAD_DIM]
            rhs = jnp.concatenate([jnp.where(first_head, xp, 0.0), jnp.where(first_head, 0.0, xp)], axis=0)
            y_diag.append(_dot(lhs, rhs.astype(BF16)))
        y_parts.append(jnp.concatenate(y_diag, axis=1) + y_off)
        contrib = _dot(bg.T.astype(BF16), (xs[:, gs] * dt_decay_out_w[:, gs]).astype(BF16))
        state_ref[g] = state * chunk_decay_w[:, gs] + contrib

    y = jnp.concatenate(y_parts, axis=1) + xs * dskip_ref[...]
    y = y * _silu(z_ref[...].astype(F32))
    o_ref[...] = (_rms(y) * ng_ref[...]).astype(o_ref.dtype)


def _ssd_mixer(proj, dt_raw, conv_w, conv_b, dt_bias, a_log, d_skip, norm_g):
    b, s, _ = proj.shape
    assert proj.dtype == BF16
    L = SSD_CHUNK
    z_blk = 0
    xs_blk = SSD_WIDTH // SSD_WIDTH
    bm_blk = (2 * SSD_WIDTH) // SSD_BC_WIDTH
    cm_blk = bm_blk + 1
    pad = LANES - SSD_HEADS
    small = lambda a: pl.BlockSpec(a.shape, lambda bi, c: (0, 0))
    dt_bias_p = jnp.pad(dt_bias, (0, pad)).reshape(1, LANES)
    a_log_p = jnp.pad(a_log, (0, pad)).reshape(1, LANES)
    d_skip_w = jnp.repeat(d_skip, SSD_HEAD_DIM).reshape(1, SSD_WIDTH)
    conv_b2 = conv_b.reshape(1, SSD_CONV_CH)
    norm_g2 = norm_g.reshape(1, SSD_WIDTH)
    return pl.pallas_call(
        _ssd_body,
        grid=(b, s // L),
        in_specs=[
            pl.BlockSpec((None, L, SSD_WIDTH), lambda bi, c: (bi, c, xs_blk)),
            pl.BlockSpec((None, L, SSD_BC_WIDTH), lambda bi, c: (bi, c, bm_blk)),
            pl.BlockSpec((None, L, SSD_BC_WIDTH), lambda bi, c: (bi, c, cm_blk)),
            pl.BlockSpec((None, L, SSD_WIDTH), lambda bi, c: (bi, c, z_blk)),
            pl.BlockSpec((None, L, LANES), lambda bi, c: (bi, c, 0)),
            small(conv_w), small(conv_b2), small(dt_bias_p), small(a_log_p), small(d_skip_w), small(norm_g2),
        ],
        out_specs=pl.BlockSpec((None, L, SSD_WIDTH), lambda bi, c: (bi, c, 0)),
        out_shape=jax.ShapeDtypeStruct((b, s, SSD_WIDTH), BF16),
        scratch_shapes=[
            pltpu.VMEM((SSD_TAIL, SSD_CONV_CH), BF16),
            pltpu.VMEM((SSD_GROUPS, SSD_STATE, SSD_GROUP_WIDTH), F32),
        ],
        compiler_params=_params("parallel", "arbitrary"),
        name="ssd_mixer",
    )(proj, proj, proj, proj, dt_raw, conv_w, conv_b2, dt_bias_p, a_log_p, d_skip_w, norm_g2)


def _attn_body(q_ref, k_ref, v_ref, cos_ref, sin_ref, o_ref, qkv_ref, acc_ref, m_ref, l_ref, *, seq):
    blk = ATT_BLOCK
    half = ATT_HEAD_DIM // 2
    dils = [d for _, d in DILATED_PATTERNS]
    step = dils[1]
    assert dils == [1, step, step * step] and all(w // d == blk for w, d in DILATED_PATTERNS)
    sub = seq // step
    assert seq // dils[2] == blk

    cos = cos_ref[...]
    sin = sin_ref[...]
    q = q_ref[...].astype(F32)
    k = k_ref[...].astype(F32)
    qkv_ref[0, 0] = (q * cos + pltpu.roll(q, half, 1) * sin) * (ATT_HEAD_DIM ** -0.5 * LOG2_E)
    qkv_ref[0, 1] = k * cos + pltpu.roll(k, half, 1) * sin
    qkv_ref[0, 2] = v_ref[...].astype(F32)
    for t in range(3):
        for r in range(step):
            qkv_ref[1, t, pl.ds(r * sub, sub), :] = qkv_ref[0, t, pl.ds(r, sub, stride=step), :]
    for t in range(3):
        for r in range(step):
            for a in range(step):
                qkv_ref[2, t, pl.ds((r + step * a) * blk, blk), :] = qkv_ref[1, t, pl.ds(r * sub + a, blk, stride=step), :]

    row = lax.broadcasted_iota(jnp.int32, (blk, blk), 0)
    col = lax.broadcasted_iota(jnp.int32, (blk, blk), 1)
    cur_ok = col <= row
    prev_ok = col >= row

    def attend(p, starts, with_prev):
        def load(t, st):
            return qkv_ref[p, t, pl.ds(st, blk), :].astype(BF16)

        def window(t, st):
            return jnp.concatenate([load(t, st - blk), load(t, st)], axis=0) if with_prev else load(t, st)

        qb = jnp.stack([load(0, st) for st in starts])
        kk = jnp.stack([window(1, st) for st in starts])
        keys = kk.shape[1]
        vv = jnp.stack([jnp.concatenate([window(2, st), jnp.ones((keys, LANES), BF16)], axis=1) for st in starts])
        ok = jnp.concatenate([prev_ok, cur_ok], axis=1) if with_prev else cur_ok
        s = lax.dot_general(qb, kk, (((2,), (2,)), ((0,), (0,))), preferred_element_type=F32)
        s = jnp.where(ok[None], s, -jnp.inf)
        m = jnp.max(s, axis=2, keepdims=True)
        e = jnp.exp2(s - m)
        acc = lax.dot_general(e.astype(BF16), vv, (((2,), (1,)), ((0,), (0,))), preferred_element_type=F32)
        for i, st in enumerate(starts):
            rows = pl.ds(st, blk)
            acc_ref[p, rows, :] = acc[i, :, 0:ATT_HEAD_DIM]
            l_ref[p, rows, :] = acc[i, :, ATT_HEAD_DIM:]
            m_ref[p, rows, :] = jnp.broadcast_to(m[i], (blk, LANES))

    def groups(starts, size):
        return [starts[i:i + size] for i in range(0, len(starts), size)]

    for p, dil in enumerate(dils):
        class_rows = seq // dil
        firsts = [r * class_rows for r in range(dil)]
        laters = [r * class_rows + n * blk for r in range(dil) for n in range(1, class_rows // blk)]
        for g in groups(firsts, ATT_GROUP_FIRST):
            attend(p, g, False)
        for g in groups(laters, ATT_GROUP_LATER):
            attend(p, g, True)

    def merged(dst, dst_rows, src, src_rows):
        m_a, m_b = m_ref[dst, dst_rows, :], m_ref[src, src_rows, :]
        top = jnp.maximum(m_a, m_b)
        w_a, w_b = jnp.exp2(m_a - top), jnp.exp2(m_b - top)
        acc = w_a * acc_ref[dst, dst_rows, :] + w_b * acc_ref[src, src_rows, :]
        return top, acc, w_a * l_ref[dst, dst_rows, :] + w_b * l_ref[src, src_rows, :]

    for r in range(step):
        for a in range(step):
            mid_rows = pl.ds(r * sub + a, blk, stride=step)
            top, acc, l = merged(1, mid_rows, 2, pl.ds((r + step * a) * blk, blk))
            m_ref[1, mid_rows, :] = top
            acc_ref[1, mid_rows, :] = acc
            l_ref[1, mid_rows, :] = l
    for r in range(step):
        for n in range(sub // blk):
            nat_rows = pl.ds(r + n * blk * step, blk, stride=step)
            _, acc, l = merged(0, nat_rows, 1, pl.ds(r * sub + n * blk, blk))
            acc_ref[0, nat_rows, :] = acc / l
    o_ref[...] = acc_ref[0].astype(o_ref.dtype)


def _dilated_attention(proj, cos2, sin2):
    b, s, _ = proj.shape
    q_blk = (2 * SSD_WIDTH + 2 * SSD_BC_WIDTH) // ATT_HEAD_DIM
    k_blk = q_blk + ATT_HEADS
    v_blk = k_blk + ATT_HEADS
    n_pat = len(DILATED_PATTERNS)
    head = lambda base: pl.BlockSpec((None, s, ATT_HEAD_DIM), lambda bi, h: (bi, 0, base + h))
    table = pl.BlockSpec((None, s, ATT_HEAD_DIM), lambda bi, h: (bi, 0, 0))
    return pl.pallas_call(
        functools.partial(_attn_body, seq=s),
        grid=(b, ATT_HEADS),
        in_specs=[head(q_blk), head(k_blk), head(v_blk), table, table],
        out_specs=pl.BlockSpec((None, s, ATT_HEAD_DIM), lambda bi, h: (bi, 0, h)),
        out_shape=jax.ShapeDtypeStruct((b, s, ATT_WIDTH), BF16),
        scratch_shapes=[
            pltpu.VMEM((n_pat, 3, s, ATT_HEAD_DIM), F32),
            pltpu.VMEM((n_pat, s, ATT_HEAD_DIM), F32),
            pltpu.VMEM((n_pat, s, LANES), F32),
            pltpu.VMEM((n_pat, s, LANES), F32),
        ],
        compiler_params=_params("parallel", "arbitrary"),
        name="dilated_attention",
    )(proj, proj, proj, cos2, sin2)


def kernel(x, c, positions, w_mod, b_mod, norm_pre, norm_post, ffn_w_gate, ffn_w_up, ffn_w_down, hyb_w_in, hyb_conv_w, hyb_conv_b, hyb_dt_bias, hyb_a_log, hyb_d_skip, hyb_norm_g, hyb_w_out, sgu_w_in, sgu_b_in, sgu_ln_g, sgu_ln_b, sgu_w_spatial, sgu_b_spatial, sgu_w_out):
    depth = w_mod.shape[0]
    b, s, d = x.shape
    n_sub = norm_pre.shape[1]

    mod = _modulation(c, w_mod, b_mod).reshape(depth, b, n_sub, 3, d)
    g_pre = norm_pre.reshape(depth, n_sub, 1, d)
    g_post = norm_post.reshape(depth, n_sub, 1, d)
    w_gate = ffn_w_gate.astype(BF16)
    w_up = ffn_w_up.astype(BF16)
    w_down = ffn_w_down.astype(BF16)

    for layer in range(depth):
        i = layer // 2
        x = _ffn_sublayer(x, mod, g_pre, g_post, w_gate, w_up, w_down, layer, 0, 0)
        if layer % 2 == 0:
            w_in = hyb_w_in[i]
            dt_lo = SSD_WIDTH + SSD_CONV_CH
            dt_hi = dt_lo + SSD_HEADS
            w_dt = jnp.pad(w_in[:, dt_lo:dt_hi], ((0, 0), (0, LANES - SSD_HEADS))).astype(BF16)
            proj, dt_raw = _in_proj(x, mod, g_pre, (w_in[:, :dt_lo].astype(BF16), w_in[:, dt_hi:].astype(BF16), w_dt),
                                    layer, 1, hyb=True)
            cos2, sin2 = _rope_tables(positions)
            y_a = _ssd_mixer(proj, dt_raw, hyb_conv_w[i], hyb_conv_b[i], hyb_dt_bias[i], hyb_a_log[i],
                             hyb_d_skip[i], hyb_norm_g[i])
            y_b = _dilated_attention(proj, cos2, sin2)
            x = _mixer_out((y_a, y_b), hyb_w_out[i].astype(BF16), x, mod, g_post, layer, 1, sgu=False)
        else:
            zz, = _in_proj(x, mod, g_pre, (sgu_w_in[i].astype(BF16), sgu_b_in[i].reshape(1, -1)), layer, 1,
                           hyb=False)
            x = _mixer_out((zz, sgu_ln_g[i], sgu_ln_b[i], sgu_w_spatial[i], sgu_b_spatial[i]),
                           sgu_w_out[i].astype(BF16), x, mod, g_post, layer, 1, sgu=True)
        x = _ffn_sublayer(x, mod, g_pre, g_post, w_gate, w_up, w_down, layer, 2, 1)
    return x
```

```python
import functools

import jax
import jax.numpy as jnp
from jax import lax
from jax.experimental import pallas as pl
from jax.experimental.pallas import tpu as pltpu

NORM_EPS = 1e-6
LOG2_E = 1.4426950408889634
FFN_RES_WEIGHT = 0.5
MIXER_RES_WEIGHT = 1.0

SSD_HEADS = 32
SSD_HEAD_DIM = 64
SSD_WIDTH = SSD_HEADS * SSD_HEAD_DIM
SSD_GROUPS = 4
SSD_STATE = 128
SSD_CONV = 4
SSD_CHUNK = 128
SSD_TAIL = 16
SSD_BC_WIDTH = SSD_GROUPS * SSD_STATE
SSD_CONV_CH = SSD_WIDTH + 2 * SSD_BC_WIDTH
SSD_HEADS_PER_GROUP = SSD_HEADS // SSD_GROUPS
SSD_GROUP_WIDTH = SSD_HEADS_PER_GROUP * SSD_HEAD_DIM

ATT_HEADS = 16
ATT_HEAD_DIM = 128
ATT_WIDTH = ATT_HEADS * ATT_HEAD_DIM
ATT_BLOCK = 128
DILATED_PATTERNS = ((128, 1), (512, 4), (2048, 16))
ROPE_THETA = 10000.0

SGU_WIDTH = 4096
SGU_GROUPS = 8
SGU_CHUNK = 128
SGU_GROUP_WIDTH = SGU_WIDTH // SGU_GROUPS

LANES = 128
SUBLANES = 8
VMEM_LIMIT_BYTES = 56 * 1024 * 1024

MIX_ROWS = 512
MIX_STEPS = 2
FFN_ROWS = 1024
FFN_TILE = 512
FFN_NORM_CHUNKS = 8
FFN_NORM_ROWS = 16
PROJ_ROWS = 1024
PROJ_TILE = 1024
PROJ_NORM_CHUNKS = 8
MOD_TILE = 1024
ATT_GROUP_FIRST = 16
ATT_GROUP_LATER = 15

BF16 = jnp.bfloat16
F32 = jnp.float32


def _params(*semantics):
    return pltpu.CompilerParams(dimension_semantics=semantics, vmem_limit_bytes=VMEM_LIMIT_BYTES)


def _rms(x):
    return x * lax.rsqrt(jnp.mean(x * x, axis=-1, keepdims=True) + NORM_EPS)


def _silu(x):
    h = 0.5 * x
    return h + h * jnp.tanh(h)


def _dot(a, b):
    return jnp.dot(a, b, preferred_element_type=F32)


def _dot_exact(a, b):
    return jnp.dot(a, b, preferred_element_type=F32, precision=lax.Precision.HIGHEST)


def _dot_nt(a, b):
    return lax.dot_general(a, b, (((1,), (1,)), ((), ())), preferred_element_type=F32)


def _mod_body(c_ref, w_ref, b_ref, o_ref):
    ca = _silu(c_ref[...]).astype(BF16)
    o_ref[...] = _dot(ca, w_ref[...].astype(BF16)) + b_ref[...]


def _modulation(c, w_mod, b_mod):
    depth, d, n = w_mod.shape
    b = c.shape[0]
    return pl.pallas_call(
        _mod_body,
        grid=(depth, n // MOD_TILE),
        in_specs=[
            pl.BlockSpec((b, d), lambda l, j: (0, 0)),
            pl.BlockSpec((None, d, MOD_TILE), lambda l, j: (l, 0, j)),
            pl.BlockSpec((None, 1, MOD_TILE), lambda l, j: (l, 0, j)),
        ],
        out_specs=pl.BlockSpec((None, b, MOD_TILE), lambda l, j: (l, 0, j)),
        out_shape=jax.ShapeDtypeStruct((depth, b, n), F32),
        compiler_params=_params("parallel", "parallel"),
        name="modulation",
    )(c, w_mod, b_mod.reshape(depth, 1, n))


def _ffn_body(xn_ref, xp_ref, modn_ref, modp_ref, gpre_ref, gpost_ref, wg_ref, wu_ref, wd_ref, o_ref,
              h0_ref, h1_ref, acc0_ref, acc1_ref, *, n_tiles, n_f):
    g = pl.program_id(0)
    f = pl.program_id(1)
    chunk = FFN_ROWS // FFN_NORM_CHUNKS
    c0 = jnp.minimum(f, FFN_NORM_CHUNKS - 1) * chunk

    @pl.when((g == 0) & (f == 0))
    def _():
        for ref in (h0_ref, h1_ref, acc0_ref, acc1_ref):
            ref[...] = jnp.zeros_like(ref)

    def pre_norm_chunk(h_ref):
        gain = gpre_ref[...] * (1.0 + modn_ref[1:2, :])
        for q in range(chunk // FFN_NORM_ROWS):
            rows = pl.ds(q * FFN_NORM_ROWS, FFN_NORM_ROWS)
            tile_rows = pl.ds(pl.multiple_of(c0 + q * FFN_NORM_ROWS, FFN_NORM_ROWS), FFN_NORM_ROWS)
            h_ref[tile_rows, :] = (_rms(xn_ref[rows, :]) * gain + modn_ref[0:1, :]).astype(BF16)

    def post_norm_chunk(acc_ref):
        gain = (FFN_RES_WEIGHT * (1.0 + modp_ref[2:3, :])) * gpost_ref[...]
        for q in range(chunk // FFN_NORM_ROWS):
            rows = pl.ds(q * FFN_NORM_ROWS, FFN_NORM_ROWS)
            tile_rows = pl.ds(pl.multiple_of(c0 + q * FFN_NORM_ROWS, FFN_NORM_ROWS), FFN_NORM_ROWS)
            o_ref[rows, :] = xp_ref[rows, :] + _rms(acc_ref[tile_rows, :]) * gain

    def swiglu_step(h_ref, acc_ref):
        h = h_ref[...]
        a = (_silu(_dot(h, wg_ref[...])) * _dot(h, wu_ref[...])).astype(BF16)
        acc_ref[...] = jnp.where(f > 0, acc_ref[...], 0.0) + _dot(a, wd_ref[...])

    has_matmul = (g >= 1) & (g <= n_tiles)
    for parity, (h_new, acc_old, h_mid, acc_mid) in enumerate(
            [(h0_ref, acc0_ref, h1_ref, acc1_ref), (h1_ref, acc1_ref, h0_ref, acc0_ref)]):
        @pl.when((lax.rem(g, 2) == parity) & has_matmul)
        def _():
            post_norm_chunk(acc_old)
            swiglu_step(h_mid, acc_mid)
            pre_norm_chunk(h_new)

        @pl.when((lax.rem(g, 2) == parity) & (g == 0) & (f < FFN_NORM_CHUNKS))
        def _():
            pre_norm_chunk(h_new)

        @pl.when((lax.rem(g, 2) == parity) & (g == n_tiles + 1) & (f < FFN_NORM_CHUNKS))
        def _():
            post_norm_chunk(acc_old)


def _ffn_sublayer(x, mod, g_pre, g_post, w_gate, w_up, w_down, layer, sub, idx):
    b, s, d = x.shape
    f_dim = w_gate.shape[-1]
    n_f = f_dim // FFN_TILE
    per_batch = s // FFN_ROWS
    n_tiles = b * per_batch
    chunks = FFN_NORM_CHUNKS
    chunk = FFN_ROWS // chunks
    assert n_f >= chunks
    new_tile = lambda g: jnp.minimum(g, n_tiles - 1)
    old_tile = lambda g: jnp.clip(g - 2, 0, n_tiles - 1)
    new_chunk = lambda g, f: (new_tile(g) * chunks + jnp.minimum(f, chunks - 1), 0, 0)
    old_chunk = lambda g, f: (old_tile(g) * chunks + jnp.minimum(f, chunks - 1), 0, 0)
    out_chunk = lambda g, f: (jnp.where(g < 2, 0, old_chunk(g, f)[0]), 0, 0)
    w_step = lambda g, f: jnp.where(g == 0, 0, jnp.where(g == n_tiles + 1, n_f - 1, f))
    x_chunks = x.reshape(n_tiles * chunks, chunk, d)
    out = pl.pallas_call(
        functools.partial(_ffn_body, n_tiles=n_tiles, n_f=n_f),
        grid=(n_tiles + 2, n_f),
        in_specs=[
            pl.BlockSpec((None, chunk, d), new_chunk),
            pl.BlockSpec((None, chunk, d), old_chunk),
            pl.BlockSpec((None, None, None, 3, d), lambda g, f: (layer, new_tile(g) // per_batch, sub, 0, 0)),
            pl.BlockSpec((None, None, None, 3, d), lambda g, f: (layer, old_tile(g) // per_batch, sub, 0, 0)),
            pl.BlockSpec((None, None, 1, d), lambda g, f: (layer, sub, 0, 0)),
            pl.BlockSpec((None, None, 1, d), lambda g, f: (layer, sub, 0, 0)),
            pl.BlockSpec((None, None, d, FFN_TILE), lambda g, f: (layer, idx, 0, w_step(g, f))),
            pl.BlockSpec((None, None, d, FFN_TILE), lambda g, f: (layer, idx, 0, w_step(g, f))),
            pl.BlockSpec((None, None, FFN_TILE, d), lambda g, f: (layer, idx, w_step(g, f), 0)),
        ],
        out_specs=pl.BlockSpec((None, chunk, d), out_chunk),
        out_shape=jax.ShapeDtypeStruct(x_chunks.shape, F32),
        scratch_shapes=[pltpu.VMEM((FFN_ROWS, d), BF16), pltpu.VMEM((FFN_ROWS, d), BF16),
                        pltpu.VMEM((FFN_ROWS, d), F32), pltpu.VMEM((FFN_ROWS, d), F32)],
        compiler_params=_params("arbitrary", "arbitrary"),
        name="ffn_sublayer",
    )(x_chunks, x_chunks, mod, mod, g_pre, g_post, w_gate, w_up, w_down)
    return out.reshape(b, s, d)


def _gelu_tanh(x):
    return 0.5 * x * (1.0 + jnp.tanh(0.7978845608028654 * (x + 0.044715 * (x * x * x))))


def _in_proj_body(*refs, hyb, n_a):
    if hyb:
        xn_ref, modn_ref, gpre_ref, wa_ref, wb_ref, wdt_ref, o_ref, dt_ref, h0_ref, h1_ref = refs
    else:
        xn_ref, modn_ref, gpre_ref, wa_ref, bias_ref, o_ref, h0_ref, h1_ref = refs
    g = pl.program_id(0)
    j = pl.program_id(1)
    chunk = PROJ_ROWS // PROJ_NORM_CHUNKS
    c0 = jnp.minimum(j, PROJ_NORM_CHUNKS - 1) * chunk

    @pl.when((g == 0) & (j == 0))
    def _():
        h0_ref[...] = jnp.zeros_like(h0_ref)
        h1_ref[...] = jnp.zeros_like(h1_ref)

    def pre_norm_chunk(h_ref):
        gain = gpre_ref[...] * (1.0 + modn_ref[1:2, :])
        for q in range(chunk // FFN_NORM_ROWS):
            rows = pl.ds(q * FFN_NORM_ROWS, FFN_NORM_ROWS)
            tile_rows = pl.ds(pl.multiple_of(c0 + q * FFN_NORM_ROWS, FFN_NORM_ROWS), FFN_NORM_ROWS)
            h_ref[tile_rows, :] = (_rms(xn_ref[rows, :]) * gain + modn_ref[0:1, :]).astype(BF16)

    def project(h_ref, w_ref):
        y = _dot(h_ref[...], w_ref[...])
        if not hyb:
            y = _gelu_tanh(y + bias_ref[...])
        o_ref[...] = y.astype(o_ref.dtype)

    has_matmul = g >= 1
    for parity, (h_new, h_mid) in enumerate([(h0_ref, h1_ref), (h1_ref, h0_ref)]):
        mine = lax.rem(g, 2) == parity
        if hyb:
            @pl.when(mine & has_matmul & (j == 0))
            def _():
                dt_ref[...] = _dot(h_mid[...], wdt_ref[...])

            @pl.when(mine & has_matmul & (j < n_a))
            def _():
                project(h_mid, wa_ref)
                pre_norm_chunk(h_new)

            @pl.when(mine & has_matmul & (j >= n_a))
            def _():
                project(h_mid, wb_ref)
                pre_norm_chunk(h_new)
        else:
            @pl.when(mine & has_matmul)
            def _():
                project(h_mid, wa_ref)
                pre_norm_chunk(h_new)

        @pl.when(mine & jnp.logical_not(has_matmul) & (j < PROJ_NORM_CHUNKS))
        def _():
            pre_norm_chunk(h_new)


def _in_proj(x, mod, g_pre, weights, layer, sub, hyb):
    b, s, d = x.shape
    per_batch = s // PROJ_ROWS
    n_tiles = b * per_batch
    chunks = PROJ_NORM_CHUNKS
    chunk = PROJ_ROWS // chunks
    w_a = weights[0]
    n_a = w_a.shape[1] // PROJ_TILE
    n_steps = n_a + (weights[1].shape[1] // PROJ_TILE if hyb else 0)
    assert n_steps >= chunks
    new_tile = lambda g: jnp.minimum(g, n_tiles - 1)
    mid_tile = lambda g: jnp.maximum(g - 1, 0)
    col = lambda g, j: jnp.where(g == 0, 0, j)
    in_specs = [
        pl.BlockSpec((None, chunk, d), lambda g, j: (new_tile(g) * chunks + jnp.minimum(j, chunks - 1), 0, 0)),
        pl.BlockSpec((None, None, None, 3, d), lambda g, j: (layer, new_tile(g) // per_batch, sub, 0, 0)),
        pl.BlockSpec((None, None, 1, d), lambda g, j: (layer, sub, 0, 0)),
        pl.BlockSpec((d, PROJ_TILE), lambda g, j: (0, jnp.minimum(col(g, j), n_a - 1))),
    ]
    out_specs = [pl.BlockSpec((None, PROJ_ROWS, PROJ_TILE), lambda g, j: (mid_tile(g), 0, col(g, j)))]
    out_shape = [jax.ShapeDtypeStruct((n_tiles, PROJ_ROWS, n_steps * PROJ_TILE), BF16)]
    if hyb:
        in_specs += [
            pl.BlockSpec((d, PROJ_TILE), lambda g, j: (0, jnp.maximum(col(g, j) - n_a, 0))),
            pl.BlockSpec((d, LANES), lambda g, j: (0, 0)),
        ]
        out_specs.append(pl.BlockSpec((None, PROJ_ROWS, LANES), lambda g, j: (mid_tile(g), 0, 0)))
        out_shape.append(jax.ShapeDtypeStruct((n_tiles, PROJ_ROWS, LANES), F32))
    else:
        in_specs.append(pl.BlockSpec((1, PROJ_TILE), lambda g, j: (0, col(g, j))))
    outs = pl.pallas_call(
        functools.partial(_in_proj_body, hyb=hyb, n_a=n_a),
        grid=(n_tiles + 1, n_steps),
        in_specs=in_specs,
        out_specs=out_specs,
        out_shape=out_shape,
        scratch_shapes=[pltpu.VMEM((PROJ_ROWS, d), BF16)] * 2,
        compiler_params=_params("arbitrary", "arbitrary"),
        name="in_proj",
    )(x.reshape(n_tiles * chunks, chunk, d), mod, g_pre, *weights)
    return [o.reshape(b, s, o.shape[-1]) for o in outs]


def _sgu_gate_chunk(u, v, lng_ref, lnb_ref, ws_ref, bs_ref):
    L = SGU_CHUNK
    v = v.astype(F32)
    mu = jnp.mean(v, axis=-1, keepdims=True)
    vc = v - mu
    var = jnp.mean(vc * vc, axis=-1, keepdims=True)
    vn = (vc * lax.rsqrt(var + NORM_EPS) * lng_ref[...] + lnb_ref[...]).astype(BF16)
    row = lax.broadcasted_iota(jnp.int32, (L, L), 0)
    col = lax.broadcasted_iota(jnp.int32, (L, L), 1)
    causal = row >= col
    parts = []
    for g in range(SGU_GROUPS):
        gs = slice(g * SGU_GROUP_WIDTH, (g + 1) * SGU_GROUP_WIDTH)
        w = jnp.where(causal, ws_ref[g], 0.0).astype(BF16)
        mixed = _dot(w, vn[:, gs]) + bs_ref[:, g:g + 1]
        parts.append((u[:, gs].astype(F32) * mixed).astype(BF16))
    return jnp.concatenate(parts, axis=1)


def _mixer_out_body(*refs, sgu, n_tiles):
    if sgu:
        (u_ref, v_ref, lng_ref, lnb_ref, ws_ref, bs_ref, w_ref, xp_ref, modp_ref, gpost_ref, o_ref,
         lhs0_ref, lhs1_ref, acc0_ref, acc1_ref) = refs
    else:
        a_ref, b_ref, w_ref, xp_ref, modp_ref, gpost_ref, o_ref, acc0_ref, acc1_ref = refs
        lhs0_ref = lhs1_ref = None
    g = pl.program_id(0)
    k = pl.program_id(1)
    chunk = MIX_ROWS // MIX_STEPS
    half = w_ref.shape[0] // MIX_STEPS

    @pl.when((g == 0) & (k == 0))
    def _():
        for ref in (lhs0_ref, lhs1_ref, acc0_ref, acc1_ref):
            if ref is not None:
                ref[...] = jnp.zeros_like(ref)

    def post_norm_chunk(acc_ref):
        gain = (MIXER_RES_WEIGHT * (1.0 + modp_ref[2:3, :])) * gpost_ref[...]
        for q in range(chunk // FFN_NORM_ROWS):
            rows = pl.ds(q * FFN_NORM_ROWS, FFN_NORM_ROWS)
            tile_rows = pl.ds(pl.multiple_of(k * chunk + q * FFN_NORM_ROWS, FFN_NORM_ROWS), FFN_NORM_ROWS)
            o_ref[rows, :] = xp_ref[rows, :] + _rms(acc_ref[tile_rows, :]) * gain

    def gate_chunk(lhs_ref):
        for j in range(chunk // SGU_CHUNK):
            rows = slice(j * SGU_CHUNK, (j + 1) * SGU_CHUNK)
            gated = _sgu_gate_chunk(u_ref[rows, :], v_ref[rows, :], lng_ref, lnb_ref, ws_ref, bs_ref)
            tile_rows = pl.ds(pl.multiple_of(k * chunk + j * SGU_CHUNK, SGU_CHUNK), SGU_CHUNK)
            for part in range(MIX_STEPS):
                lhs_ref[part, tile_rows, :] = gated[:, part * half:(part + 1) * half]

    def matmul_step(lhs_ref, acc_ref):
        lhs = lhs_ref[k] if sgu else jnp.where(k == 0, a_ref[...], b_ref[...])
        w = w_ref[pl.ds(pl.multiple_of(k * half, half), half), :]
        acc_ref[...] = jnp.where(k > 0, acc_ref[...], 0.0) + _dot(lhs, w)

    has_matmul = (g >= 1) & (g <= n_tiles)
    for parity, (lhs_new, acc_old, lhs_mid, acc_mid) in enumerate(
            [(lhs0_ref, acc0_ref, lhs1_ref, acc1_ref), (lhs1_ref, acc1_ref, lhs0_ref, acc0_ref)]):
        @pl.when((lax.rem(g, 2) == parity) & has_matmul)
        def _():
            post_norm_chunk(acc_old)
            matmul_step(lhs_mid, acc_mid)
            if sgu:
                gate_chunk(lhs_new)

        if sgu:
            @pl.when((lax.rem(g, 2) == parity) & (g == 0))
            def _():
                gate_chunk(lhs_new)

        @pl.when((lax.rem(g, 2) == parity) & (g == n_tiles + 1))
        def _():
            post_norm_chunk(acc_old)


def _mixer_out(operands, w, x, mod, g_post, layer, sub, sgu):
    b, s, d = x.shape
    half = w.shape[0] // MIX_STEPS
    per_batch = s // MIX_ROWS
    n_tiles = b * per_batch
    steps = MIX_STEPS
    chunk = MIX_ROWS // steps
    new_tile = lambda g: jnp.minimum(g, n_tiles - 1)
    mid_tile = lambda g: jnp.clip(g - 1, 0, n_tiles - 1)
    old_tile = lambda g: jnp.clip(g - 2, 0, n_tiles - 1)
    new_chunk = lambda g, k: new_tile(g) * steps + k
    old_chunk = lambda g, k: old_tile(g) * steps + k
    out_chunk = lambda g, k: (jnp.where(g < 2, 0, old_chunk(g, k)), 0, 0)
    x_chunks = x.reshape(n_tiles * steps, chunk, d)
    common_specs = [
        pl.BlockSpec((steps * half, d), lambda g, k: (0, 0), pipeline_mode=pl.Buffered(1)),
        pl.BlockSpec((None, chunk, d), lambda g, k: (old_chunk(g, k), 0, 0)),
        pl.BlockSpec((None, None, None, 3, d), lambda g, k: (layer, old_tile(g) // per_batch, sub, 0, 0)),
        pl.BlockSpec((None, None, 1, d), lambda g, k: (layer, sub, 0, 0)),
    ]
    acc = [pltpu.VMEM((MIX_ROWS, d), F32)] * 2
    if sgu:
        zz, ln_g, ln_b, w_spatial, b_spatial = operands
        zz_chunks = zz.reshape(n_tiles * steps, chunk, 2 * SGU_WIDTH)
        vec = pl.BlockSpec((1, SGU_WIDTH), lambda g, k: (0, 0))
        lhs_specs = [
            pl.BlockSpec((None, chunk, SGU_WIDTH), lambda g, k: (new_chunk(g, k), 0, 0)),
            pl.BlockSpec((None, chunk, SGU_WIDTH), lambda g, k: (new_chunk(g, k), 0, 1)),
            vec, vec,
            pl.BlockSpec((SGU_GROUPS, SGU_CHUNK, SGU_CHUNK), lambda g, k: (0, 0, 0)),
            pl.BlockSpec((SGU_CHUNK, SGU_GROUPS), lambda g, k: (0, 0)),
        ]
        lhs_args = [zz_chunks, zz_chunks, ln_g.reshape(1, SGU_WIDTH), ln_b.reshape(1, SGU_WIDTH), w_spatial,
                    b_spatial.T]
        scratch = [pltpu.VMEM((steps, MIX_ROWS, half), BF16)] * 2 + acc
    else:
        y_a, y_b = operands
        tile = pl.BlockSpec((None, MIX_ROWS, half), lambda g, k: (mid_tile(g), 0, 0))
        lhs_specs = [tile, tile]
        lhs_args = [y_a.reshape(n_tiles, MIX_ROWS, half), y_b.reshape(n_tiles, MIX_ROWS, half)]
        scratch = acc
    out = pl.pallas_call(
        functools.partial(_mixer_out_body, sgu=sgu, n_tiles=n_tiles),
        grid=(n_tiles + 2, steps),
        in_specs=lhs_specs + common_specs,
        out_specs=pl.BlockSpec((None, chunk, d), out_chunk),
        out_shape=jax.ShapeDtypeStruct(x_chunks.shape, F32),
        scratch_shapes=scratch,
        compiler_params=_params("arbitrary", "arbitrary"),
        name="mixer_out",
    )(*lhs_args, w, x_chunks, mod, g_post)
    return out.reshape(b, s, d)


def _rope_body(pos_ref, freq_ref, cos_ref, sin_ref):
    ang = pos_ref[...].astype(F32) * freq_ref[...]
    lane = lax.broadcasted_iota(jnp.int32, ang.shape, 1)
    cos_ref[...] = jnp.cos(ang)
    sin_ref[...] = jnp.where(lane < ATT_HEAD_DIM // 2, -1.0, 1.0) * jnp.sin(ang)


def _rope_tables(positions):
    b, s = positions.shape
    half = ATT_HEAD_DIM // 2
    inv_freq = ROPE_THETA ** (-jnp.arange(half, dtype=F32) / half)
    freq = jnp.concatenate([inv_freq, inv_freq]).reshape(1, ATT_HEAD_DIM)
    tile = 512
    spec = pl.BlockSpec((None, tile, ATT_HEAD_DIM), lambda bi, i: (bi, i, 0))
    return pl.pallas_call(
        _rope_body,
        grid=(b, s // tile),
        in_specs=[
            pl.BlockSpec((None, tile, 1), lambda bi, i: (bi, i, 0)),
            pl.BlockSpec((1, ATT_HEAD_DIM), lambda bi, i: (0, 0)),
        ],
        out_specs=[spec, spec],
        out_shape=[jax.ShapeDtypeStruct((b, s, ATT_HEAD_DIM), F32)] * 2,
        compiler_params=_params("parallel", "parallel"),
        name="rope_tables",
    )(positions.reshape(b, s, 1), freq)


def _ssd_body(xs_ref, bm_ref, cm_ref, z_ref, dt_ref, cw_ref, cb_ref, dtb_ref, alog_ref, dskip_ref, ng_ref,
              o_ref, tail_ref, state_ref):
    L = SSD_CHUNK
    T = SSD_TAIL

    @pl.when(pl.program_id(1) == 0)
    def _():
        tail_ref[...] = jnp.zeros_like(tail_ref)
        state_ref[...] = jnp.zeros_like(state_ref)

    srow = lax.broadcasted_iota(jnp.int32, ((SSD_CONV - 1) * L, L + T), 0)
    scol = lax.broadcasted_iota(jnp.int32, ((SSD_CONV - 1) * L, L + T), 1)
    lag = (srow >> (L.bit_length() - 1)) + 1
    t_in = srow & (L - 1)
    shift = (scol == jnp.where(t_in >= lag, t_in - lag, t_in - lag + (L + T))).astype(BF16)

    def conv_silu(raw_ref, lo, hi):
        raw = raw_ref[...]
        lagged = _dot(shift, jnp.concatenate([raw, tail_ref[:, lo:hi]], axis=0))
        acc = cb_ref[:, lo:hi] + raw.astype(F32) * cw_ref[SSD_CONV - 1:SSD_CONV, lo:hi]
        for k in range(1, SSD_CONV):
            acc = acc + lagged[(k - 1) * L:k * L] * cw_ref[SSD_CONV - 1 - k:SSD_CONV - k, lo:hi]
        tail_ref[:, lo:hi] = raw[L - T:L]
        return _silu(acc)

    xs = conv_silu(xs_ref, 0, SSD_WIDTH)
    bm = conv_silu(bm_ref, SSD_WIDTH, SSD_WIDTH + SSD_BC_WIDTH)
    cm = conv_silu(cm_ref, SSD_WIDTH + SSD_BC_WIDTH, SSD_CONV_CH)

    dt_in = dt_ref[...] + dtb_ref[...]
    dt = jnp.maximum(dt_in, 0.0) + jnp.log1p(jnp.exp(-jnp.abs(dt_in)))
    adt = dt * (-jnp.exp(alog_ref[...]))
    row = lax.broadcasted_iota(jnp.int32, (L, L), 0)
    col = lax.broadcasted_iota(jnp.int32, (L, L), 1)
    causal = row >= col
    acs = _dot_exact(causal.astype(F32), adt)
    acs_t = acs.T
    dt_t = dt.T
    acs_last = acs[L - 1:L, :]

    hrow = lax.broadcasted_iota(jnp.int32, (LANES, SSD_WIDTH), 0)
    hcol = lax.broadcasted_iota(jnp.int32, (LANES, SSD_WIDTH), 1)
    expand = (hrow == (hcol >> (SSD_HEAD_DIM.bit_length() - 1))).astype(BF16)
    stacked = jnp.concatenate([jnp.exp(acs), jnp.exp(acs_last - acs) * dt], axis=0)
    high = stacked.astype(BF16)
    rest = (stacked - high.astype(F32)).astype(BF16)
    wide = _dot(high, expand) + _dot(rest, expand)
    decay_in_w, dt_decay_out_w = wide[0:L], wide[L:2 * L]
    chunk_decay_w = decay_in_w[L - 1:L, :]

    lane = lax.broadcasted_iota(jnp.int32, (L, LANES), 1)
    first_head = lane < SSD_HEAD_DIM

    y_parts = []
    for g in range(SSD_GROUPS):
        gs = slice(g * SSD_GROUP_WIDTH, (g + 1) * SSD_GROUP_WIDTH)
        bg = bm[:, g * SSD_STATE:(g + 1) * SSD_STATE]
        cg = cm[:, g * SSD_STATE:(g + 1) * SSD_STATE].astype(BF16)
        cb = _dot_nt(cg, bg.astype(BF16))
        state = state_ref[g]
        y_off = _dot(cg, state.astype(BF16)) * decay_in_w[:, gs]
        y_diag = []
        for j in range(SSD_HEADS_PER_GROUP // 2):
            h0 = g * SSD_HEADS_PER_GROUP + 2 * j
            ms = []
            for h in (h0, h0 + 1):
                seg = acs[:, h:h + 1] - acs_t[h:h + 1, :]
                ms.append(cb * jnp.exp(jnp.where(causal, seg, -jnp.inf)) * dt_t[h:h + 1, :])
            lhs = jnp.concatenate(ms, axis=1).astype(BF16)
            xp = xs[:, h0 * SSD_HEAD_DIM:(h0 + 2) * SSD_HEAD_DIM]
            rhs = jnp.concatenate([jnp.where(first_head, xp, 0.0), jnp.where(first_head, 0.0, xp)], axis=0)
            y_diag.append(_dot(lhs, rhs.astype(BF16)))
        y_parts.append(jnp.concatenate(y_diag, axis=1) + y_off)
        contrib = _dot(bg.T.astype(BF16), (xs[:, gs] * dt_decay_out_w[:, gs]).astype(BF16))
        state_ref[g] = state * chunk_decay_w[:, gs] + contrib

    y = jnp.concatenate(y_parts, axis=1) + xs * dskip_ref[...]
    y = y * _silu(z_ref[...].astype(F32))
    o_ref[...] = (_rms(y) * ng_ref[...]).astype(o_ref.dtype)


def _ssd_mixer(proj, dt_raw, conv_w, conv_b, dt_bias, a_log, d_skip, norm_g):
    b, s, _ = proj.shape
    assert proj.dtype == BF16
    L = SSD_CHUNK
    z_blk = 0
    xs_blk = SSD_WIDTH // SSD_WIDTH
    bm_blk = (2 * SSD_WIDTH) // SSD_BC_WIDTH
    cm_blk = bm_blk + 1
    pad = LANES - SSD_HEADS
    small = lambda a: pl.BlockSpec(a.shape, lambda bi, c: (0, 0))
    dt_bias_p = jnp.pad(dt_bias, (0, pad)).reshape(1, LANES)
    a_log_p = jnp.pad(a_log, (0, pad)).reshape(1, LANES)
    d_skip_w = jnp.repeat(d_skip, SSD_HEAD_DIM).reshape(1, SSD_WIDTH)
    conv_b2 = conv_b.reshape(1, SSD_CONV_CH)
    norm_g2 = norm_g.reshape(1, SSD_WIDTH)
    return pl.pallas_call(
        _ssd_body,
        grid=(b, s // L),
        in_specs=[
            pl.BlockSpec((None, L, SSD_WIDTH), lambda bi, c: (bi, c, xs_blk)),
            pl.BlockSpec((None, L, SSD_BC_WIDTH), lambda bi, c: (bi, c, bm_blk)),
            pl.BlockSpec((None, L, SSD_BC_WIDTH), lambda bi, c: (bi, c, cm_blk)),
            pl.BlockSpec((None, L, SSD_WIDTH), lambda bi, c: (bi, c, z_blk)),
            pl.BlockSpec((None, L, LANES), lambda bi, c: (bi, c, 0)),
            small(conv_w), small(conv_b2), small(dt_bias_p), small(a_log_p), small(d_skip_w), small(norm_g2),
        ],
        out_specs=pl.BlockSpec((None, L, SSD_WIDTH), lambda bi, c: (bi, c, 0)),
        out_shape=jax.ShapeDtypeStruct((b, s, SSD_WIDTH), BF16),
        scratch_shapes=[
            pltpu.VMEM((SSD_TAIL, SSD_CONV_CH), BF16),
            pltpu.VMEM((SSD_GROUPS, SSD_STATE, SSD_GROUP_WIDTH), F32),
        ],
        compiler_params=_params("parallel", "arbitrary"),
        name="ssd_mixer",
    )(proj, proj, proj, proj, dt_raw, conv_w, conv_b2, dt_bias_p, a_log_p, d_skip_w, norm_g2)


def _attn_body(q_ref, k_ref, v_ref, cos_ref, sin_ref, o_ref, qkv_ref, acc_ref, m_ref, l_ref, *, seq):
    blk = ATT_BLOCK
    half = ATT_HEAD_DIM // 2
    dils = [d for _, d in DILATED_PATTERNS]
    step = dils[1]
    assert dils == [1, step, step * step] and all(w // d == blk for w, d in DILATED_PATTERNS)
    sub = seq // step
    assert seq // dils[2] == blk

    cos = cos_ref[...]
    sin = sin_ref[...]
    q = q_ref[...].astype(F32)
    k = k_ref[...].astype(F32)
    qkv_ref[0, 0] = (q * cos + pltpu.roll(q, half, 1) * sin) * (ATT_HEAD_DIM ** -0.5 * LOG2_E)
    qkv_ref[0, 1] = k * cos + pltpu.roll(k, half, 1) * sin
    qkv_ref[0, 2] = v_ref[...].astype(F32)
    for t in range(3):
        for r in range(step):
            qkv_ref[1, t, pl.ds(r * sub, sub), :] = qkv_ref[0, t, pl.ds(r, sub, stride=step), :]
    for t in range(3):
        for r in range(step):
            for a in range(step):
                qkv_ref[2, t, pl.ds((r + step * a) * blk, blk), :] = qkv_ref[1, t, pl.ds(r * sub + a, blk, stride=step), :]

    row = lax.broadcasted_iota(jnp.int32, (blk, blk), 0)
    col = lax.broadcasted_iota(jnp.int32, (blk, blk), 1)
    cur_ok = col <= row
    prev_ok = col >= row

    def attend(p, starts, with_prev):
        def load(t, st):
            return qkv_ref[p, t, pl.ds(st, blk), :].astype(BF16)

        def window(t, st):
            return jnp.concatenate([load(t, st - blk), load(t, st)], axis=0) if with_prev else load(t, st)

        qb = jnp.stack([load(0, st) for st in starts])
        kk = jnp.stack([window(1, st) for st in starts])
        keys = kk.shape[1]
        vv = jnp.stack([jnp.concatenate([window(2, st), jnp.ones((keys, LANES), BF16)], axis=1) for st in starts])
        ok = jnp.concatenate([prev_ok, cur_ok], axis=1) if with_prev else cur_ok
        s = lax.dot_general(qb, kk, (((2,), (2,)), ((0,), (0,))), preferred_element_type=F32)
        s = jnp.where(ok[None], s, -jnp.inf)
        m = jnp.max(s, axis=2, keepdims=True)
        e = jnp.exp2(s - m)
        acc = lax.dot_general(e.astype(BF16), vv, (((2,), (1,)), ((0,), (0,))), preferred_element_type=F32)
        for i, st in enumerate(starts):
            rows = pl.ds(st, blk)
            acc_ref[p, rows, :] = acc[i, :, 0:ATT_HEAD_DIM]
            l_ref[p, rows, :] = acc[i, :, ATT_HEAD_DIM:]
            m_ref[p, rows, :] = jnp.broadcast_to(m[i], (blk, LANES))

    def groups(starts, size):
        return [starts[i:i + size] for i in range(0, len(starts), size)]

    for p, dil in enumerate(dils):
        class_rows = seq // dil
        firsts = [r * class_rows for r in range(dil)]
        laters = [r * class_rows + n * blk for r in range(dil) for n in range(1, class_rows // blk)]
        for g in groups(firsts, ATT_GROUP_FIRST):
            attend(p, g, False)
        for g in groups(laters, ATT_GROUP_LATER):
            attend(p, g, True)

    def merged(dst, dst_rows, src, src_rows):
        m_a, m_b = m_ref[dst, dst_rows, :], m_ref[src, src_rows, :]
        top = jnp.maximum(m_a, m_b)
        w_a, w_b = jnp.exp2(m_a - top), jnp.exp2(m_b - top)
        acc = w_a * acc_ref[dst, dst_rows, :] + w_b * acc_ref[src, src_rows, :]
        return top, acc, w_a * l_ref[dst, dst_rows, :] + w_b * l_ref[src, src_rows, :]

    for r in range(step):
        for a in range(step):
            mid_rows = pl.ds(r * sub + a, blk, stride=step)
            top, acc, l = merged(1, mid_rows, 2, pl.ds((r + step * a) * blk, blk))
            m_ref[1, mid_rows, :] = top
            acc_ref[1, mid_rows, :] = acc
            l_ref[1, mid_rows, :] = l
    for r in range(step):
        for n in range(sub // blk):
            nat_rows = pl.ds(r + n * blk * step, blk, stride=step)
            _, acc, l = merged(0, nat_rows, 1, pl.ds(r * sub + n * blk, blk))
            acc_ref[0, nat_rows, :] = acc / l
    o_ref[...] = acc_ref[0].astype(o_ref.dtype)


def _dilated_attention(proj, cos2, sin2):
    b, s, _ = proj.shape
    q_blk = (2 * SSD_WIDTH + 2 * SSD_BC_WIDTH) // ATT_HEAD_DIM
    k_blk = q_blk + ATT_HEADS
    v_blk = k_blk + ATT_HEADS
    n_pat = len(DILATED_PATTERNS)
    head = lambda base: pl.BlockSpec((None, s, ATT_HEAD_DIM), lambda bi, h: (bi, 0, base + h))
    table = pl.BlockSpec((None, s, ATT_HEAD_DIM), lambda bi, h: (bi, 0, 0))
    return pl.pallas_call(
        functools.partial(_attn_body, seq=s),
        grid=(b, ATT_HEADS),
        in_specs=[head(q_blk), head(k_blk), head(v_blk), table, table],
        out_specs=pl.BlockSpec((None, s, ATT_HEAD_DIM), lambda bi, h: (bi, 0, h)),
        out_shape=jax.ShapeDtypeStruct((b, s, ATT_WIDTH), BF16),
        scratch_shapes=[
            pltpu.VMEM((n_pat, 3, s, ATT_HEAD_DIM), F32),
            pltpu.VMEM((n_pat, s, ATT_HEAD_DIM), F32),
            pltpu.VMEM((n_pat, s, LANES), F32),
            pltpu.VMEM((n_pat, s, LANES), F32),
        ],
        compiler_params=_params("parallel", "arbitrary"),
        name="dilated_attention",
    )(proj, proj, proj, cos2, sin2)


def kernel(x, c, positions, w_mod, b_mod, norm_pre, norm_post, ffn_w_gate, ffn_w_up, ffn_w_down, hyb_w_in, hyb_conv_w, hyb_conv_b, hyb_dt_bias, hyb_a_log, hyb_d_skip, hyb_norm_g, hyb_w_out, sgu_w_in, sgu_b_in, sgu_ln_g, sgu_ln_b, sgu_w_spatial, sgu_b_spatial, sgu_w_out):
    depth = w_mod.shape[0]
    b, s, d = x.shape
    n_sub = norm_pre.shape[1]

    mod = _modulation(c, w_mod, b_mod).reshape(depth, b, n_sub, 3, d)
    g_pre = norm_pre.reshape(depth, n_sub, 1, d)
    g_post = norm_post.reshape(depth, n_sub, 1, d)
    w_gate = ffn_w_gate.astype(BF16)
    w_up = ffn_w_up.astype(BF16)
    w_down = ffn_w_down.astype(BF16)

    for layer in range(depth):
        i = layer // 2
        x = _ffn_sublayer(x, mod, g_pre, g_post, w_gate, w_up, w_down, layer, 0, 0)
        if layer % 2 == 0:
            w_in = hyb_w_in[i]
            dt_lo = SSD_WIDTH + SSD_CONV_CH
            dt_hi = dt_lo + SSD_HEADS
            w_dt = jnp.pad(w_in[:, dt_lo:dt_hi], ((0, 0), (0, LANES - SSD_HEADS))).astype(BF16)
            proj, dt_raw = _in_proj(x, mod, g_pre, (w_in[:, :dt_lo].astype(BF16), w_in[:, dt_hi:].astype(BF16), w_dt),
                                    layer, 1, hyb=True)
            cos2, sin2 = _rope_tables(positions)
            y_a = _ssd_mixer(proj, dt_raw, hyb_conv_w[i], hyb_conv_b[i], hyb_dt_bias[i], hyb_a_log[i],
                             hyb_d_skip[i], hyb_norm_g[i])
            y_b = _dilated_attention(proj, cos2, sin2)
            x = _mixer_out((y_a, y_b), hyb_w_out[i].astype(BF16), x, mod, g_post, layer, 1, sgu=False)
        else:
            zz, = _in_proj(x, mod, g_pre, (sgu_w_in[i].astype(BF16), sgu_b_in[i].reshape(1, -1)), layer, 1,
                           hyb=False)
            x = _mixer_out((zz, sgu_ln_g[i], sgu_ln_b[i], sgu_w_spatial[i], sgu_b_spatial[i]),
                           sgu_w_out[i].astype(BF16), x, mod, g_post, layer, 1, sgu=True)
        x = _ffn_sublayer(x, mod, g_pre, g_post, w_gate, w_up, w_down, layer, 2, 1)
    return x
```

```python
import functools

import jax
import jax.numpy as jnp
from jax import lax
from jax.experimental import pallas as pl
from jax.experimental.pallas import tpu as pltpu

NORM_EPS = 1e-6
LOG2_E = 1.4426950408889634
FFN_RES_WEIGHT = 0.5
MIXER_RES_WEIGHT = 1.0

SSD_HEADS = 32
SSD_HEAD_DIM = 64
SSD_WIDTH = SSD_HEADS * SSD_HEAD_DIM
SSD_GROUPS = 4
SSD_STATE = 128
SSD_CONV = 4
SSD_CHUNK = 128
SSD_TAIL = 16
SSD_BC_WIDTH = SSD_GROUPS * SSD_STATE
SSD_CONV_CH = SSD_WIDTH + 2 * SSD_BC_WIDTH
SSD_HEADS_PER_GROUP = SSD_HEADS // SSD_GROUPS
SSD_GROUP_WIDTH = SSD_HEADS_PER_GROUP * SSD_HEAD_DIM

ATT_HEADS = 16
ATT_HEAD_DIM = 128
ATT_WIDTH = ATT_HEADS * ATT_HEAD_DIM
ATT_BLOCK = 128
DILATED_PATTERNS = ((128, 1), (512, 4), (2048, 16))
ROPE_THETA = 10000.0

SGU_WIDTH = 4096
SGU_GROUPS = 8
SGU_CHUNK = 128
SGU_GROUP_WIDTH = SGU_WIDTH // SGU_GROUPS

LANES = 128
SUBLANES = 8
VMEM_LIMIT_BYTES = 56 * 1024 * 1024

MIX_ROWS = 512
MIX_STEPS = 2
FFN_ROWS = 1024
FFN_TILE = 512
FFN_NORM_CHUNKS = 8
FFN_NORM_ROWS = 16
PROJ_ROWS = 1024
PROJ_TILE = 1024
PROJ_BLOCK = 512
PROJ_NORM_CHUNKS = 8
MOD_TILE = 1024
ATT_GROUP_FIRST = 16
ATT_GROUP_LATER = 15

BF16 = jnp.bfloat16
F32 = jnp.float32


def _params(*semantics):
    return pltpu.CompilerParams(dimension_semantics=semantics, vmem_limit_bytes=VMEM_LIMIT_BYTES)


def _rms(x):
    return x * lax.rsqrt(jnp.mean(x * x, axis=-1, keepdims=True) + NORM_EPS)


def _silu(x):
    h = 0.5 * x
    return h + h * jnp.tanh(h)


def _dot(a, b):
    return jnp.dot(a, b, preferred_element_type=F32)


def _dot_exact(a, b):
    return jnp.dot(a, b, preferred_element_type=F32, precision=lax.Precision.HIGHEST)


def _zero_after(done, never, width):
    total = jnp.zeros((SUBLANES, LANES), F32)
    for res in done:
        for r in range(0, res.shape[0], SUBLANES):
            for c in range(0, res.shape[1], LANES):
                total = total + res[r:r + SUBLANES, c:c + LANES]
    row = jnp.sum(jnp.where(never, total, 0.0), axis=0, keepdims=True)
    return jnp.tile(row, (1, width // LANES))


def _dot_nt(a, b):
    return lax.dot_general(a, b, (((1,), (1,)), ((), ())), preferred_element_type=F32)


def _mod_body(c_ref, w_ref, b_ref, o_ref):
    ca = _silu(c_ref[...]).astype(BF16)
    o_ref[...] = _dot(ca, w_ref[...].astype(BF16)) + b_ref[...]


def _modulation(c, w_mod, b_mod):
    depth, d, n = w_mod.shape
    b = c.shape[0]
    return pl.pallas_call(
        _mod_body,
        grid=(depth, n // MOD_TILE),
        in_specs=[
            pl.BlockSpec((b, d), lambda l, j: (0, 0)),
            pl.BlockSpec((None, d, MOD_TILE), lambda l, j: (l, 0, j)),
            pl.BlockSpec((None, 1, MOD_TILE), lambda l, j: (l, 0, j)),
        ],
        out_specs=pl.BlockSpec((None, b, MOD_TILE), lambda l, j: (l, 0, j)),
        out_shape=jax.ShapeDtypeStruct((depth, b, n), F32),
        compiler_params=_params("parallel", "parallel"),
        name="modulation",
    )(c, w_mod, b_mod.reshape(depth, 1, n))


def _ffn_body(xn_ref, xp_ref, modn_ref, modp_ref, gpre_ref, gpost_ref, wg_ref, wu_ref, wd_ref, o_ref,
              h0_ref, h1_ref, acc0_ref, acc1_ref, *, n_tiles, n_f):
    g = pl.program_id(0)
    f = pl.program_id(1)
    chunk = FFN_ROWS // FFN_NORM_CHUNKS
    c0 = jnp.minimum(f, FFN_NORM_CHUNKS - 1) * chunk

    @pl.when((g == 0) & (f == 0))
    def _():
        for ref in (h0_ref, h1_ref, acc0_ref, acc1_ref):
            ref[...] = jnp.zeros_like(ref)

    def pre_norm_chunk(h_ref):
        gain = gpre_ref[...] * (1.0 + modn_ref[1:2, :])
        done = []
        for q in range(chunk // FFN_NORM_ROWS):
            rows = pl.ds(q * FFN_NORM_ROWS, FFN_NORM_ROWS)
            tile_rows = pl.ds(pl.multiple_of(c0 + q * FFN_NORM_ROWS, FFN_NORM_ROWS), FFN_NORM_ROWS)
            res = _rms(xn_ref[rows, :]) * gain + modn_ref[0:1, :]
            h_ref[tile_rows, :] = res.astype(BF16)
            done.append(res)
        return done

    def post_norm_chunk(acc_ref):
        gain = (FFN_RES_WEIGHT * (1.0 + modp_ref[2:3, :])) * gpost_ref[...]
        done = []
        for q in range(chunk // FFN_NORM_ROWS):
            rows = pl.ds(q * FFN_NORM_ROWS, FFN_NORM_ROWS)
            tile_rows = pl.ds(pl.multiple_of(c0 + q * FFN_NORM_ROWS, FFN_NORM_ROWS), FFN_NORM_ROWS)
            res = xp_ref[rows, :] + _rms(acc_ref[tile_rows, :]) * gain
            o_ref[rows, :] = res
            done.append(res)
        return done

    def zero_after(done):
        return _zero_after(done, f < 0, FFN_TILE)

    def swiglu_step(h_ref, acc_ref, zero):
        h = h_ref[...]
        a = (_silu(_dot(h, wg_ref[...])) * _dot(h, wu_ref[...]) + zero).astype(BF16)
        acc_ref[...] = jnp.where(f > 0, acc_ref[...], 0.0) + _dot(a, wd_ref[...])

    has_matmul = (g >= 1) & (g <= n_tiles)
    for parity, (h_new, acc_old, h_mid, acc_mid) in enumerate(
            [(h0_ref, acc0_ref, h1_ref, acc1_ref), (h1_ref, acc1_ref, h0_ref, acc0_ref)]):
        @pl.when((lax.rem(g, 2) == parity) & has_matmul)
        def _():
            swiglu_step(h_mid, acc_mid, zero_after(post_norm_chunk(acc_old) + pre_norm_chunk(h_new)))

        @pl.when((lax.rem(g, 2) == parity) & (g == 0) & (f < FFN_NORM_CHUNKS))
        def _():
            pre_norm_chunk(h_new)

        @pl.when((lax.rem(g, 2) == parity) & (g == n_tiles + 1) & (f < FFN_NORM_CHUNKS))
        def _():
            post_norm_chunk(acc_old)


def _ffn_sublayer(x, mod, g_pre, g_post, w_gate, w_up, w_down, layer, sub, idx):
    b, s, d = x.shape
    f_dim = w_gate.shape[-1]
    n_f = f_dim // FFN_TILE
    per_batch = s // FFN_ROWS
    n_tiles = b * per_batch
    chunks = FFN_NORM_CHUNKS
    chunk = FFN_ROWS // chunks
    assert n_f >= chunks
    new_tile = lambda g: jnp.minimum(g, n_tiles - 1)
    old_tile = lambda g: jnp.clip(g - 2, 0, n_tiles - 1)
    new_chunk = lambda g, f: (new_tile(g) * chunks + jnp.minimum(f, chunks - 1), 0, 0)
    old_chunk = lambda g, f: (old_tile(g) * chunks + jnp.minimum(f, chunks - 1), 0, 0)
    out_chunk = lambda g, f: (jnp.where(g < 2, 0, old_chunk(g, f)[0]), 0, 0)
    w_step = lambda g, f: jnp.where(g == 0, 0, jnp.where(g == n_tiles + 1, n_f - 1, f))
    x_chunks = x.reshape(n_tiles * chunks, chunk, d)
    out = pl.pallas_call(
        functools.partial(_ffn_body, n_tiles=n_tiles, n_f=n_f),
        grid=(n_tiles + 2, n_f),
        in_specs=[
            pl.BlockSpec((None, chunk, d), new_chunk),
            pl.BlockSpec((None, chunk, d), old_chunk),
            pl.BlockSpec((None, None, None, 3, d), lambda g, f: (layer, new_tile(g) // per_batch, sub, 0, 0)),
            pl.BlockSpec((None, None, None, 3, d), lambda g, f: (layer, old_tile(g) // per_batch, sub, 0, 0)),
            pl.BlockSpec((None, None, 1, d), lambda g, f: (layer, sub, 0, 0)),
            pl.BlockSpec((None, None, 1, d), lambda g, f: (layer, sub, 0, 0)),
            pl.BlockSpec((None, None, d, FFN_TILE), lambda g, f: (layer, idx, 0, w_step(g, f))),
            pl.BlockSpec((None, None, d, FFN_TILE), lambda g, f: (layer, idx, 0, w_step(g, f))),
            pl.BlockSpec((None, None, FFN_TILE, d), lambda g, f: (layer, idx, w_step(g, f), 0)),
        ],
        out_specs=pl.BlockSpec((None, chunk, d), out_chunk),
        out_shape=jax.ShapeDtypeStruct(x_chunks.shape, F32),
        scratch_shapes=[pltpu.VMEM((FFN_ROWS, d), BF16), pltpu.VMEM((FFN_ROWS, d), BF16),
                        pltpu.VMEM((FFN_ROWS, d), F32), pltpu.VMEM((FFN_ROWS, d), F32)],
        compiler_params=_params("arbitrary", "arbitrary"),
        name="ffn_sublayer",
    )(x_chunks, x_chunks, mod, mod, g_pre, g_post, w_gate, w_up, w_down)
    return out.reshape(b, s, d)


def _gelu_tanh(x):
    return 0.5 * x * (1.0 + jnp.tanh(0.7978845608028654 * (x + 0.044715 * (x * x * x))))


def _in_proj_body(*refs, hyb, n_a, chunks):
    if hyb:
        xn_ref, modn_ref, gpre_ref, wa_ref, wb_ref, wdt_ref, o_ref, dt_ref, h0_ref, h1_ref = refs
    else:
        xn_ref, modn_ref, gpre_ref, wa_ref, bias_ref, o_ref, h0_ref, h1_ref = refs
    g = pl.program_id(0)
    j = pl.program_id(1)
    chunk = PROJ_ROWS // chunks
    c0 = jnp.minimum(j, chunks - 1) * chunk

    @pl.when((g == 0) & (j == 0))
    def _():
        h0_ref[...] = jnp.zeros_like(h0_ref)
        h1_ref[...] = jnp.zeros_like(h1_ref)

    def pre_norm_chunk(h_ref):
        gain = gpre_ref[...] * (1.0 + modn_ref[1:2, :])
        done = []
        for q in range(chunk // FFN_NORM_ROWS):
            rows = pl.ds(q * FFN_NORM_ROWS, FFN_NORM_ROWS)
            tile_rows = pl.ds(pl.multiple_of(c0 + q * FFN_NORM_ROWS, FFN_NORM_ROWS), FFN_NORM_ROWS)
            res = _rms(xn_ref[rows, :]) * gain + modn_ref[0:1, :]
            h_ref[tile_rows, :] = res.astype(BF16)
            done.append(res)
        return done

    def project(h_ref, w_ref, h_new):
        h = h_ref[...]
        later = h + _zero_after(pre_norm_chunk(h_new), j < 0, h.shape[1]).astype(BF16)
        for t in range(o_ref.shape[1] // PROJ_BLOCK):
            cols = slice(t * PROJ_BLOCK, (t + 1) * PROJ_BLOCK)
            y = _dot(h if t == 0 else later, w_ref[:, cols])
            if not hyb:
                y = _gelu_tanh(y + bias_ref[:, cols])
            o_ref[:, cols] = y.astype(o_ref.dtype)

    has_matmul = g >= 1
    for parity, (h_new, h_mid) in enumerate([(h0_ref, h1_ref), (h1_ref, h0_ref)]):
        mine = lax.rem(g, 2) == parity
        if hyb:
            @pl.when(mine & has_matmul & (j == 0))
            def _():
                dt_ref[...] = _dot(h_mid[...], wdt_ref[...])

            @pl.when(mine & has_matmul & (j < n_a))
            def _():
                project(h_mid, wa_ref, h_new)

            @pl.when(mine & has_matmul & (j >= n_a))
            def _():
                project(h_mid, wb_ref, h_new)
        else:
            @pl.when(mine & has_matmul)
            def _():
                project(h_mid, wa_ref, h_new)

        @pl.when(mine & jnp.logical_not(has_matmul) & (j < chunks))
        def _():
            pre_norm_chunk(h_new)


def _in_proj(x, mod, g_pre, weights, layer, sub, hyb, tile):
    b, s, d = x.shape
    per_batch = s // PROJ_ROWS
    n_tiles = b * per_batch
    w_a = weights[0]
    n_a = w_a.shape[1] // tile
    n_steps = n_a + (weights[1].shape[1] // tile if hyb else 0)
    chunks = min(PROJ_NORM_CHUNKS, n_steps)
    chunk = PROJ_ROWS // chunks
    new_tile = lambda g: jnp.minimum(g, n_tiles - 1)
    mid_tile = lambda g: jnp.maximum(g - 1, 0)
    col = lambda g, j: jnp.where(g == 0, 0, j)
    in_specs = [
        pl.BlockSpec((None, chunk, d), lambda g, j: (new_tile(g) * chunks + jnp.minimum(j, chunks - 1), 0, 0)),
        pl.BlockSpec((None, None, None, 3, d), lambda g, j: (layer, new_tile(g) // per_batch, sub, 0, 0)),
        pl.BlockSpec((None, None, 1, d), lambda g, j: (layer, sub, 0, 0)),
        pl.BlockSpec((d, tile), lambda g, j: (0, jnp.minimum(col(g, j), n_a - 1))),
    ]
    out_specs = [pl.BlockSpec((None, PROJ_ROWS, tile), lambda g, j: (mid_tile(g), 0, col(g, j)))]
    out_shape = [jax.ShapeDtypeStruct((n_tiles, PROJ_ROWS, n_steps * tile), BF16)]
    if hyb:
        in_specs += [
            pl.BlockSpec((d, tile), lambda g, j: (0, jnp.maximum(col(g, j) - n_a, 0))),
            pl.BlockSpec((d, LANES), lambda g, j: (0, 0)),
        ]
        out_specs.append(pl.BlockSpec((None, PROJ_ROWS, LANES), lambda g, j: (mid_tile(g), 0, 0)))
        out_shape.append(jax.ShapeDtypeStruct((n_tiles, PROJ_ROWS, LANES), F32))
    else:
        in_specs.append(pl.BlockSpec((1, tile), lambda g, j: (0, col(g, j))))
    outs = pl.pallas_call(
        functools.partial(_in_proj_body, hyb=hyb, n_a=n_a, chunks=chunks),
        grid=(n_tiles + 1, n_steps),
        in_specs=in_specs,
        out_specs=out_specs,
        out_shape=out_shape,
        scratch_shapes=[pltpu.VMEM((PROJ_ROWS, d), BF16)] * 2,
        compiler_params=_params("arbitrary", "arbitrary"),
        name="in_proj",
    )(x.reshape(n_tiles * chunks, chunk, d), mod, g_pre, *weights)
    return [o.reshape(b, s, o.shape[-1]) for o in outs]


def _sgu_gate_chunk(u, v, lng_ref, lnb_ref, ws_ref, bs_ref):
    L = SGU_CHUNK
    v = v.astype(F32)
    mu = jnp.mean(v, axis=-1, keepdims=True)
    vc = v - mu
    var = jnp.mean(vc * vc, axis=-1, keepdims=True)
    vn = (vc * lax.rsqrt(var + NORM_EPS) * lng_ref[...] + lnb_ref[...]).astype(BF16)
    row = lax.broadcasted_iota(jnp.int32, (L, L), 0)
    col = lax.broadcasted_iota(jnp.int32, (L, L), 1)
    causal = row >= col
    parts = []
    for g in range(SGU_GROUPS):
        gs = slice(g * SGU_GROUP_WIDTH, (g + 1) * SGU_GROUP_WIDTH)
        w = jnp.where(causal, ws_ref[g], 0.0).astype(BF16)
        mixed = _dot(w, vn[:, gs]) + bs_ref[:, g:g + 1]
        parts.append((u[:, gs].astype(F32) * mixed).astype(BF16))
    return jnp.concatenate(parts, axis=1)


def _mixer_out_body(*refs, sgu, n_tiles):
    if sgu:
        (u_ref, v_ref, lng_ref, lnb_ref, ws_ref, bs_ref, w_ref, xp_ref, modp_ref, gpost_ref, o_ref,
         lhs0_ref, lhs1_ref, acc0_ref, acc1_ref) = refs
    else:
        a_ref, b_ref, w_ref, xp_ref, modp_ref, gpost_ref, o_ref, acc0_ref, acc1_ref = refs
        lhs0_ref = lhs1_ref = None
    g = pl.program_id(0)
    k = pl.program_id(1)
    chunk = MIX_ROWS // MIX_STEPS
    half = w_ref.shape[0] // MIX_STEPS

    @pl.when((g == 0) & (k == 0))
    def _():
        for ref in (lhs0_ref, lhs1_ref, acc0_ref, acc1_ref):
            if ref is not None:
                ref[...] = jnp.zeros_like(ref)

    def post_norm_chunk(acc_ref):
        gain = (MIXER_RES_WEIGHT * (1.0 + modp_ref[2:3, :])) * gpost_ref[...]
        done = []
        for q in range(chunk // FFN_NORM_ROWS):
            rows = pl.ds(q * FFN_NORM_ROWS, FFN_NORM_ROWS)
            tile_rows = pl.ds(pl.multiple_of(k * chunk + q * FFN_NORM_ROWS, FFN_NORM_ROWS), FFN_NORM_ROWS)
            res = xp_ref[rows, :] + _rms(acc_ref[tile_rows, :]) * gain
            o_ref[rows, :] = res
            done.append(res)
        return done

    def gate_chunk(lhs_ref):
        for j in range(chunk // SGU_CHUNK):
            rows = slice(j * SGU_CHUNK, (j + 1) * SGU_CHUNK)
            gated = _sgu_gate_chunk(u_ref[rows, :], v_ref[rows, :], lng_ref, lnb_ref, ws_ref, bs_ref)
            tile_rows = pl.ds(pl.multiple_of(k * chunk + j * SGU_CHUNK, SGU_CHUNK), SGU_CHUNK)
            for part in range(MIX_STEPS):
                lhs_ref[part, tile_rows, :] = gated[:, part * half:(part + 1) * half]

    def matmul_step(lhs_ref, acc_ref, done):
        lhs = lhs_ref[k] if sgu else jnp.where(k == 0, a_ref[...], b_ref[...])
        later = lhs + _zero_after(done, k < 0, half).astype(BF16)
        rows = pl.ds(pl.multiple_of(k * half, half), half)
        mid = acc_ref.shape[1] // 2
        for cols, operand in ((slice(0, mid), lhs), (slice(mid, 2 * mid), later)):
            acc_ref[:, cols] = jnp.where(k > 0, acc_ref[:, cols], 0.0) + _dot(operand, w_ref[rows, cols])

    has_matmul = (g >= 1) & (g <= n_tiles)
    for parity, (lhs_new, acc_old, lhs_mid, acc_mid) in enumerate(
            [(lhs0_ref, acc0_ref, lhs1_ref, acc1_ref), (lhs1_ref, acc1_ref, lhs0_ref, acc0_ref)]):
        @pl.when((lax.rem(g, 2) == parity) & has_matmul)
        def _():
            matmul_step(lhs_mid, acc_mid, post_norm_chunk(acc_old))
            if sgu:
                gate_chunk(lhs_new)

        if sgu:
            @pl.when((lax.rem(g, 2) == parity) & (g == 0))
            def _():
                gate_chunk(lhs_new)

        @pl.when((lax.rem(g, 2) == parity) & (g == n_tiles + 1))
        def _():
            post_norm_chunk(acc_old)


def _mixer_out(operands, w, x, mod, g_post, layer, sub, sgu):
    b, s, d = x.shape
    half = w.shape[0] // MIX_STEPS
    per_batch = s // MIX_ROWS
    n_tiles = b * per_batch
    steps = MIX_STEPS
    chunk = MIX_ROWS // steps
    new_tile = lambda g: jnp.minimum(g, n_tiles - 1)
    mid_tile = lambda g: jnp.clip(g - 1, 0, n_tiles - 1)
    old_tile = lambda g: jnp.clip(g - 2, 0, n_tiles - 1)
    new_chunk = lambda g, k: new_tile(g) * steps + k
    old_chunk = lambda g, k: old_tile(g) * steps + k
    out_chunk = lambda g, k: (jnp.where(g < 2, 0, old_chunk(g, k)), 0, 0)
    x_chunks = x.reshape(n_tiles * steps, chunk, d)
    common_specs = [
        pl.BlockSpec((steps * half, d), lambda g, k: (0, 0), pipeline_mode=pl.Buffered(1)),
        pl.BlockSpec((None, chunk, d), lambda g, k: (old_chunk(g, k), 0, 0)),
        pl.BlockSpec((None, None, None, 3, d), lambda g, k: (layer, old_tile(g) // per_batch, sub, 0, 0)),
        pl.BlockSpec((None, None, 1, d), lambda g, k: (layer, sub, 0, 0)),
    ]
    acc = [pltpu.VMEM((MIX_ROWS, d), F32)] * 2
    if sgu:
        zz, ln_g, ln_b, w_spatial, b_spatial = operands
        zz_chunks = zz.reshape(n_tiles * steps, chunk, 2 * SGU_WIDTH)
        vec = pl.BlockSpec((1, SGU_WIDTH), lambda g, k: (0, 0))
        lhs_specs = [
            pl.BlockSpec((None, chunk, SGU_WIDTH), lambda g, k: (new_chunk(g, k), 0, 0)),
            pl.BlockSpec((None, chunk, SGU_WIDTH), lambda g, k: (new_chunk(g, k), 0, 1)),
            vec, vec,
            pl.BlockSpec((SGU_GROUPS, SGU_CHUNK, SGU_CHUNK), lambda g, k: (0, 0, 0)),
            pl.BlockSpec((SGU_CHUNK, SGU_GROUPS), lambda g, k: (0, 0)),
        ]
        lhs_args = [zz_chunks, zz_chunks, ln_g.reshape(1, SGU_WIDTH), ln_b.reshape(1, SGU_WIDTH), w_spatial,
                    b_spatial.T]
        scratch = [pltpu.VMEM((steps, MIX_ROWS, half), BF16)] * 2 + acc
    else:
        y_a, y_b = operands
        tile = pl.BlockSpec((None, MIX_ROWS, half), lambda g, k: (mid_tile(g), 0, 0))
        lhs_specs = [tile, tile]
        lhs_args = [y_a.reshape(n_tiles, MIX_ROWS, half), y_b.reshape(n_tiles, MIX_ROWS, half)]
        scratch = acc
    out = pl.pallas_call(
        functools.partial(_mixer_out_body, sgu=sgu, n_tiles=n_tiles),
        grid=(n_tiles + 2, steps),
        in_specs=lhs_specs + common_specs,
        out_specs=pl.BlockSpec((None, chunk, d), out_chunk),
        out_shape=jax.ShapeDtypeStruct(x_chunks.shape, F32),
        scratch_shapes=scratch,
        compiler_params=_params("arbitrary", "arbitrary"),
        name="mixer_out",
    )(*lhs_args, w, x_chunks, mod, g_post)
    return out.reshape(b, s, d)


def _rope_body(pos_ref, freq_ref, cos_ref, sin_ref):
    ang = pos_ref[...].astype(F32) * freq_ref[...]
    lane = lax.broadcasted_iota(jnp.int32, ang.shape, 1)
    cos_ref[...] = jnp.cos(ang)
    sin_ref[...] = jnp.where(lane < ATT_HEAD_DIM // 2, -1.0, 1.0) * jnp.sin(ang)


def _rope_tables(positions):
    b, s = positions.shape
    half = ATT_HEAD_DIM // 2
    inv_freq = ROPE_THETA ** (-jnp.arange(half, dtype=F32) / half)
    freq = jnp.concatenate([inv_freq, inv_freq]).reshape(1, ATT_HEAD_DIM)
    tile = 512
    spec = pl.BlockSpec((None, tile, ATT_HEAD_DIM), lambda bi, i: (bi, i, 0))
    return pl.pallas_call(
        _rope_body,
        grid=(b, s // tile),
        in_specs=[
            pl.BlockSpec((None, tile, 1), lambda bi, i: (bi, i, 0)),
            pl.BlockSpec((1, ATT_HEAD_DIM), lambda bi, i: (0, 0)),
        ],
        out_specs=[spec, spec],
        out_shape=[jax.ShapeDtypeStruct((b, s, ATT_HEAD_DIM), F32)] * 2,
        compiler_params=_params("parallel", "parallel"),
        name="rope_tables",
    )(positions.reshape(b, s, 1), freq)


def _ssd_body(xs_ref, bm_ref, cm_ref, z_ref, dt_ref, cw_ref, cb_ref, dtb_ref, alog_ref, dskip_ref, ng_ref,
              o_ref, tail_ref, state_ref):
    L = SSD_CHUNK
    T = SSD_TAIL

    @pl.when(pl.program_id(1) == 0)
    def _():
        tail_ref[...] = jnp.zeros_like(tail_ref)
        state_ref[...] = jnp.zeros_like(state_ref)

    srow = lax.broadcasted_iota(jnp.int32, ((SSD_CONV - 1) * L, L + T), 0)
    scol = lax.broadcasted_iota(jnp.int32, ((SSD_CONV - 1) * L, L + T), 1)
    lag = (srow >> (L.bit_length() - 1)) + 1
    t_in = srow & (L - 1)
    shift = (scol == jnp.where(t_in >= lag, t_in - lag, t_in - lag + (L + T))).astype(BF16)

    def conv_silu(raw_ref, lo, hi):
        raw = raw_ref[...]
        lagged = _dot(shift, jnp.concatenate([raw, tail_ref[:, lo:hi]], axis=0))
        acc = cb_ref[:, lo:hi] + raw.astype(F32) * cw_ref[SSD_CONV - 1:SSD_CONV, lo:hi]
        for k in range(1, SSD_CONV):
            acc = acc + lagged[(k - 1) * L:k * L] * cw_ref[SSD_CONV - 1 - k:SSD_CONV - k, lo:hi]
        tail_ref[:, lo:hi] = raw[L - T:L]
        return _silu(acc)

    xs = conv_silu(xs_ref, 0, SSD_WIDTH)
    bm = conv_silu(bm_ref, SSD_WIDTH, SSD_WIDTH + SSD_BC_WIDTH)
    cm = conv_silu(cm_ref, SSD_WIDTH + SSD_BC_WIDTH, SSD_CONV_CH)

    dt_in = dt_ref[...] + dtb_ref[...]
    dt = jnp.maximum(dt_in, 0.0) + jnp.log1p(jnp.exp(-jnp.abs(dt_in)))
    adt = dt * (-jnp.exp(alog_ref[...]))
    row = lax.broadcasted_iota(jnp.int32, (L, L), 0)
    col = lax.broadcasted_iota(jnp.int32, (L, L), 1)
    causal = row >= col
    acs = _dot_exact(causal.astype(F32), adt)
    acs_t = acs.T
    dt_t = dt.T
    acs_last = acs[L - 1:L, :]

    hrow = lax.broadcasted_iota(jnp.int32, (LANES, SSD_WIDTH), 0)
    hcol = lax.broadcasted_iota(jnp.int32, (LANES, SSD_WIDTH), 1)
    expand = (hrow == (hcol >> (SSD_HEAD_DIM.bit_length() - 1))).astype(BF16)
    stacked = jnp.concatenate([jnp.exp(acs), jnp.exp(acs_last - acs) * dt], axis=0)
    high = stacked.astype(BF16)
    rest = (stacked - high.astype(F32)).astype(BF16)
    wide = _dot(high, expand) + _dot(rest, expand)
    decay_in_w, dt_decay_out_w = wide[0:L], wide[L:2 * L]
    chunk_decay_w = decay_in_w[L - 1:L, :]

    lane = lax.broadcasted_iota(jnp.int32, (L, LANES), 1)
    first_head = lane < SSD_HEAD_DIM

    y_parts = []
    for g in range(SSD_GROUPS):
        gs = slice(g * SSD_GROUP_WIDTH, (g + 1) * SSD_GROUP_WIDTH)
        bg = bm[:, g * SSD_STATE:(g + 1) * SSD_STATE]
        cg = cm[:, g * SSD_STATE:(g + 1) * SSD_STATE].astype(BF16)
        cb = _dot_nt(cg, bg.astype(BF16))
        state = state_ref[g]
        y_off = _dot(cg, state.astype(BF16)) * decay_in_w[:, gs]
        y_diag = []
        for j in range(SSD_HEADS_PER_GROUP // 2):
            h0 = g * SSD_HEADS_PER_GROUP + 2 * j
            ms = []
            for h in (h0, h0 + 1):
                seg = acs[:, h:h + 1] - acs_t[h:h + 1, :]
                ms.append(cb * jnp.exp(jnp.where(causal, seg, -jnp.inf)) * dt_t[h:h + 1, :])
            lhs = jnp.concatenate(ms, axis=1).astype(BF16)
            xp = xs[:, h0 * SSD_HEAD_DIM:(h0 + 2) * SSD_HEAD_DIM]
            rhs = jnp.concatenate([jnp.where(first_head, xp, 0.0), jnp.where(first_head, 0.0, xp)], axis=0)
            y_diag.append(_dot(lhs, rhs.astype(BF16)))
        y_parts.append(jnp.concatenate(y_diag, axis=1) + y_off)
        contrib = _dot(bg.T.astype(BF16), (xs[:, gs] * dt_decay_out_w[:, gs]).astype(BF16))
        state_ref[g] = state * chunk_decay_w[:, gs] + contrib

    y = jnp.concatenate(y_parts, axis=1) + xs * dskip_ref[...]
    y = y * _silu(z_ref[...].astype(F32))
    o_ref[...] = (_rms(y) * ng_ref[...]).astype(o_ref.dtype)


def _ssd_mixer(proj, dt_raw, conv_w, conv_b, dt_bias, a_log, d_skip, norm_g):
    b, s, _ = proj.shape
    assert proj.dtype == BF16
    L = SSD_CHUNK
    z_blk = 0
    xs_blk = SSD_WIDTH // SSD_WIDTH
    bm_blk = (2 * SSD_WIDTH) // SSD_BC_WIDTH
    cm_blk = bm_blk + 1
    pad = LANES - SSD_HEADS
    small = lambda a: pl.BlockSpec(a.shape, lambda bi, c: (0, 0))
    dt_bias_p = jnp.pad(dt_bias, (0, pad)).reshape(1, LANES)
    a_log_p = jnp.pad(a_log, (0, pad)).reshape(1, LANES)
    d_skip_w = jnp.repeat(d_skip, SSD_HEAD_DIM).reshape(1, SSD_WIDTH)
    conv_b2 = conv_b.reshape(1, SSD_CONV_CH)
    norm_g2 = norm_g.reshape(1, SSD_WIDTH)
    return pl.pallas_call(
        _ssd_body,
        grid=(b, s // L),
        in_specs=[
            pl.BlockSpec((None, L, SSD_WIDTH), lambda bi, c: (bi, c, xs_blk)),
            pl.BlockSpec((None, L, SSD_BC_WIDTH), lambda bi, c: (bi, c, bm_blk)),
            pl.BlockSpec((None, L, SSD_BC_WIDTH), lambda bi, c: (bi, c, cm_blk)),
            pl.BlockSpec((None, L, SSD_WIDTH), lambda bi, c: (bi, c, z_blk)),
            pl.BlockSpec((None, L, LANES), lambda bi, c: (bi, c, 0)),
            small(conv_w), small(conv_b2), small(dt_bias_p), small(a_log_p), small(d_skip_w), small(norm_g2),
        ],
        out_specs=pl.BlockSpec((None, L, SSD_WIDTH), lambda bi, c: (bi, c, 0)),
        out_shape=jax.ShapeDtypeStruct((b, s, SSD_WIDTH), BF16),
        scratch_shapes=[
            pltpu.VMEM((SSD_TAIL, SSD_CONV_CH), BF16),
            pltpu.VMEM((SSD_GROUPS, SSD_STATE, SSD_GROUP_WIDTH), F32),
        ],
        compiler_params=_params("parallel", "arbitrary"),
        name="ssd_mixer",
    )(proj, proj, proj, proj, dt_raw, conv_w, conv_b2, dt_bias_p, a_log_p, d_skip_w, norm_g2)


def _attn_body(q_ref, k_ref, v_ref, cos_ref, sin_ref, o_ref, qkv_ref, acc_ref, m_ref, l_ref, *, seq):
    blk = ATT_BLOCK
    half = ATT_HEAD_DIM // 2
    dils = [d for _, d in DILATED_PATTERNS]
    step = dils[1]
    assert dils == [1, step, step * step] and all(w // d == blk for w, d in DILATED_PATTERNS)
    sub = seq // step
    assert seq // dils[2] == blk

    cos = cos_ref[...]
    sin = sin_ref[...]
    q = q_ref[...].astype(F32)
    k = k_ref[...].astype(F32)
    qkv_ref[0, 0] = (q * cos + pltpu.roll(q, half, 1) * sin) * (ATT_HEAD_DIM ** -0.5 * LOG2_E)
    qkv_ref[0, 1] = k * cos + pltpu.roll(k, half, 1) * sin
    qkv_ref[0, 2] = v_ref[...].astype(F32)
    for t in range(3):
        for r in range(step):
            qkv_ref[1, t, pl.ds(r * sub, sub), :] = qkv_ref[0, t, pl.ds(r, sub, stride=step), :]
    for t in range(3):
        for r in range(step):
            for a in range(step):
                qkv_ref[2, t, pl.ds((r + step * a) * blk, blk), :] = qkv_ref[1, t, pl.ds(r * sub + a, blk, stride=step), :]

    row = lax.broadcasted_iota(jnp.int32, (blk, blk), 0)
    col = lax.broadcasted_iota(jnp.int32, (blk, blk), 1)
    cur_ok = col <= row
    prev_ok = col >= row

    def attend(p, starts, with_prev):
        def load(t, st):
            return qkv_ref[p, t, pl.ds(st, blk), :].astype(BF16)

        def window(t, st):
            return jnp.concatenate([load(t, st - blk), load(t, st)], axis=0) if with_prev else load(t, st)

        qb = jnp.stack([load(0, st) for st in starts])
        kk = jnp.stack([window(1, st) for st in starts])
        keys = kk.shape[1]
        vv = jnp.stack([jnp.concatenate([window(2, st), jnp.ones((keys, LANES), BF16)], axis=1) for st in starts])
        ok = jnp.concatenate([prev_ok, cur_ok], axis=1) if with_prev else cur_ok
        s = lax.dot_general(qb, kk, (((2,), (2,)), ((0,), (0,))), preferred_element_type=F32)
        s = jnp.where(ok[None], s, -jnp.inf)
        m = jnp.max(s, axis=2, keepdims=True)
        e = jnp.exp2(s - m)
        acc = lax.dot_general(e.astype(BF16), vv, (((2,), (1,)), ((0,), (0,))), preferred_element_type=F32)
        for i, st in enumerate(starts):
            rows = pl.ds(st, blk)
            acc_ref[p, rows, :] = acc[i, :, 0:ATT_HEAD_DIM]
            l_ref[p, rows, :] = acc[i, :, ATT_HEAD_DIM:]
            m_ref[p, rows, :] = jnp.broadcast_to(m[i], (blk, LANES))

    def groups(starts, size):
        return [starts[i:i + size] for i in range(0, len(starts), size)]

    for p, dil in enumerate(dils):
        class_rows = seq // dil
        firsts = [r * class_rows for r in range(dil)]
        laters = [r * class_rows + n * blk for r in range(dil) for n in range(1, class_rows // blk)]
        for g in groups(firsts, ATT_GROUP_FIRST):
            attend(p, g, False)
        for g in groups(laters, ATT_GROUP_LATER):
            attend(p, g, True)

    def merged(dst, dst_rows, src, src_rows):
        m_a, m_b = m_ref[dst, dst_rows, :], m_ref[src, src_rows, :]
        top = jnp.maximum(m_a, m_b)
        w_a, w_b = jnp.exp2(m_a - top), jnp.exp2(m_b - top)
        acc = w_a * acc_ref[dst, dst_rows, :] + w_b * acc_ref[src, src_rows, :]
        return top, acc, w_a * l_ref[dst, dst_rows, :] + w_b * l_ref[src, src_rows, :]

    for r in range(step):
        for a in range(step):
            mid_rows = pl.ds(r * sub + a, blk, stride=step)
            top, acc, l = merged(1, mid_rows, 2, pl.ds((r + step * a) * blk, blk))
            m_ref[1, mid_rows, :] = top
            acc_ref[1, mid_rows, :] = acc
            l_ref[1, mid_rows, :] = l
    for r in range(step):
        for n in range(sub // blk):
            nat_rows = pl.ds(r + n * blk * step, blk, stride=step)
            _, acc, l = merged(0, nat_rows, 1, pl.ds(r * sub + n * blk, blk))
            acc_ref[0, nat_rows, :] = acc / l
    o_ref[...] = acc_ref[0].astype(o_ref.dtype)


def _dilated_attention(proj, cos2, sin2):
    b, s, _ = proj.shape
    q_blk = (2 * SSD_WIDTH + 2 * SSD_BC_WIDTH) // ATT_HEAD_DIM
    k_blk = q_blk + ATT_HEADS
    v_blk = k_blk + ATT_HEADS
    n_pat = len(DILATED_PATTERNS)
    head = lambda base: pl.BlockSpec((None, s, ATT_HEAD_DIM), lambda bi, h: (bi, 0, base + h))
    table = pl.BlockSpec((None, s, ATT_HEAD_DIM), lambda bi, h: (bi, 0, 0))
    return pl.pallas_call(
        functools.partial(_attn_body, seq=s),
        grid=(b, ATT_HEADS),
        in_specs=[head(q_blk), head(k_blk), head(v_blk), table, table],
        out_specs=pl.BlockSpec((None, s, ATT_HEAD_DIM), lambda bi, h: (bi, 0, h)),
        out_shape=jax.ShapeDtypeStruct((b, s, ATT_WIDTH), BF16),
        scratch_shapes=[
            pltpu.VMEM((n_pat, 3, s, ATT_HEAD_DIM), F32),
            pltpu.VMEM((n_pat, s, ATT_HEAD_DIM), F32),
            pltpu.VMEM((n_pat, s, LANES), F32),
            pltpu.VMEM((n_pat, s, LANES), F32),
        ],
        compiler_params=_params("parallel", "arbitrary"),
        name="dilated_attention",
    )(proj, proj, proj, cos2, sin2)


def kernel(x, c, positions, w_mod, b_mod, norm_pre, norm_post, ffn_w_gate, ffn_w_up, ffn_w_down, hyb_w_in, hyb_conv_w, hyb_conv_b, hyb_dt_bias, hyb_a_log, hyb_d_skip, hyb_norm_g, hyb_w_out, sgu_w_in, sgu_b_in, sgu_ln_g, sgu_ln_b, sgu_w_spatial, sgu_b_spatial, sgu_w_out):
    depth = w_mod.shape[0]
    b, s, d = x.shape
    n_sub = norm_pre.shape[1]

    mod = _modulation(c, w_mod, b_mod).reshape(depth, b, n_sub, 3, d)
    g_pre = norm_pre.reshape(depth, n_sub, 1, d)
    g_post = norm_post.reshape(depth, n_sub, 1, d)
    w_gate = ffn_w_gate.astype(BF16)
    w_up = ffn_w_up.astype(BF16)
    w_down = ffn_w_down.astype(BF16)

    for layer in range(depth):
        i = layer // 2
        x = _ffn_sublayer(x, mod, g_pre, g_post, w_gate, w_up, w_down, layer, 0, 0)
        if layer % 2 == 0:
            w_in = hyb_w_in[i]
            dt_lo = SSD_WIDTH + SSD_CONV_CH
            dt_hi = dt_lo + SSD_HEADS
            w_dt = jnp.pad(w_in[:, dt_lo:dt_hi], ((0, 0), (0, LANES - SSD_HEADS))).astype(BF16)
            proj, dt_raw = _in_proj(x, mod, g_pre, (w_in[:, :dt_lo].astype(BF16), w_in[:, dt_hi:].astype(BF16), w_dt),
                                    layer, 1, hyb=True, tile=PROJ_TILE)
            cos2, sin2 = _rope_tables(positions)
            y_a = _ssd_mixer(proj, dt_raw, hyb_conv_w[i], hyb_conv_b[i], hyb_dt_bias[i], hyb_a_log[i],
                             hyb_d_skip[i], hyb_norm_g[i])
            y_b = _dilated_attention(proj, cos2, sin2)
            x = _mixer_out((y_a, y_b), hyb_w_out[i].astype(BF16), x, mod, g_post, layer, 1, sgu=False)
        else:
            zz, = _in_proj(x, mod, g_pre, (sgu_w_in[i].astype(BF16), sgu_b_in[i].reshape(1, -1)), layer, 1,
                           hyb=False, tile=2 * PROJ_TILE)
            x = _mixer_out((zz, sgu_ln_g[i], sgu_ln_b[i], sgu_w_spatial[i], sgu_b_spatial[i]),
                           sgu_w_out[i].astype(BF16), x, mod, g_post, layer, 1, sgu=True)
        x = _ffn_sublayer(x, mod, g_pre, g_post, w_gate, w_up, w_down, layer, 2, 1)
    return x
```

```python
import functools

import jax
import jax.numpy as jnp
from jax import lax
from jax.experimental import pallas as pl
from jax.experimental.pallas import tpu as pltpu

NORM_EPS = 1e-6
LOG2_E = 1.4426950408889634
FFN_RES_WEIGHT = 0.5
MIXER_RES_WEIGHT = 1.0

SSD_HEADS = 32
SSD_HEAD_DIM = 64
SSD_WIDTH = SSD_HEADS * SSD_HEAD_DIM
SSD_GROUPS = 4
SSD_STATE = 128
SSD_CONV = 4
SSD_CHUNK = 128
SSD_TAIL = 16
SSD_BC_WIDTH = SSD_GROUPS * SSD_STATE
SSD_CONV_CH = SSD_WIDTH + 2 * SSD_BC_WIDTH
SSD_HEADS_PER_GROUP = SSD_HEADS // SSD_GROUPS
SSD_GROUP_WIDTH = SSD_HEADS_PER_GROUP * SSD_HEAD_DIM

ATT_HEADS = 16
ATT_HEAD_DIM = 128
ATT_WIDTH = ATT_HEADS * ATT_HEAD_DIM
ATT_BLOCK = 128
DILATED_PATTERNS = ((128, 1), (512, 4), (2048, 16))
ROPE_THETA = 10000.0

SGU_WIDTH = 4096
SGU_GROUPS = 8
SGU_CHUNK = 128
SGU_GROUP_WIDTH = SGU_WIDTH // SGU_GROUPS

LANES = 128
SUBLANES = 8
VMEM_LIMIT_BYTES = 56 * 1024 * 1024

MIX_ROWS = 512
MIX_STEPS = 2
FFN_ROWS = 1024
FFN_TILE = 512
FFN_NORM_CHUNKS = 8
FFN_NORM_ROWS = 16
PROJ_ROWS_HYB = 1024
PROJ_ROWS_GELU = 1024
PROJ_TILE = 1024
PROJ_BLOCK_HYB = 512
PROJ_BLOCK_GELU = 256
PROJ_NORM_CHUNKS = 8
MOD_TILE = 1024
ATT_GROUP_FIRST = 16
ATT_GROUP_LATER = 15

BF16 = jnp.bfloat16
F32 = jnp.float32


def _params(*semantics):
    return pltpu.CompilerParams(dimension_semantics=semantics, vmem_limit_bytes=VMEM_LIMIT_BYTES)


def _rms(x):
    return x * lax.rsqrt(jnp.mean(x * x, axis=-1, keepdims=True) + NORM_EPS)


def _silu(x):
    h = 0.5 * x
    return h + h * jnp.tanh(h)


def _dot(a, b):
    return jnp.dot(a, b, preferred_element_type=F32)


def _dot_exact(a, b):
    return jnp.dot(a, b, preferred_element_type=F32, precision=lax.Precision.HIGHEST)


def _zero_after(done, never, width):
    total = jnp.zeros((SUBLANES, LANES), F32)
    for res in done:
        for r in range(0, res.shape[0], SUBLANES):
            for c in range(0, res.shape[1], LANES):
                total = total + res[r:r + SUBLANES, c:c + LANES]
    row = jnp.sum(jnp.where(never, total, 0.0), axis=0, keepdims=True)
    return jnp.tile(row, (1, width // LANES))


def _dot_nt(a, b):
    return lax.dot_general(a, b, (((1,), (1,)), ((), ())), preferred_element_type=F32)


def _mod_body(c_ref, w_ref, b_ref, o_ref):
    ca = _silu(c_ref[...]).astype(BF16)
    o_ref[...] = _dot(ca, w_ref[...].astype(BF16)) + b_ref[...]


def _modulation(c, w_mod, b_mod):
    depth, d, n = w_mod.shape
    b = c.shape[0]
    return pl.pallas_call(
        _mod_body,
        grid=(depth, n // MOD_TILE),
        in_specs=[
            pl.BlockSpec((b, d), lambda l, j: (0, 0)),
            pl.BlockSpec((None, d, MOD_TILE), lambda l, j: (l, 0, j)),
            pl.BlockSpec((None, 1, MOD_TILE), lambda l, j: (l, 0, j)),
        ],
        out_specs=pl.BlockSpec((None, b, MOD_TILE), lambda l, j: (l, 0, j)),
        out_shape=jax.ShapeDtypeStruct((depth, b, n), F32),
        compiler_params=_params("parallel", "parallel"),
        name="modulation",
    )(c, w_mod, b_mod.reshape(depth, 1, n))


def _ffn_body(xn_ref, xp_ref, modn_ref, modp_ref, gpre_ref, gpost_ref, wg_ref, wu_ref, wd_ref, o_ref,
              h0_ref, h1_ref, acc0_ref, acc1_ref, *, n_tiles, n_f):
    g = pl.program_id(0)
    f = pl.program_id(1)
    chunk = FFN_ROWS // FFN_NORM_CHUNKS
    c0 = jnp.minimum(f, FFN_NORM_CHUNKS - 1) * chunk

    @pl.when((g == 0) & (f == 0))
    def _():
        for ref in (h0_ref, h1_ref, acc0_ref, acc1_ref):
            ref[...] = jnp.zeros_like(ref)

    def pre_norm_chunk(h_ref):
        gain = gpre_ref[...] * (1.0 + modn_ref[1:2, :])
        done = []
        for q in range(chunk // FFN_NORM_ROWS):
            rows = pl.ds(q * FFN_NORM_ROWS, FFN_NORM_ROWS)
            tile_rows = pl.ds(pl.multiple_of(c0 + q * FFN_NORM_ROWS, FFN_NORM_ROWS), FFN_NORM_ROWS)
            res = _rms(xn_ref[rows, :]) * gain + modn_ref[0:1, :]
            h_ref[tile_rows, :] = res.astype(BF16)
            done.append(res)
        return done

    def post_norm_chunk(acc_ref):
        gain = (FFN_RES_WEIGHT * (1.0 + modp_ref[2:3, :])) * gpost_ref[...]
        done = []
        for q in range(chunk // FFN_NORM_ROWS):
            rows = pl.ds(q * FFN_NORM_ROWS, FFN_NORM_ROWS)
            tile_rows = pl.ds(pl.multiple_of(c0 + q * FFN_NORM_ROWS, FFN_NORM_ROWS), FFN_NORM_ROWS)
            res = xp_ref[rows, :] + _rms(acc_ref[tile_rows, :]) * gain
            o_ref[rows, :] = res
            done.append(res)
        return done

    def zero_after(done):
        return _zero_after(done, f < 0, FFN_TILE)

    def swiglu_step(h_ref, acc_ref, zero):
        h = h_ref[...]
        a = (_silu(_dot(h, wg_ref[...])) * _dot(h, wu_ref[...]) + zero).astype(BF16)
        acc_ref[...] = jnp.where(f > 0, acc_ref[...], 0.0) + _dot(a, wd_ref[...])

    has_matmul = (g >= 1) & (g <= n_tiles)
    for parity, (h_new, acc_old, h_mid, acc_mid) in enumerate(
            [(h0_ref, acc0_ref, h1_ref, acc1_ref), (h1_ref, acc1_ref, h0_ref, acc0_ref)]):
        @pl.when((lax.rem(g, 2) == parity) & has_matmul)
        def _():
            swiglu_step(h_mid, acc_mid, zero_after(post_norm_chunk(acc_old) + pre_norm_chunk(h_new)))

        @pl.when((lax.rem(g, 2) == parity) & (g == 0) & (f < FFN_NORM_CHUNKS))
        def _():
            pre_norm_chunk(h_new)

        @pl.when((lax.rem(g, 2) == parity) & (g == n_tiles + 1) & (f < FFN_NORM_CHUNKS))
        def _():
            post_norm_chunk(acc_old)


def _ffn_sublayer(x, mod, g_pre, g_post, w_gate, w_up, w_down, layer, sub, idx):
    b, s, d = x.shape
    f_dim = w_gate.shape[-1]
    n_f = f_dim // FFN_TILE
    per_batch = s // FFN_ROWS
    n_tiles = b * per_batch
    chunks = FFN_NORM_CHUNKS
    chunk = FFN_ROWS // chunks
    assert n_f >= chunks
    new_tile = lambda g: jnp.minimum(g, n_tiles - 1)
    old_tile = lambda g: jnp.clip(g - 2, 0, n_tiles - 1)
    new_chunk = lambda g, f: (new_tile(g) * chunks + jnp.minimum(f, chunks - 1), 0, 0)
    old_chunk = lambda g, f: (old_tile(g) * chunks + jnp.minimum(f, chunks - 1), 0, 0)
    out_chunk = lambda g, f: (jnp.where(g < 2, 0, old_chunk(g, f)[0]), 0, 0)
    w_step = lambda g, f: jnp.where(g == 0, 0, jnp.where(g == n_tiles + 1, n_f - 1, f))
    x_chunks = x.reshape(n_tiles * chunks, chunk, d)
    out = pl.pallas_call(
        functools.partial(_ffn_body, n_tiles=n_tiles, n_f=n_f),
        grid=(n_tiles + 2, n_f),
        in_specs=[
            pl.BlockSpec((None, chunk, d), new_chunk),
            pl.BlockSpec((None, chunk, d), old_chunk),
            pl.BlockSpec((None, None, None, 3, d), lambda g, f: (layer, new_tile(g) // per_batch, sub, 0, 0)),
            pl.BlockSpec((None, None, None, 3, d), lambda g, f: (layer, old_tile(g) // per_batch, sub, 0, 0)),
            pl.BlockSpec((None, None, 1, d), lambda g, f: (layer, sub, 0, 0)),
            pl.BlockSpec((None, None, 1, d), lambda g, f: (layer, sub, 0, 0)),
            pl.BlockSpec((None, None, d, FFN_TILE), lambda g, f: (layer, idx, 0, w_step(g, f))),
            pl.BlockSpec((None, None, d, FFN_TILE), lambda g, f: (layer, idx, 0, w_step(g, f))),
            pl.BlockSpec((None, None, FFN_TILE, d), lambda g, f: (layer, idx, w_step(g, f), 0)),
        ],
        out_specs=pl.BlockSpec((None, chunk, d), out_chunk),
        out_shape=jax.ShapeDtypeStruct(x_chunks.shape, F32),
        scratch_shapes=[pltpu.VMEM((FFN_ROWS, d), BF16), pltpu.VMEM((FFN_ROWS, d), BF16),
                        pltpu.VMEM((FFN_ROWS, d), F32), pltpu.VMEM((FFN_ROWS, d), F32)],
        compiler_params=_params("arbitrary", "arbitrary"),
        name="ffn_sublayer",
    )(x_chunks, x_chunks, mod, mod, g_pre, g_post, w_gate, w_up, w_down)
    return out.reshape(b, s, d)


def _gelu_tanh(x):
    return 0.5 * x * (1.0 + jnp.tanh(0.7978845608028654 * (x + 0.044715 * (x * x * x))))


def _in_proj_body(*refs, hyb, n_a, chunks):
    if hyb:
        xn_ref, modn_ref, gpre_ref, wa_ref, wb_ref, wdt_ref, o_ref, dt_ref, h0_ref, h1_ref = refs
    else:
        xn_ref, modn_ref, gpre_ref, wa_ref, bias_ref, o_ref, h0_ref, h1_ref = refs
    g = pl.program_id(0)
    j = pl.program_id(1)
    chunk = h0_ref.shape[0] // chunks
    c0 = jnp.minimum(j, chunks - 1) * chunk

    @pl.when((g == 0) & (j == 0))
    def _():
        h0_ref[...] = jnp.zeros_like(h0_ref)
        h1_ref[...] = jnp.zeros_like(h1_ref)

    def pre_norm_chunk(h_ref):
        gain = gpre_ref[...] * (1.0 + modn_ref[1:2, :])
        done = []
        for q in range(chunk // FFN_NORM_ROWS):
            rows = pl.ds(q * FFN_NORM_ROWS, FFN_NORM_ROWS)
            tile_rows = pl.ds(pl.multiple_of(c0 + q * FFN_NORM_ROWS, FFN_NORM_ROWS), FFN_NORM_ROWS)
            res = _rms(xn_ref[rows, :]) * gain + modn_ref[0:1, :]
            h_ref[tile_rows, :] = res.astype(BF16)
            done.append(res)
        return done

    def project(h_ref, w_ref, h_new):
        h = h_ref[...]
        later = h + _zero_after(pre_norm_chunk(h_new), j < 0, h.shape[1]).astype(BF16)
        block = PROJ_BLOCK_HYB if hyb else PROJ_BLOCK_GELU
        for t in range(o_ref.shape[1] // block):
            cols = slice(t * block, (t + 1) * block)
            y = _dot(h if t == 0 else later, w_ref[:, cols])
            if not hyb:
                y = _gelu_tanh(y + bias_ref[:, cols])
            o_ref[:, cols] = y.astype(o_ref.dtype)

    has_matmul = g >= 1
    for parity, (h_new, h_mid) in enumerate([(h0_ref, h1_ref), (h1_ref, h0_ref)]):
        mine = lax.rem(g, 2) == parity
        if hyb:
            @pl.when(mine & has_matmul & (j == 0))
            def _():
                project(h_mid, wa_ref, h_new)
                dt_ref[...] = _dot(h_mid[...], wdt_ref[...])

            @pl.when(mine & has_matmul & (j > 0) & (j < n_a))
            def _():
                project(h_mid, wa_ref, h_new)

            @pl.when(mine & has_matmul & (j >= n_a))
            def _():
                project(h_mid, wb_ref, h_new)
        else:
            @pl.when(mine & has_matmul)
            def _():
                project(h_mid, wa_ref, h_new)

        @pl.when(mine & jnp.logical_not(has_matmul) & (j < chunks))
        def _():
            pre_norm_chunk(h_new)


def _in_proj(x, mod, g_pre, weights, layer, sub, hyb, rows, tile):
    b, s, d = x.shape
    per_batch = s // rows
    n_tiles = b * per_batch
    w_a = weights[0]
    n_a = w_a.shape[1] // tile
    n_steps = n_a + (weights[1].shape[1] // tile if hyb else 0)
    chunks = min(PROJ_NORM_CHUNKS, n_steps)
    chunk = rows // chunks
    new_tile = lambda g: jnp.minimum(g, n_tiles - 1)
    mid_tile = lambda g: jnp.maximum(g - 1, 0)
    col = lambda g, j: jnp.where(g == 0, 0, j)
    in_specs = [
        pl.BlockSpec((None, chunk, d), lambda g, j: (new_tile(g) * chunks + jnp.minimum(j, chunks - 1), 0, 0)),
        pl.BlockSpec((None, None, None, 3, d), lambda g, j: (layer, new_tile(g) // per_batch, sub, 0, 0)),
        pl.BlockSpec((None, None, 1, d), lambda g, j: (layer, sub, 0, 0)),
        pl.BlockSpec((d, tile), lambda g, j: (0, jnp.minimum(col(g, j), n_a - 1))),
    ]
    out_specs = [pl.BlockSpec((None, rows, tile), lambda g, j: (mid_tile(g), 0, col(g, j)))]
    out_shape = [jax.ShapeDtypeStruct((n_tiles, rows, n_steps * tile), BF16)]
    if hyb:
        in_specs += [
            pl.BlockSpec((d, tile), lambda g, j: (0, jnp.maximum(col(g, j) - n_a, 0))),
            pl.BlockSpec((d, LANES), lambda g, j: (0, 0)),
        ]
        out_specs.append(pl.BlockSpec((None, rows, LANES), lambda g, j: (mid_tile(g), 0, 0)))
        out_shape.append(jax.ShapeDtypeStruct((n_tiles, rows, LANES), F32))
    else:
        in_specs.append(pl.BlockSpec((1, tile), lambda g, j: (0, col(g, j))))
    outs = pl.pallas_call(
        functools.partial(_in_proj_body, hyb=hyb, n_a=n_a, chunks=chunks),
        grid=(n_tiles + 1, n_steps),
        in_specs=in_specs,
        out_specs=out_specs,
        out_shape=out_shape,
        scratch_shapes=[pltpu.VMEM((rows, d), BF16)] * 2,
        compiler_params=_params("arbitrary", "arbitrary"),
        name="in_proj",
    )(x.reshape(n_tiles * chunks, chunk, d), mod, g_pre, *weights)
    return [o.reshape(b, s, o.shape[-1]) for o in outs]


def _sgu_gate_chunk(u, v, lng_ref, lnb_ref, ws_ref, bs_ref):
    L = SGU_CHUNK
    v = v.astype(F32)
    mu = jnp.mean(v, axis=-1, keepdims=True)
    vc = v - mu
    var = jnp.mean(vc * vc, axis=-1, keepdims=True)
    vn = (vc * lax.rsqrt(var + NORM_EPS) * lng_ref[...] + lnb_ref[...]).astype(BF16)
    row = lax.broadcasted_iota(jnp.int32, (L, L), 0)
    col = lax.broadcasted_iota(jnp.int32, (L, L), 1)
    causal = row >= col
    parts = []
    for g in range(SGU_GROUPS):
        gs = slice(g * SGU_GROUP_WIDTH, (g + 1) * SGU_GROUP_WIDTH)
        w = jnp.where(causal, ws_ref[g], 0.0).astype(BF16)
        mixed = _dot(w, vn[:, gs]) + bs_ref[:, g:g + 1]
        parts.append((u[:, gs].astype(F32) * mixed).astype(BF16))
    return jnp.concatenate(parts, axis=1)


def _mixer_out_body(*refs, sgu, n_tiles):
    if sgu:
        (u_ref, v_ref, lng_ref, lnb_ref, ws_ref, bs_ref, w_ref, xp_ref, modp_ref, gpost_ref, o_ref,
         lhs0_ref, lhs1_ref, acc0_ref, acc1_ref) = refs
    else:
        a_ref, b_ref, w_ref, xp_ref, modp_ref, gpost_ref, o_ref, acc0_ref, acc1_ref = refs
        lhs0_ref = lhs1_ref = None
    g = pl.program_id(0)
    k = pl.program_id(1)
    chunk = MIX_ROWS // MIX_STEPS
    half = w_ref.shape[0] // MIX_STEPS

    @pl.when((g == 0) & (k == 0))
    def _():
        for ref in (lhs0_ref, lhs1_ref, acc0_ref, acc1_ref):
            if ref is not None:
                ref[...] = jnp.zeros_like(ref)

    def post_norm_chunk(acc_ref):
        gain = (MIXER_RES_WEIGHT * (1.0 + modp_ref[2:3, :])) * gpost_ref[...]
        done = []
        for q in range(chunk // FFN_NORM_ROWS):
            rows = pl.ds(q * FFN_NORM_ROWS, FFN_NORM_ROWS)
            tile_rows = pl.ds(pl.multiple_of(k * chunk + q * FFN_NORM_ROWS, FFN_NORM_ROWS), FFN_NORM_ROWS)
            res = xp_ref[rows, :] + _rms(acc_ref[tile_rows, :]) * gain
            o_ref[rows, :] = res
            done.append(res)
        return done

    def gate_chunk(lhs_ref):
        for j in range(chunk // SGU_CHUNK):
            rows = slice(j * SGU_CHUNK, (j + 1) * SGU_CHUNK)
            gated = _sgu_gate_chunk(u_ref[rows, :], v_ref[rows, :], lng_ref, lnb_ref, ws_ref, bs_ref)
            tile_rows = pl.ds(pl.multiple_of(k * chunk + j * SGU_CHUNK, SGU_CHUNK), SGU_CHUNK)
            for part in range(MIX_STEPS):
                lhs_ref[part, tile_rows, :] = gated[:, part * half:(part + 1) * half]

    def matmul_step(lhs_ref, acc_ref, done):
        lhs = lhs_ref[k] if sgu else jnp.where(k == 0, a_ref[...], b_ref[...])
        later = lhs + _zero_after(done, k < 0, half).astype(BF16)
        rows = pl.ds(pl.multiple_of(k * half, half), half)
        mid = acc_ref.shape[1] // 2
        for cols, operand in ((slice(0, mid), lhs), (slice(mid, 2 * mid), later)):
            acc_ref[:, cols] = jnp.where(k > 0, acc_ref[:, cols], 0.0) + _dot(operand, w_ref[rows, cols])

    has_matmul = (g >= 1) & (g <= n_tiles)
    for parity, (lhs_new, acc_old, lhs_mid, acc_mid) in enumerate(
            [(lhs0_ref, acc0_ref, lhs1_ref, acc1_ref), (lhs1_ref, acc1_ref, lhs0_ref, acc0_ref)]):
        @pl.when((lax.rem(g, 2) == parity) & has_matmul)
        def _():
            matmul_step(lhs_mid, acc_mid, post_norm_chunk(acc_old))
            if sgu:
                gate_chunk(lhs_new)

        if sgu:
            @pl.when((lax.rem(g, 2) == parity) & (g == 0))
            def _():
                gate_chunk(lhs_new)

        @pl.when((lax.rem(g, 2) == parity) & (g == n_tiles + 1))
        def _():
            post_norm_chunk(acc_old)


def _mixer_out(operands, w, x, mod, g_post, layer, sub, sgu):
    b, s, d = x.shape
    half = w.shape[0] // MIX_STEPS
    per_batch = s // MIX_ROWS
    n_tiles = b * per_batch
    steps = MIX_STEPS
    chunk = MIX_ROWS // steps
    new_tile = lambda g: jnp.minimum(g, n_tiles - 1)
    mid_tile = lambda g: jnp.clip(g - 1, 0, n_tiles - 1)
    old_tile = lambda g: jnp.clip(g - 2, 0, n_tiles - 1)
    new_chunk = lambda g, k: new_tile(g) * steps + k
    old_chunk = lambda g, k: old_tile(g) * steps + k
    out_chunk = lambda g, k: (jnp.where(g < 2, 0, old_chunk(g, k)), 0, 0)
    x_chunks = x.reshape(n_tiles * steps, chunk, d)
    common_specs = [
        pl.BlockSpec((steps * half, d), lambda g, k: (0, 0), pipeline_mode=pl.Buffered(1)),
        pl.BlockSpec((None, chunk, d), lambda g, k: (old_chunk(g, k), 0, 0)),
        pl.BlockSpec((None, None, None, 3, d), lambda g, k: (layer, old_tile(g) // per_batch, sub, 0, 0)),
        pl.BlockSpec((None, None, 1, d), lambda g, k: (layer, sub, 0, 0)),
    ]
    acc = [pltpu.VMEM((MIX_ROWS, d), F32)] * 2
    if sgu:
        zz, ln_g, ln_b, w_spatial, b_spatial = operands
        zz_chunks = zz.reshape(n_tiles * steps, chunk, 2 * SGU_WIDTH)
        vec = pl.BlockSpec((1, SGU_WIDTH), lambda g, k: (0, 0))
        lhs_specs = [
            pl.BlockSpec((None, chunk, SGU_WIDTH), lambda g, k: (new_chunk(g, k), 0, 0)),
            pl.BlockSpec((None, chunk, SGU_WIDTH), lambda g, k: (new_chunk(g, k), 0, 1)),
            vec, vec,
            pl.BlockSpec((SGU_GROUPS, SGU_CHUNK, SGU_CHUNK), lambda g, k: (0, 0, 0)),
            pl.BlockSpec((SGU_CHUNK, SGU_GROUPS), lambda g, k: (0, 0)),
        ]
        lhs_args = [zz_chunks, zz_chunks, ln_g.reshape(1, SGU_WIDTH), ln_b.reshape(1, SGU_WIDTH), w_spatial,
                    b_spatial.T]
        scratch = [pltpu.VMEM((steps, MIX_ROWS, half), BF16)] * 2 + acc
    else:
        y_a, y_b = operands
        tile = pl.BlockSpec((None, MIX_ROWS, half), lambda g, k: (mid_tile(g), 0, 0))
        lhs_specs = [tile, tile]
        lhs_args = [y_a.reshape(n_tiles, MIX_ROWS, half), y_b.reshape(n_tiles, MIX_ROWS, half)]
        scratch = acc
    out = pl.pallas_call(
        functools.partial(_mixer_out_body, sgu=sgu, n_tiles=n_tiles),
        grid=(n_tiles + 2, steps),
        in_specs=lhs_specs + common_specs,
        out_specs=pl.BlockSpec((None, chunk, d), out_chunk),
        out_shape=jax.ShapeDtypeStruct(x_chunks.shape, F32),
        scratch_shapes=scratch,
        compiler_params=_params("arbitrary", "arbitrary"),
        name="mixer_out",
    )(*lhs_args, w, x_chunks, mod, g_post)
    return out.reshape(b, s, d)


def _rope_body(pos_ref, freq_ref, cos_ref, sin_ref):
    ang = pos_ref[...].astype(F32) * freq_ref[...]
    lane = lax.broadcasted_iota(jnp.int32, ang.shape, 1)
    cos_ref[...] = jnp.cos(ang)
    sin_ref[...] = jnp.where(lane < ATT_HEAD_DIM // 2, -1.0, 1.0) * jnp.sin(ang)


def _rope_tables(positions):
    b, s = positions.shape
    half = ATT_HEAD_DIM // 2
    inv_freq = ROPE_THETA ** (-jnp.arange(half, dtype=F32) / half)
    freq = jnp.concatenate([inv_freq, inv_freq]).reshape(1, ATT_HEAD_DIM)
    tile = 512
    spec = pl.BlockSpec((None, tile, ATT_HEAD_DIM), lambda bi, i: (bi, i, 0))
    return pl.pallas_call(
        _rope_body,
        grid=(b, s // tile),
        in_specs=[
            pl.BlockSpec((None, tile, 1), lambda bi, i: (bi, i, 0)),
            pl.BlockSpec((1, ATT_HEAD_DIM), lambda bi, i: (0, 0)),
        ],
        out_specs=[spec, spec],
        out_shape=[jax.ShapeDtypeStruct((b, s, ATT_HEAD_DIM), F32)] * 2,
        compiler_params=_params("parallel", "parallel"),
        name="rope_tables",
    )(positions.reshape(b, s, 1), freq)


def _ssd_body(xs_ref, bm_ref, cm_ref, z_ref, dt_ref, cw_ref, cb_ref, dtb_ref, alog_ref, dskip_ref, ng_ref,
              o_ref, tail_ref, state_ref):
    L = SSD_CHUNK
    T = SSD_TAIL

    @pl.when(pl.program_id(1) == 0)
    def _():
        tail_ref[...] = jnp.zeros_like(tail_ref)
        state_ref[...] = jnp.zeros_like(state_ref)

    srow = lax.broadcasted_iota(jnp.int32, ((SSD_CONV - 1) * L, L + T), 0)
    scol = lax.broadcasted_iota(jnp.int32, ((SSD_CONV - 1) * L, L + T), 1)
    lag = (srow >> (L.bit_length() - 1)) + 1
    t_in = srow & (L - 1)
    shift = (scol == jnp.where(t_in >= lag, t_in - lag, t_in - lag + (L + T))).astype(BF16)

    def conv_silu(raw_ref, lo, hi):
        raw = raw_ref[...]
        lagged = _dot(shift, jnp.concatenate([raw, tail_ref[:, lo:hi]], axis=0))
        acc = cb_ref[:, lo:hi] + raw.astype(F32) * cw_ref[SSD_CONV - 1:SSD_CONV, lo:hi]
        for k in range(1, SSD_CONV):
            acc = acc + lagged[(k - 1) * L:k * L] * cw_ref[SSD_CONV - 1 - k:SSD_CONV - k, lo:hi]
        tail_ref[:, lo:hi] = raw[L - T:L]
        return _silu(acc)

    xs = conv_silu(xs_ref, 0, SSD_WIDTH)
    bm = conv_silu(bm_ref, SSD_WIDTH, SSD_WIDTH + SSD_BC_WIDTH)
    cm = conv_silu(cm_ref, SSD_WIDTH + SSD_BC_WIDTH, SSD_CONV_CH)

    dt_in = dt_ref[...] + dtb_ref[...]
    dt = jnp.maximum(dt_in, 0.0) + jnp.log1p(jnp.exp(-jnp.abs(dt_in)))
    adt = dt * (-jnp.exp(alog_ref[...]))
    row = lax.broadcasted_iota(jnp.int32, (L, L), 0)
    col = lax.broadcasted_iota(jnp.int32, (L, L), 1)
    causal = row >= col
    acs = _dot_exact(causal.astype(F32), adt)
    acs_t = acs.T
    dt_t = dt.T
    acs_last = acs[L - 1:L, :]

    hrow = lax.broadcasted_iota(jnp.int32, (LANES, SSD_WIDTH), 0)
    hcol = lax.broadcasted_iota(jnp.int32, (LANES, SSD_WIDTH), 1)
    expand = (hrow == (hcol >> (SSD_HEAD_DIM.bit_length() - 1))).astype(BF16)
    stacked = jnp.concatenate([jnp.exp(acs), jnp.exp(acs_last - acs) * dt], axis=0)
    high = stacked.astype(BF16)
    rest = (stacked - high.astype(F32)).astype(BF16)
    wide = _dot(high, expand) + _dot(rest, expand)
    decay_in_w, dt_decay_out_w = wide[0:L], wide[L:2 * L]
    chunk_decay_w = decay_in_w[L - 1:L, :]

    lane = lax.broadcasted_iota(jnp.int32, (L, LANES), 1)
    first_head = lane < SSD_HEAD_DIM

    y_parts = []
    for g in range(SSD_GROUPS):
        gs = slice(g * SSD_GROUP_WIDTH, (g + 1) * SSD_GROUP_WIDTH)
        bg = bm[:, g * SSD_STATE:(g + 1) * SSD_STATE]
        cg = cm[:, g * SSD_STATE:(g + 1) * SSD_STATE].astype(BF16)
        cb = _dot_nt(cg, bg.astype(BF16))
        state = state_ref[g]
        y_off = _dot(cg, state.astype(BF16)) * decay_in_w[:, gs]
        y_diag = []
        for j in range(SSD_HEADS_PER_GROUP // 2):
            h0 = g * SSD_HEADS_PER_GROUP + 2 * j
            ms = []
            for h in (h0, h0 + 1):
                seg = acs[:, h:h + 1] - acs_t[h:h + 1, :]
                ms.append(cb * jnp.exp(jnp.where(causal, seg, -jnp.inf)) * dt_t[h:h + 1, :])
            lhs = jnp.concatenate(ms, axis=1).astype(BF16)
            xp = xs[:, h0 * SSD_HEAD_DIM:(h0 + 2) * SSD_HEAD_DIM]
            rhs = jnp.concatenate([jnp.where(first_head, xp, 0.0), jnp.where(first_head, 0.0, xp)], axis=0)
            y_diag.append(_dot(lhs, rhs.astype(BF16)))
        y_parts.append(jnp.concatenate(y_diag, axis=1) + y_off)
        contrib = _dot(bg.T.astype(BF16), (xs[:, gs] * dt_decay_out_w[:, gs]).astype(BF16))
        state_ref[g] = state * chunk_decay_w[:, gs] + contrib

    y = jnp.concatenate(y_parts, axis=1) + xs * dskip_ref[...]
    y = y * _silu(z_ref[...].astype(F32))
    o_ref[...] = (_rms(y) * ng_ref[...]).astype(o_ref.dtype)


def _ssd_mixer(proj, dt_raw, conv_w, conv_b, dt_bias, a_log, d_skip, norm_g):
    b, s, _ = proj.shape
    assert proj.dtype == BF16
    L = SSD_CHUNK
    z_blk = 0
    xs_blk = SSD_WIDTH // SSD_WIDTH
    bm_blk = (2 * SSD_WIDTH) // SSD_BC_WIDTH
    cm_blk = bm_blk + 1
    pad = LANES - SSD_HEADS
    small = lambda a: pl.BlockSpec(a.shape, lambda bi, c: (0, 0))
    dt_bias_p = jnp.pad(dt_bias, (0, pad)).reshape(1, LANES)
    a_log_p = jnp.pad(a_log, (0, pad)).reshape(1, LANES)
    d_skip_w = jnp.repeat(d_skip, SSD_HEAD_DIM).reshape(1, SSD_WIDTH)
    conv_b2 = conv_b.reshape(1, SSD_CONV_CH)
    norm_g2 = norm_g.reshape(1, SSD_WIDTH)
    return pl.pallas_call(
        _ssd_body,
        grid=(b, s // L),
        in_specs=[
            pl.BlockSpec((None, L, SSD_WIDTH), lambda bi, c: (bi, c, xs_blk)),
            pl.BlockSpec((None, L, SSD_BC_WIDTH), lambda bi, c: (bi, c, bm_blk)),
            pl.BlockSpec((None, L, SSD_BC_WIDTH), lambda bi, c: (bi, c, cm_blk)),
            pl.BlockSpec((None, L, SSD_WIDTH), lambda bi, c: (bi, c, z_blk)),
            pl.BlockSpec((None, L, LANES), lambda bi, c: (bi, c, 0)),
            small(conv_w), small(conv_b2), small(dt_bias_p), small(a_log_p), small(d_skip_w), small(norm_g2),
        ],
        out_specs=pl.BlockSpec((None, L, SSD_WIDTH), lambda bi, c: (bi, c, 0)),
        out_shape=jax.ShapeDtypeStruct((b, s, SSD_WIDTH), BF16),
        scratch_shapes=[
            pltpu.VMEM((SSD_TAIL, SSD_CONV_CH), BF16),
            pltpu.VMEM((SSD_GROUPS, SSD_STATE, SSD_GROUP_WIDTH), F32),
        ],
        compiler_params=_params("parallel", "arbitrary"),
        name="ssd_mixer",
    )(proj, proj, proj, proj, dt_raw, conv_w, conv_b2, dt_bias_p, a_log_p, d_skip_w, norm_g2)


def _attn_body(q_ref, k_ref, v_ref, cos_ref, sin_ref, o_ref, qkv_ref, acc_ref, m_ref, l_ref, *, seq):
    blk = ATT_BLOCK
    half = ATT_HEAD_DIM // 2
    dils = [d for _, d in DILATED_PATTERNS]
    step = dils[1]
    assert dils == [1, step, step * step] and all(w // d == blk for w, d in DILATED_PATTERNS)
    sub = seq // step
    assert seq // dils[2] == blk

    cos = cos_ref[...]
    sin = sin_ref[...]
    q = q_ref[...].astype(F32)
    k = k_ref[...].astype(F32)
    qkv_ref[0, 0] = (q * cos + pltpu.roll(q, half, 1) * sin) * (ATT_HEAD_DIM ** -0.5 * LOG2_E)
    qkv_ref[0, 1] = k * cos + pltpu.roll(k, half, 1) * sin
    qkv_ref[0, 2] = v_ref[...].astype(F32)
    for t in range(3):
        for r in range(step):
            qkv_ref[1, t, pl.ds(r * sub, sub), :] = qkv_ref[0, t, pl.ds(r, sub, stride=step), :]
    for t in range(3):
        for r in range(step):
            for a in range(step):
                qkv_ref[2, t, pl.ds((r + step * a) * blk, blk), :] = qkv_ref[1, t, pl.ds(r * sub + a, blk, stride=step), :]

    row = lax.broadcasted_iota(jnp.int32, (blk, blk), 0)
    col = lax.broadcasted_iota(jnp.int32, (blk, blk), 1)
    cur_ok = col <= row
    prev_ok = col >= row

    def attend(p, starts, with_prev):
        def load(t, st):
            return qkv_ref[p, t, pl.ds(st, blk), :].astype(BF16)

        def window(t, st):
            return jnp.concatenate([load(t, st - blk), load(t, st)], axis=0) if with_prev else load(t, st)

        qb = jnp.stack([load(0, st) for st in starts])
        kk = jnp.stack([window(1, st) for st in starts])
        keys = kk.shape[1]
        vv = jnp.stack([jnp.concatenate([window(2, st), jnp.ones((keys, LANES), BF16)], axis=1) for st in starts])
        ok = jnp.concatenate([prev_ok, cur_ok], axis=1) if with_prev else cur_ok
        s = lax.dot_general(qb, kk, (((2,), (2,)), ((0,), (0,))), preferred_element_type=F32)
        s = jnp.where(ok[None], s, -jnp.inf)
        m = jnp.max(s, axis=2, keepdims=True)
        e = jnp.exp2(s - m)
        acc = lax.dot_general(e.astype(BF16), vv, (((2,), (1,)), ((0,), (0,))), preferred_element_type=F32)
        for i, st in enumerate(starts):
            rows = pl.ds(st, blk)
            acc_ref[p, rows, :] = acc[i, :, 0:ATT_HEAD_DIM]
            l_ref[p, rows, :] = acc[i, :, ATT_HEAD_DIM:]
            m_ref[p, rows, :] = jnp.broadcast_to(m[i], (blk, LANES))

    def groups(starts, size):
        return [starts[i:i + size] for i in range(0, len(starts), size)]

    for p, dil in enumerate(dils):
        class_rows = seq // dil
        firsts = [r * class_rows for r in range(dil)]
        laters = [r * class_rows + n * blk for r in range(dil) for n in range(1, class_rows // blk)]
        for g in groups(firsts, ATT_GROUP_FIRST):
            attend(p, g, False)
        for g in groups(laters, ATT_GROUP_LATER):
            attend(p, g, True)

    def merged(dst, dst_rows, src, src_rows):
        m_a, m_b = m_ref[dst, dst_rows, :], m_ref[src, src_rows, :]
        top = jnp.maximum(m_a, m_b)
        w_a, w_b = jnp.exp2(m_a - top), jnp.exp2(m_b - top)
        acc = w_a * acc_ref[dst, dst_rows, :] + w_b * acc_ref[src, src_rows, :]
        return top, acc, w_a * l_ref[dst, dst_rows, :] + w_b * l_ref[src, src_rows, :]

    for r in range(step):
        for a in range(step):
            mid_rows = pl.ds(r * sub + a, blk, stride=step)
            top, acc, l = merged(1, mid_rows, 2, pl.ds((r + step * a) * blk, blk))
            m_ref[1, mid_rows, :] = top
            acc_ref[1, mid_rows, :] = acc
            l_ref[1, mid_rows, :] = l
    for r in range(step):
        for n in range(sub // blk):
            nat_rows = pl.ds(r + n * blk * step, blk, stride=step)
            _, acc, l = merged(0, nat_rows, 1, pl.ds(r * sub + n * blk, blk))
            acc_ref[0, nat_rows, :] = acc / l
    o_ref[...] = acc_ref[0].astype(o_ref.dtype)


def _dilated_attention(proj, cos2, sin2):
    b, s, _ = proj.shape
    q_blk = (2 * SSD_WIDTH + 2 * SSD_BC_WIDTH) // ATT_HEAD_DIM
    k_blk = q_blk + ATT_HEADS
    v_blk = k_blk + ATT_HEADS
    n_pat = len(DILATED_PATTERNS)
    head = lambda base: pl.BlockSpec((None, s, ATT_HEAD_DIM), lambda bi, h: (bi, 0, base + h))
    table = pl.BlockSpec((None, s, ATT_HEAD_DIM), lambda bi, h: (bi, 0, 0))
    return pl.pallas_call(
        functools.partial(_attn_body, seq=s),
        grid=(b, ATT_HEADS),
        in_specs=[head(q_blk), head(k_blk), head(v_blk), table, table],
        out_specs=pl.BlockSpec((None, s, ATT_HEAD_DIM), lambda bi, h: (bi, 0, h)),
        out_shape=jax.ShapeDtypeStruct((b, s, ATT_WIDTH), BF16),
        scratch_shapes=[
            pltpu.VMEM((n_pat, 3, s, ATT_HEAD_DIM), F32),
            pltpu.VMEM((n_pat, s, ATT_HEAD_DIM), F32),
            pltpu.VMEM((n_pat, s, LANES), F32),
            pltpu.VMEM((n_pat, s, LANES), F32),
        ],
        compiler_params=_params("parallel", "arbitrary"),
        name="dilated_attention",
    )(proj, proj, proj, cos2, sin2)


def kernel(x, c, positions, w_mod, b_mod, norm_pre, norm_post, ffn_w_gate, ffn_w_up, ffn_w_down, hyb_w_in, hyb_conv_w, hyb_conv_b, hyb_dt_bias, hyb_a_log, hyb_d_skip, hyb_norm_g, hyb_w_out, sgu_w_in, sgu_b_in, sgu_ln_g, sgu_ln_b, sgu_w_spatial, sgu_b_spatial, sgu_w_out):
    depth = w_mod.shape[0]
    b, s, d = x.shape
    n_sub = norm_pre.shape[1]

    mod = _modulation(c, w_mod, b_mod).reshape(depth, b, n_sub, 3, d)
    g_pre = norm_pre.reshape(depth, n_sub, 1, d)
    g_post = norm_post.reshape(depth, n_sub, 1, d)
    w_gate = ffn_w_gate.astype(BF16)
    w_up = ffn_w_up.astype(BF16)
    w_down = ffn_w_down.astype(BF16)

    for layer in range(depth):
        i = layer // 2
        x = _ffn_sublayer(x, mod, g_pre, g_post, w_gate, w_up, w_down, layer, 0, 0)
        if layer % 2 == 0:
            w_in = hyb_w_in[i]
            dt_lo = SSD_WIDTH + SSD_CONV_CH
            dt_hi = dt_lo + SSD_HEADS
            w_dt = jnp.pad(w_in[:, dt_lo:dt_hi], ((0, 0), (0, LANES - SSD_HEADS))).astype(BF16)
            proj, dt_raw = _in_proj(x, mod, g_pre, (w_in[:, :dt_lo].astype(BF16), w_in[:, dt_hi:].astype(BF16), w_dt),
                                    layer, 1, hyb=True, rows=PROJ_ROWS_HYB, tile=PROJ_TILE)
            cos2, sin2 = _rope_tables(positions)
            y_a = _ssd_mixer(proj, dt_raw, hyb_conv_w[i], hyb_conv_b[i], hyb_dt_bias[i], hyb_a_log[i],
                             hyb_d_skip[i], hyb_norm_g[i])
            y_b = _dilated_attention(proj, cos2, sin2)
            x = _mixer_out((y_a, y_b), hyb_w_out[i].astype(BF16), x, mod, g_post, layer, 1, sgu=False)
        else:
            zz, = _in_proj(x, mod, g_pre, (sgu_w_in[i].astype(BF16), sgu_b_in[i].reshape(1, -1)), layer, 1,
                           hyb=False, rows=PROJ_ROWS_GELU, tile=2 * PROJ_TILE)
            x = _mixer_out((zz, sgu_ln_g[i], sgu_ln_b[i], sgu_w_spatial[i], sgu_b_spatial[i]),
                           sgu_w_out[i].astype(BF16), x, mod, g_post, layer, 1, sgu=True)
        x = _ffn_sublayer(x, mod, g_pre, g_post, w_gate, w_up, w_down, layer, 2, 1)
    return x
```

```python
import functools

import jax
import jax.numpy as jnp
from jax import lax
from jax.experimental import pallas as pl
from jax.experimental.pallas import tpu as pltpu

NORM_EPS = 1e-6
LOG2_E = 1.4426950408889634
FFN_RES_WEIGHT = 0.5
MIXER_RES_WEIGHT = 1.0

SSD_HEADS = 32
SSD_HEAD_DIM = 64
SSD_WIDTH = SSD_HEADS * SSD_HEAD_DIM
SSD_GROUPS = 4
SSD_STATE = 128
SSD_CONV = 4
SSD_CHUNK = 128
SSD_TAIL = 16
SSD_STEP_CHUNKS = 2
SSD_BC_WIDTH = SSD_GROUPS * SSD_STATE
SSD_CONV_CH = SSD_WIDTH + 2 * SSD_BC_WIDTH
SSD_HEADS_PER_GROUP = SSD_HEADS // SSD_GROUPS
SSD_GROUP_WIDTH = SSD_HEADS_PER_GROUP * SSD_HEAD_DIM

ATT_HEADS = 16
ATT_HEAD_DIM = 128
ATT_WIDTH = ATT_HEADS * ATT_HEAD_DIM
ATT_BLOCK = 128
DILATED_PATTERNS = ((128, 1), (512, 4), (2048, 16))
ROPE_THETA = 10000.0

SGU_WIDTH = 4096
SGU_GROUPS = 8
SGU_CHUNK = 128
SGU_GROUP_WIDTH = SGU_WIDTH // SGU_GROUPS

LANES = 128
SUBLANES = 8
VMEM_LIMIT_BYTES = 56 * 1024 * 1024

MIX_ROWS = 512
MIX_STEPS = 2
FFN_ROWS = 1024
FFN_TILE = 512
FFN_NORM_CHUNKS = 8
FFN_NORM_ROWS = 16
PROJ_ROWS_HYB = 1024
PROJ_ROWS_GELU = 1024
PROJ_TILE = 1024
PROJ_BLOCK_HYB = 512
PROJ_BLOCK_GELU = 256
PROJ_NORM_CHUNKS = 8
MOD_TILE = 2048
ATT_GROUP_FIRST = 16
ATT_GROUP_LATER = 15

BF16 = jnp.bfloat16
F32 = jnp.float32


def _params(*semantics):
    return pltpu.CompilerParams(dimension_semantics=semantics, vmem_limit_bytes=VMEM_LIMIT_BYTES)


def _rms(x):
    return x * lax.rsqrt(jnp.mean(x * x, axis=-1, keepdims=True) + NORM_EPS)


def _silu(x):
    h = 0.5 * x
    return h + h * jnp.tanh(h)


def _dot(a, b):
    return jnp.dot(a, b, preferred_element_type=F32)


def _dot_exact(a, b):
    return jnp.dot(a, b, preferred_element_type=F32, precision=lax.Precision.HIGHEST)


def _zero_after(done, never, width):
    total = jnp.zeros((SUBLANES, LANES), F32)
    for res in done:
        for r in range(0, res.shape[0], SUBLANES):
            for c in range(0, res.shape[1], LANES):
                total = total + res[r:r + SUBLANES, c:c + LANES]
    row = jnp.sum(jnp.where(never, total, 0.0), axis=0, keepdims=True)
    return jnp.tile(row, (1, width // LANES))


def _dot_nt(a, b):
    return lax.dot_general(a, b, (((1,), (1,)), ((), ())), preferred_element_type=F32)


def _mod_body(c_ref, w_ref, b_ref, o_ref):
    ca = _silu(c_ref[...]).astype(BF16)
    o_ref[...] = _dot(ca, w_ref[...].astype(BF16)) + b_ref[...]


def _modulation(c, w_mod, b_mod):
    depth, d, n = w_mod.shape
    b = c.shape[0]
    return pl.pallas_call(
        _mod_body,
        grid=(depth, n // MOD_TILE),
        in_specs=[
            pl.BlockSpec((b, d), lambda l, j: (0, 0)),
            pl.BlockSpec((None, d, MOD_TILE), lambda l, j: (l, 0, j)),
            pl.BlockSpec((None, 1, MOD_TILE), lambda l, j: (l, 0, j)),
        ],
        out_specs=pl.BlockSpec((None, b, MOD_TILE), lambda l, j: (l, 0, j)),
        out_shape=jax.ShapeDtypeStruct((depth, b, n), F32),
        compiler_params=_params("parallel", "parallel"),
        name="modulation",
    )(c, w_mod, b_mod.reshape(depth, 1, n))


def _ffn_body(xn_ref, xp_ref, modn_ref, modp_ref, gpre_ref, gpost_ref, wg_ref, wu_ref, wd_ref, o_ref,
              h0_ref, h1_ref, acc0_ref, acc1_ref, *, n_tiles, n_f):
    g = pl.program_id(0)
    f = pl.program_id(1)
    chunk = FFN_ROWS // FFN_NORM_CHUNKS
    c0 = jnp.minimum(f, FFN_NORM_CHUNKS - 1) * chunk

    @pl.when((g == 0) & (f == 0))
    def _():
        for ref in (h0_ref, h1_ref, acc0_ref, acc1_ref):
            ref[...] = jnp.zeros_like(ref)

    def pre_norm_chunk(h_ref):
        gain = gpre_ref[...] * (1.0 + modn_ref[1:2, :])
        done = []
        for q in range(chunk // FFN_NORM_ROWS):
            rows = pl.ds(q * FFN_NORM_ROWS, FFN_NORM_ROWS)
            tile_rows = pl.ds(pl.multiple_of(c0 + q * FFN_NORM_ROWS, FFN_NORM_ROWS), FFN_NORM_ROWS)
            res = _rms(xn_ref[rows, :]) * gain + modn_ref[0:1, :]
            h_ref[tile_rows, :] = res.astype(BF16)
            done.append(res)
        return done

    def post_norm_chunk(acc_ref):
        gain = (FFN_RES_WEIGHT * (1.0 + modp_ref[2:3, :])) * gpost_ref[...]
        done = []
        for q in range(chunk // FFN_NORM_ROWS):
            rows = pl.ds(q * FFN_NORM_ROWS, FFN_NORM_ROWS)
            tile_rows = pl.ds(pl.multiple_of(c0 + q * FFN_NORM_ROWS, FFN_NORM_ROWS), FFN_NORM_ROWS)
            res = xp_ref[rows, :] + _rms(acc_ref[tile_rows, :]) * gain
            o_ref[rows, :] = res
            done.append(res)
        return done

    def zero_after(done):
        return _zero_after(done, f < 0, FFN_TILE)

    def swiglu_step(h_ref, acc_ref, zero):
        h = h_ref[...]
        a = (_silu(_dot(h, wg_ref[...])) * _dot(h, wu_ref[...]) + zero).astype(BF16)
        acc_ref[...] = jnp.where(f > 0, acc_ref[...], 0.0) + _dot(a, wd_ref[...])

    has_matmul = (g >= 1) & (g <= n_tiles)
    for parity, (h_new, acc_old, h_mid, acc_mid) in enumerate(
            [(h0_ref, acc0_ref, h1_ref, acc1_ref), (h1_ref, acc1_ref, h0_ref, acc0_ref)]):
        @pl.when((lax.rem(g, 2) == parity) & has_matmul)
        def _():
            swiglu_step(h_mid, acc_mid, zero_after(post_norm_chunk(acc_old) + pre_norm_chunk(h_new)))

        @pl.when((lax.rem(g, 2) == parity) & (g == 0) & (f < FFN_NORM_CHUNKS))
        def _():
            pre_norm_chunk(h_new)

        @pl.when((lax.rem(g, 2) == parity) & (g == n_tiles + 1) & (f < FFN_NORM_CHUNKS))
        def _():
            post_norm_chunk(acc_old)


def _ffn_sublayer(x, mod, g_pre, g_post, w_gate, w_up, w_down, layer, sub, idx):
    b, s, d = x.shape
    f_dim = w_gate.shape[-1]
    n_f = f_dim // FFN_TILE
    per_batch = s // FFN_ROWS
    n_tiles = b * per_batch
    chunks = FFN_NORM_CHUNKS
    chunk = FFN_ROWS // chunks
    assert n_f >= chunks
    new_tile = lambda g: jnp.minimum(g, n_tiles - 1)
    old_tile = lambda g: jnp.clip(g - 2, 0, n_tiles - 1)
    new_chunk = lambda g, f: (new_tile(g) * chunks + jnp.minimum(f, chunks - 1), 0, 0)
    old_chunk = lambda g, f: (old_tile(g) * chunks + jnp.minimum(f, chunks - 1), 0, 0)
    out_chunk = lambda g, f: (jnp.where(g < 2, 0, old_chunk(g, f)[0]), 0, 0)
    w_step = lambda g, f: jnp.where(g == 0, 0, jnp.where(g == n_tiles + 1, n_f - 1, f))
    x_chunks = x.reshape(n_tiles * chunks, chunk, d)
    out = pl.pallas_call(
        functools.partial(_ffn_body, n_tiles=n_tiles, n_f=n_f),
        grid=(n_tiles + 2, n_f),
        in_specs=[
            pl.BlockSpec((None, chunk, d), new_chunk),
            pl.BlockSpec((None, chunk, d), old_chunk),
            pl.BlockSpec((None, None, None, 3, d), lambda g, f: (layer, new_tile(g) // per_batch, sub, 0, 0)),
            pl.BlockSpec((None, None, None, 3, d), lambda g, f: (layer, old_tile(g) // per_batch, sub, 0, 0)),
            pl.BlockSpec((None, None, 1, d), lambda g, f: (layer, sub, 0, 0)),
            pl.BlockSpec((None, None, 1, d), lambda g, f: (layer, sub, 0, 0)),
            pl.BlockSpec((None, None, d, FFN_TILE), lambda g, f: (layer, idx, 0, w_step(g, f))),
            pl.BlockSpec((None, None, d, FFN_TILE), lambda g, f: (layer, idx, 0, w_step(g, f))),
            pl.BlockSpec((None, None, FFN_TILE, d), lambda g, f: (layer, idx, w_step(g, f), 0)),
        ],
        out_specs=pl.BlockSpec((None, chunk, d), out_chunk),
        out_shape=jax.ShapeDtypeStruct(x_chunks.shape, F32),
        scratch_shapes=[pltpu.VMEM((FFN_ROWS, d), BF16), pltpu.VMEM((FFN_ROWS, d), BF16),
                        pltpu.VMEM((FFN_ROWS, d), F32), pltpu.VMEM((FFN_ROWS, d), F32)],
        compiler_params=_params("arbitrary", "arbitrary"),
        name="ffn_sublayer",
    )(x_chunks, x_chunks, mod, mod, g_pre, g_post, w_gate, w_up, w_down)
    return out.reshape(b, s, d)


def _gelu_tanh(x):
    return 0.5 * x * (1.0 + jnp.tanh(0.7978845608028654 * (x + 0.044715 * (x * x * x))))


def _in_proj_body(*refs, hyb, n_a, chunks):
    if hyb:
        xn_ref, modn_ref, gpre_ref, wa_ref, wb_ref, wdt_ref, o_ref, dt_ref, h0_ref, h1_ref = refs
    else:
        xn_ref, modn_ref, gpre_ref, wa_ref, bias_ref, o_ref, h0_ref, h1_ref = refs
    g = pl.program_id(0)
    j = pl.program_id(1)
    chunk = h0_ref.shape[0] // chunks
    c0 = jnp.minimum(j, chunks - 1) * chunk

    @pl.when((g == 0) & (j == 0))
    def _():
        h0_ref[...] = jnp.zeros_like(h0_ref)
        h1_ref[...] = jnp.zeros_like(h1_ref)

    def pre_norm_chunk(h_ref):
        gain = gpre_ref[...] * (1.0 + modn_ref[1:2, :])
        done = []
        for q in range(chunk // FFN_NORM_ROWS):
            rows = pl.ds(q * FFN_NORM_ROWS, FFN_NORM_ROWS)
            tile_rows = pl.ds(pl.multiple_of(c0 + q * FFN_NORM_ROWS, FFN_NORM_ROWS), FFN_NORM_ROWS)
            res = _rms(xn_ref[rows, :]) * gain + modn_ref[0:1, :]
            h_ref[tile_rows, :] = res.astype(BF16)
            done.append(res)
        return done

    def project(h_ref, w_ref, h_new):
        h = h_ref[...]
        later = h + _zero_after(pre_norm_chunk(h_new), j < 0, h.shape[1]).astype(BF16)
        block = PROJ_BLOCK_HYB if hyb else PROJ_BLOCK_GELU
        for t in range(o_ref.shape[1] // block):
            cols = slice(t * block, (t + 1) * block)
            y = _dot(h if t == 0 else later, w_ref[:, cols])
            if not hyb:
                y = _gelu_tanh(y + bias_ref[:, cols])
            o_ref[:, cols] = y.astype(o_ref.dtype)

    has_matmul = g >= 1
    for parity, (h_new, h_mid) in enumerate([(h0_ref, h1_ref), (h1_ref, h0_ref)]):
        mine = lax.rem(g, 2) == parity
        if hyb:
            @pl.when(mine & has_matmul & (j == 0))
            def _():
                project(h_mid, wa_ref, h_new)
                dt_ref[...] = _dot(h_mid[...], wdt_ref[...])

            @pl.when(mine & has_matmul & (j > 0) & (j < n_a))
            def _():
                project(h_mid, wa_ref, h_new)

            @pl.when(mine & has_matmul & (j >= n_a))
            def _():
                project(h_mid, wb_ref, h_new)
        else:
            @pl.when(mine & has_matmul)
            def _():
                project(h_mid, wa_ref, h_new)

        @pl.when(mine & jnp.logical_not(has_matmul) & (j < chunks))
        def _():
            pre_norm_chunk(h_new)


def _in_proj(x, mod, g_pre, weights, layer, sub, hyb, rows, tile):
    b, s, d = x.shape
    per_batch = s // rows
    n_tiles = b * per_batch
    w_a = weights[0]
    n_a = w_a.shape[1] // tile
    n_steps = n_a + (weights[1].shape[1] // tile if hyb else 0)
    chunks = min(PROJ_NORM_CHUNKS, n_steps)
    chunk = rows // chunks
    new_tile = lambda g: jnp.minimum(g, n_tiles - 1)
    mid_tile = lambda g: jnp.maximum(g - 1, 0)
    col = lambda g, j: jnp.where(g == 0, 0, j)
    in_specs = [
        pl.BlockSpec((None, chunk, d), lambda g, j: (new_tile(g) * chunks + jnp.minimum(j, chunks - 1), 0, 0)),
        pl.BlockSpec((None, None, None, 3, d), lambda g, j: (layer, new_tile(g) // per_batch, sub, 0, 0)),
        pl.BlockSpec((None, None, 1, d), lambda g, j: (layer, sub, 0, 0)),
        pl.BlockSpec((d, tile), lambda g, j: (0, jnp.minimum(col(g, j), n_a - 1))),
    ]
    out_specs = [pl.BlockSpec((None, rows, tile), lambda g, j: (mid_tile(g), 0, col(g, j)))]
    out_shape = [jax.ShapeDtypeStruct((n_tiles, rows, n_steps * tile), BF16)]
    if hyb:
        in_specs += [
            pl.BlockSpec((d, tile), lambda g, j: (0, jnp.maximum(col(g, j) - n_a, 0))),
            pl.BlockSpec((d, LANES), lambda g, j: (0, 0)),
        ]
        out_specs.append(pl.BlockSpec((None, rows, LANES), lambda g, j: (mid_tile(g), 0, 0)))
        out_shape.append(jax.ShapeDtypeStruct((n_tiles, rows, LANES), F32))
    else:
        in_specs.append(pl.BlockSpec((1, tile), lambda g, j: (0, col(g, j))))
    outs = pl.pallas_call(
        functools.partial(_in_proj_body, hyb=hyb, n_a=n_a, chunks=chunks),
        grid=(n_tiles + 1, n_steps),
        in_specs=in_specs,
        out_specs=out_specs,
        out_shape=out_shape,
        scratch_shapes=[pltpu.VMEM((rows, d), BF16)] * 2,
        compiler_params=_params("arbitrary", "arbitrary"),
        name="in_proj",
    )(x.reshape(n_tiles * chunks, chunk, d), mod, g_pre, *weights)
    return [o.reshape(b, s, o.shape[-1]) for o in outs]


def _sgu_gate_chunk(u, v, lng_ref, lnb_ref, ws_ref, bs_ref):
    L = SGU_CHUNK
    v = v.astype(F32)
    mu = jnp.mean(v, axis=-1, keepdims=True)
    vc = v - mu
    var = jnp.mean(vc * vc, axis=-1, keepdims=True)
    vn = (vc * lax.rsqrt(var + NORM_EPS) * lng_ref[...] + lnb_ref[...]).astype(BF16)
    row = lax.broadcasted_iota(jnp.int32, (L, L), 0)
    col = lax.broadcasted_iota(jnp.int32, (L, L), 1)
    causal = row >= col
    parts = []
    for g in range(SGU_GROUPS):
        gs = slice(g * SGU_GROUP_WIDTH, (g + 1) * SGU_GROUP_WIDTH)
        w = jnp.where(causal, ws_ref[g], 0.0).astype(BF16)
        mixed = _dot(w, vn[:, gs]) + bs_ref[:, g:g + 1]
        parts.append((u[:, gs].astype(F32) * mixed).astype(BF16))
    return jnp.concatenate(parts, axis=1)


def _mixer_out_body(*refs, sgu, n_tiles):
    if sgu:
        (u_ref, v_ref, lng_ref, lnb_ref, ws_ref, bs_ref, w_ref, xp_ref, modp_ref, gpost_ref, o_ref,
         lhs0_ref, lhs1_ref, acc0_ref, acc1_ref) = refs
    else:
        a_ref, b_ref, w_ref, xp_ref, modp_ref, gpost_ref, o_ref, acc0_ref, acc1_ref = refs
        lhs0_ref = lhs1_ref = None
    g = pl.program_id(0)
    k = pl.program_id(1)
    chunk = MIX_ROWS // MIX_STEPS
    half = w_ref.shape[0] // MIX_STEPS

    @pl.when((g == 0) & (k == 0))
    def _():
        for ref in (lhs0_ref, lhs1_ref, acc0_ref, acc1_ref):
            if ref is not None:
                ref[...] = jnp.zeros_like(ref)

    def post_norm_chunk(acc_ref):
        gain = (MIXER_RES_WEIGHT * (1.0 + modp_ref[2:3, :])) * gpost_ref[...]
        done = []
        for q in range(chunk // FFN_NORM_ROWS):
            rows = pl.ds(q * FFN_NORM_ROWS, FFN_NORM_ROWS)
            tile_rows = pl.ds(pl.multiple_of(k * chunk + q * FFN_NORM_ROWS, FFN_NORM_ROWS), FFN_NORM_ROWS)
            res = xp_ref[rows, :] + _rms(acc_ref[tile_rows, :]) * gain
            o_ref[rows, :] = res
            done.append(res)
        return done

    def gate_chunk(lhs_ref):
        for j in range(chunk // SGU_CHUNK):
            rows = slice(j * SGU_CHUNK, (j + 1) * SGU_CHUNK)
            gated = _sgu_gate_chunk(u_ref[rows, :], v_ref[rows, :], lng_ref, lnb_ref, ws_ref, bs_ref)
            tile_rows = pl.ds(pl.multiple_of(k * chunk + j * SGU_CHUNK, SGU_CHUNK), SGU_CHUNK)
            for part in range(MIX_STEPS):
                lhs_ref[part, tile_rows, :] = gated[:, part * half:(part + 1) * half]

    def matmul_step(lhs_ref, acc_ref, done):
        lhs = lhs_ref[k] if sgu else jnp.where(k == 0, a_ref[...], b_ref[...])
        later = lhs + _zero_after(done, k < 0, half).astype(BF16)
        rows = pl.ds(pl.multiple_of(k * half, half), half)
        mid = acc_ref.shape[1] // 2
        for cols, operand in ((slice(0, mid), lhs), (slice(mid, 2 * mid), later)):
            acc_ref[:, cols] = jnp.where(k > 0, acc_ref[:, cols], 0.0) + _dot(operand, w_ref[rows, cols])

    has_matmul = (g >= 1) & (g <= n_tiles)
    for parity, (lhs_new, acc_old, lhs_mid, acc_mid) in enumerate(
            [(lhs0_ref, acc0_ref, lhs1_ref, acc1_ref), (lhs1_ref, acc1_ref, lhs0_ref, acc0_ref)]):
        @pl.when((lax.rem(g, 2) == parity) & has_matmul)
        def _():
            matmul_step(lhs_mid, acc_mid, post_norm_chunk(acc_old))
            if sgu:
                gate_chunk(lhs_new)

        if sgu:
            @pl.when((lax.rem(g, 2) == parity) & (g == 0))
            def _():
                gate_chunk(lhs_new)

        @pl.when((lax.rem(g, 2) == parity) & (g == n_tiles + 1))
        def _():
            post_norm_chunk(acc_old)


def _mixer_out(operands, w, x, mod, g_post, layer, sub, sgu):
    b, s, d = x.shape
    half = w.shape[0] // MIX_STEPS
    per_batch = s // MIX_ROWS
    n_tiles = b * per_batch
    steps = MIX_STEPS
    chunk = MIX_ROWS // steps
    new_tile = lambda g: jnp.minimum(g, n_tiles - 1)
    mid_tile = lambda g: jnp.clip(g - 1, 0, n_tiles - 1)
    old_tile = lambda g: jnp.clip(g - 2, 0, n_tiles - 1)
    new_chunk = lambda g, k: new_tile(g) * steps + k
    old_chunk = lambda g, k: old_tile(g) * steps + k
    out_chunk = lambda g, k: (jnp.where(g < 2, 0, old_chunk(g, k)), 0, 0)
    x_chunks = x.reshape(n_tiles * steps, chunk, d)
    common_specs = [
        pl.BlockSpec((steps * half, d), lambda g, k: (0, 0), pipeline_mode=pl.Buffered(1)),
        pl.BlockSpec((None, chunk, d), lambda g, k: (old_chunk(g, k), 0, 0)),
        pl.BlockSpec((None, None, None, 3, d), lambda g, k: (layer, old_tile(g) // per_batch, sub, 0, 0)),
        pl.BlockSpec((None, None, 1, d), lambda g, k: (layer, sub, 0, 0)),
    ]
    acc = [pltpu.VMEM((MIX_ROWS, d), F32)] * 2
    if sgu:
        zz, ln_g, ln_b, w_spatial, b_spatial = operands
        zz_chunks = zz.reshape(n_tiles * steps, chunk, 2 * SGU_WIDTH)
        vec = pl.BlockSpec((1, SGU_WIDTH), lambda g, k: (0, 0))
        lhs_specs = [
            pl.BlockSpec((None, chunk, SGU_WIDTH), lambda g, k: (new_chunk(g, k), 0, 0)),
            pl.BlockSpec((None, chunk, SGU_WIDTH), lambda g, k: (new_chunk(g, k), 0, 1)),
            vec, vec,
            pl.BlockSpec((SGU_GROUPS, SGU_CHUNK, SGU_CHUNK), lambda g, k: (0, 0, 0)),
            pl.BlockSpec((SGU_CHUNK, SGU_GROUPS), lambda g, k: (0, 0)),
        ]
        lhs_args = [zz_chunks, zz_chunks, ln_g.reshape(1, SGU_WIDTH), ln_b.reshape(1, SGU_WIDTH), w_spatial,
                    b_spatial.T]
        scratch = [pltpu.VMEM((steps, MIX_ROWS, half), BF16)] * 2 + acc
    else:
        y_a, y_b = operands
        tile = pl.BlockSpec((None, MIX_ROWS, half), lambda g, k: (mid_tile(g), 0, 0))
        lhs_specs = [tile, tile]
        lhs_args = [y_a.reshape(n_tiles, MIX_ROWS, half), y_b.reshape(n_tiles, MIX_ROWS, half)]
        scratch = acc
    out = pl.pallas_call(
        functools.partial(_mixer_out_body, sgu=sgu, n_tiles=n_tiles),
        grid=(n_tiles + 2, steps),
        in_specs=lhs_specs + common_specs,
        out_specs=pl.BlockSpec((None, chunk, d), out_chunk),
        out_shape=jax.ShapeDtypeStruct(x_chunks.shape, F32),
        scratch_shapes=scratch,
        compiler_params=_params("arbitrary", "arbitrary"),
        name="mixer_out",
    )(*lhs_args, w, x_chunks, mod, g_post)
    return out.reshape(b, s, d)


def _rope_body(pos_ref, freq_ref, cos_ref, sin_ref):
    ang = pos_ref[...].astype(F32) * freq_ref[...]
    lane = lax.broadcasted_iota(jnp.int32, ang.shape, 1)
    cos_ref[...] = jnp.cos(ang)
    sin_ref[...] = jnp.where(lane < ATT_HEAD_DIM // 2, -1.0, 1.0) * jnp.sin(ang)


def _rope_tables(positions):
    b, s = positions.shape
    half = ATT_HEAD_DIM // 2
    inv_freq = ROPE_THETA ** (-jnp.arange(half, dtype=F32) / half)
    freq = jnp.concatenate([inv_freq, inv_freq]).reshape(1, ATT_HEAD_DIM)
    tile = 512
    spec = pl.BlockSpec((None, tile, ATT_HEAD_DIM), lambda bi, i: (bi, i, 0))
    return pl.pallas_call(
        _rope_body,
        grid=(b, s // tile),
        in_specs=[
            pl.BlockSpec((None, tile, 1), lambda bi, i: (bi, i, 0)),
            pl.BlockSpec((1, ATT_HEAD_DIM), lambda bi, i: (0, 0)),
        ],
        out_specs=[spec, spec],
        out_shape=[jax.ShapeDtypeStruct((b, s, ATT_HEAD_DIM), F32)] * 2,
        compiler_params=_params("parallel", "parallel"),
        name="rope_tables",
    )(positions.reshape(b, s, 1), freq)


def _ssd_body(xs_ref, bm_ref, cm_ref, z_ref, dt_ref, cw_ref, cb_ref, dtb_ref, alog_ref, dskip_ref, ng_ref,
              o_ref, tail_ref, state_ref):
    L = SSD_CHUNK
    T = SSD_TAIL

    @pl.when(pl.program_id(1) == 0)
    def _():
        tail_ref[...] = jnp.zeros_like(tail_ref)
        state_ref[...] = jnp.zeros_like(state_ref)

    srow = lax.broadcasted_iota(jnp.int32, ((SSD_CONV - 1) * L, L + T), 0)
    scol = lax.broadcasted_iota(jnp.int32, ((SSD_CONV - 1) * L, L + T), 1)
    lag = (srow >> (L.bit_length() - 1)) + 1
    t_in = srow & (L - 1)
    shift = (scol == jnp.where(t_in >= lag, t_in - lag, t_in - lag + (L + T))).astype(BF16)

    def conv_silu(raw, lo, hi):
        lagged = _dot(shift, jnp.concatenate([raw, tail_ref[:, lo:hi]], axis=0))
        acc = cb_ref[:, lo:hi] + raw.astype(F32) * cw_ref[SSD_CONV - 1:SSD_CONV, lo:hi]
        for k in range(1, SSD_CONV):
            acc = acc + lagged[(k - 1) * L:k * L] * cw_ref[SSD_CONV - 1 - k:SSD_CONV - k, lo:hi]
        tail_ref[:, lo:hi] = raw[L - T:L]
        return _silu(acc)

    def one_chunk(rows):
        xs = conv_silu(xs_ref[rows, :], 0, SSD_WIDTH)
        bm = conv_silu(bm_ref[rows, :], SSD_WIDTH, SSD_WIDTH + SSD_BC_WIDTH)
        cm = conv_silu(cm_ref[rows, :], SSD_WIDTH + SSD_BC_WIDTH, SSD_CONV_CH)

        dt_in = dt_ref[rows, :] + dtb_ref[...]
        dt = jnp.maximum(dt_in, 0.0) + jnp.log1p(jnp.exp(-jnp.abs(dt_in)))
        adt = dt * (-jnp.exp(alog_ref[...]))
        row = lax.broadcasted_iota(jnp.int32, (L, L), 0)
        col = lax.broadcasted_iota(jnp.int32, (L, L), 1)
        causal = row >= col
        acs = _dot_exact(causal.astype(F32), adt)
        acs_t = acs.T
        dt_t = dt.T
        acs_last = acs[L - 1:L, :]

        hrow = lax.broadcasted_iota(jnp.int32, (LANES, SSD_WIDTH), 0)
        hcol = lax.broadcasted_iota(jnp.int32, (LANES, SSD_WIDTH), 1)
        expand = (hrow == (hcol >> (SSD_HEAD_DIM.bit_length() - 1))).astype(BF16)
        stacked = jnp.concatenate([jnp.exp(acs), jnp.exp(acs_last - acs) * dt], axis=0)
        high = stacked.astype(BF16)
        rest = (stacked - high.astype(F32)).astype(BF16)
        wide = _dot(high, expand) + _dot(rest, expand)
        decay_in_w, dt_decay_out_w = wide[0:L], wide[L:2 * L]
        chunk_decay_w = decay_in_w[L - 1:L, :]

        lane = lax.broadcasted_iota(jnp.int32, (L, LANES), 1)
        first_head = lane < SSD_HEAD_DIM

        y_parts = []
        for g in range(SSD_GROUPS):
            gs = slice(g * SSD_GROUP_WIDTH, (g + 1) * SSD_GROUP_WIDTH)
            bg = bm[:, g * SSD_STATE:(g + 1) * SSD_STATE]
            cg = cm[:, g * SSD_STATE:(g + 1) * SSD_STATE].astype(BF16)
            cb = _dot_nt(cg, bg.astype(BF16))
            state = state_ref[g]
            y_off = _dot(cg, state.astype(BF16)) * decay_in_w[:, gs]
            y_diag = []
            for j in range(SSD_HEADS_PER_GROUP // 2):
                h0 = g * SSD_HEADS_PER_GROUP + 2 * j
                ms = []
                for h in (h0, h0 + 1):
                    seg = acs[:, h:h + 1] - acs_t[h:h + 1, :]
                    ms.append(cb * jnp.exp(jnp.where(causal, seg, -jnp.inf)) * dt_t[h:h + 1, :])
                lhs = jnp.concatenate(ms, axis=1).astype(BF16)
                xp = xs[:, h0 * SSD_HEAD_DIM:(h0 + 2) * SSD_HEAD_DIM]
                rhs = jnp.concatenate([jnp.where(first_head, xp, 0.0), jnp.where(first_head, 0.0, xp)], axis=0)
                y_diag.append(_dot(lhs, rhs.astype(BF16)))
            y_parts.append(jnp.concatenate(y_diag, axis=1) + y_off)
            contrib = _dot(bg.T.astype(BF16), (xs[:, gs] * dt_decay_out_w[:, gs]).astype(BF16))
            state_ref[g] = state * chunk_decay_w[:, gs] + contrib

        y = jnp.concatenate(y_parts, axis=1) + xs * dskip_ref[...]
        y = y * _silu(z_ref[rows, :].astype(F32))
        o_ref[rows, :] = (_rms(y) * ng_ref[...]).astype(o_ref.dtype)

    for c in range(SSD_STEP_CHUNKS):
        one_chunk(slice(c * L, (c + 1) * L))


def _ssd_mixer(proj, dt_raw, conv_w, conv_b, dt_bias, a_log, d_skip, norm_g):
    b, s, _ = proj.shape
    assert proj.dtype == BF16
    L = SSD_CHUNK * SSD_STEP_CHUNKS
    z_blk = 0
    xs_blk = SSD_WIDTH // SSD_WIDTH
    bm_blk = (2 * SSD_WIDTH) // SSD_BC_WIDTH
    cm_blk = bm_blk + 1
    pad = LANES - SSD_HEADS
    small = lambda a: pl.BlockSpec(a.shape, lambda bi, c: (0, 0))
    dt_bias_p = jnp.pad(dt_bias, (0, pad)).reshape(1, LANES)
    a_log_p = jnp.pad(a_log, (0, pad)).reshape(1, LANES)
    d_skip_w = jnp.repeat(d_skip, SSD_HEAD_DIM).reshape(1, SSD_WIDTH)
    conv_b2 = conv_b.reshape(1, SSD_CONV_CH)
    norm_g2 = norm_g.reshape(1, SSD_WIDTH)
    return pl.pallas_call(
        _ssd_body,
        grid=(b, s // L),
        in_specs=[
            pl.BlockSpec((None, L, SSD_WIDTH), lambda bi, c: (bi, c, xs_blk)),
            pl.BlockSpec((None, L, SSD_BC_WIDTH), lambda bi, c: (bi, c, bm_blk)),
            pl.BlockSpec((None, L, SSD_BC_WIDTH), lambda bi, c: (bi, c, cm_blk)),
            pl.BlockSpec((None, L, SSD_WIDTH), lambda bi, c: (bi, c, z_blk)),
            pl.BlockSpec((None, L, LANES), lambda bi, c: (bi, c, 0)),
            small(conv_w), small(conv_b2), small(dt_bias_p), small(a_log_p), small(d_skip_w), small(norm_g2),
        ],
        out_specs=pl.BlockSpec((None, L, SSD_WIDTH), lambda bi, c: (bi, c, 0)),
        out_shape=jax.ShapeDtypeStruct((b, s, SSD_WIDTH), BF16),
        scratch_shapes=[
            pltpu.VMEM((SSD_TAIL, SSD_CONV_CH), BF16),
            pltpu.VMEM((SSD_GROUPS, SSD_STATE, SSD_GROUP_WIDTH), F32),
        ],
        compiler_params=_params("parallel", "arbitrary"),
        name="ssd_mixer",
    )(proj, proj, proj, proj, dt_raw, conv_w, conv_b2, dt_bias_p, a_log_p, d_skip_w, norm_g2)


def _attn_body(q_ref, k_ref, v_ref, cos_ref, sin_ref, o_ref, qkv_ref, acc_ref, m_ref, l_ref, *, seq):
    blk = ATT_BLOCK
    half = ATT_HEAD_DIM // 2
    dils = [d for _, d in DILATED_PATTERNS]
    step = dils[1]
    assert dils == [1, step, step * step] and all(w // d == blk for w, d in DILATED_PATTERNS)
    sub = seq // step
    assert seq // dils[2] == blk

    cos = cos_ref[...]
    sin = sin_ref[...]
    q = q_ref[...].astype(F32)
    k = k_ref[...].astype(F32)
    qkv_ref[0, 0] = (q * cos + pltpu.roll(q, half, 1) * sin) * (ATT_HEAD_DIM ** -0.5 * LOG2_E)
    qkv_ref[0, 1] = k * cos + pltpu.roll(k, half, 1) * sin
    qkv_ref[0, 2] = v_ref[...].astype(F32)
    for t in range(3):
        for r in range(step):
            qkv_ref[1, t, pl.ds(r * sub, sub), :] = qkv_ref[0, t, pl.ds(r, sub, stride=step), :]
    for t in range(3):
        for r in range(step):
            for a in range(step):
                qkv_ref[2, t, pl.ds((r + step * a) * blk, blk), :] = qkv_ref[1, t, pl.ds(r * sub + a, blk, stride=step), :]

    row = lax.broadcasted_iota(jnp.int32, (blk, blk), 0)
    col = lax.broadcasted_iota(jnp.int32, (blk, blk), 1)
    cur_ok = col <= row
    prev_ok = col >= row

    def attend(p, starts, with_prev):
        def load(t, st):
            return qkv_ref[p, t, pl.ds(st, blk), :].astype(BF16)

        def window(t, st):
            return jnp.concatenate([load(t, st - blk), load(t, st)], axis=0) if with_prev else load(t, st)

        qb = jnp.stack([load(0, st) for st in starts])
        kk = jnp.stack([window(1, st) for st in starts])
        keys = kk.shape[1]
        vv = jnp.stack([jnp.concatenate([window(2, st), jnp.ones((keys, LANES), BF16)], axis=1) for st in starts])
        ok = jnp.concatenate([prev_ok, cur_ok], axis=1) if with_prev else cur_ok
        s = lax.dot_general(qb, kk, (((2,), (2,)), ((0,), (0,))), preferred_element_type=F32)
        s = jnp.where(ok[None], s, -jnp.inf)
        m = jnp.max(s, axis=2, keepdims=True)
        e = jnp.exp2(s - m)
        acc = lax.dot_general(e.astype(BF16), vv, (((2,), (1,)), ((0,), (0,))), preferred_element_type=F32)
        for i, st in enumerate(starts):
            rows = pl.ds(st, blk)
            acc_ref[p, rows, :] = acc[i, :, 0:ATT_HEAD_DIM]
            l_ref[p, rows, :] = acc[i, :, ATT_HEAD_DIM:]
            m_ref[p, rows, :] = jnp.broadcast_to(m[i], (blk, LANES))

    def groups(starts, size):
        return [starts[i:i + size] for i in range(0, len(starts), size)]

    for p, dil in enumerate(dils):
        class_rows = seq // dil
        firsts = [r * class_rows for r in range(dil)]
        laters = [r * class_rows + n * blk for r in range(dil) for n in range(1, class_rows // blk)]
        for g in groups(firsts, ATT_GROUP_FIRST):
            attend(p, g, False)
        for g in groups(laters, ATT_GROUP_LATER):
            attend(p, g, True)

    def merged(dst, dst_rows, src, src_rows):
        m_a, m_b = m_ref[dst, dst_rows, :], m_ref[src, src_rows, :]
        top = jnp.maximum(m_a, m_b)
        w_a, w_b = jnp.exp2(m_a - top), jnp.exp2(m_b - top)
        acc = w_a * acc_ref[dst, dst_rows, :] + w_b * acc_ref[src, src_rows, :]
        return top, acc, w_a * l_ref[dst, dst_rows, :] + w_b * l_ref[src, src_rows, :]

    for r in range(step):
        for a in range(step):
            mid_rows = pl.ds(r * sub + a, blk, stride=step)
            top, acc, l = merged(1, mid_rows, 2, pl.ds((r + step * a) * blk, blk))
            m_ref[1, mid_rows, :] = top
            acc_ref[1, mid_rows, :] = acc
            l_ref[1, mid_rows, :] = l
    for r in range(step):
        for n in range(sub // blk):
            nat_rows = pl.ds(r + n * blk * step, blk, stride=step)
            _, acc, l = merged(0, nat_rows, 1, pl.ds(r * sub + n * blk, blk))
            acc_ref[0, nat_rows, :] = acc / l
    o_ref[...] = acc_ref[0].astype(o_ref.dtype)


def _dilated_attention(proj, cos2, sin2):
    b, s, _ = proj.shape
    q_blk = (2 * SSD_WIDTH + 2 * SSD_BC_WIDTH) // ATT_HEAD_DIM
    k_blk = q_blk + ATT_HEADS
    v_blk = k_blk + ATT_HEADS
    n_pat = len(DILATED_PATTERNS)
    head = lambda base: pl.BlockSpec((None, s, ATT_HEAD_DIM), lambda bi, h: (bi, 0, base + h))
    table = pl.BlockSpec((None, s, ATT_HEAD_DIM), lambda bi, h: (bi, 0, 0))
    return pl.pallas_call(
        functools.partial(_attn_body, seq=s),
        grid=(b, ATT_HEADS),
        in_specs=[head(q_blk), head(k_blk), head(v_blk), table, table],
        out_specs=pl.BlockSpec((None, s, ATT_HEAD_DIM), lambda bi, h: (bi, 0, h)),
        out_shape=jax.ShapeDtypeStruct((b, s, ATT_WIDTH), BF16),
        scratch_shapes=[
            pltpu.VMEM((n_pat, 3, s, ATT_HEAD_DIM), F32),
            pltpu.VMEM((n_pat, s, ATT_HEAD_DIM), F32),
            pltpu.VMEM((n_pat, s, LANES), F32),
            pltpu.VMEM((n_pat, s, LANES), F32),
        ],
        compiler_params=_params("parallel", "arbitrary"),
        name="dilated_attention",
    )(proj, proj, proj, cos2, sin2)


def kernel(x, c, positions, w_mod, b_mod, norm_pre, norm_post, ffn_w_gate, ffn_w_up, ffn_w_down, hyb_w_in, hyb_conv_w, hyb_conv_b, hyb_dt_bias, hyb_a_log, hyb_d_skip, hyb_norm_g, hyb_w_out, sgu_w_in, sgu_b_in, sgu_ln_g, sgu_ln_b, sgu_w_spatial, sgu_b_spatial, sgu_w_out):
    depth = w_mod.shape[0]
    b, s, d = x.shape
    n_sub = norm_pre.shape[1]

    mod = _modulation(c, w_mod, b_mod).reshape(depth, b, n_sub, 3, d)
    g_pre = norm_pre.reshape(depth, n_sub, 1, d)
    g_post = norm_post.reshape(depth, n_sub, 1, d)
    w_gate = ffn_w_gate.astype(BF16)
    w_up = ffn_w_up.astype(BF16)
    w_down = ffn_w_down.astype(BF16)

    for layer in range(depth):
        i = layer // 2
        x = _ffn_sublayer(x, mod, g_pre, g_post, w_gate, w_up, w_down, layer, 0, 0)
        if layer % 2 == 0:
            w_in = hyb_w_in[i]
            dt_lo = SSD_WIDTH + SSD_CONV_CH
            dt_hi = dt_lo + SSD_HEADS
            w_dt = jnp.pad(w_in[:, dt_lo:dt_hi], ((0, 0), (0, LANES - SSD_HEADS))).astype(BF16)
            proj, dt_raw = _in_proj(x, mod, g_pre, (w_in[:, :dt_lo].astype(BF16), w_in[:, dt_hi:].astype(BF16), w_dt),
                                    layer, 1, hyb=True, rows=PROJ_ROWS_HYB, tile=PROJ_TILE)
            cos2, sin2 = _rope_tables(positions)
            y_a = _ssd_mixer(proj, dt_raw, hyb_conv_w[i], hyb_conv_b[i], hyb_dt_bias[i], hyb_a_log[i],
                             hyb_d_skip[i], hyb_norm_g[i])
            y_b = _dilated_attention(proj, cos2, sin2)
            x = _mixer_out((y_a, y_b), hyb_w_out[i].astype(BF16), x, mod, g_post, layer, 1, sgu=False)
        else:
            zz, = _in_proj(x, mod, g_pre, (sgu_w_in[i].astype(BF16), sgu_b_in[i].reshape(1, -1)), layer, 1,
                           hyb=False, rows=PROJ_ROWS_GELU, tile=2 * PROJ_TILE)
            x = _mixer_out((zz, sgu_ln_g[i], sgu_ln_b[i], sgu_w_spatial[i], sgu_b_spatial[i]),
                           sgu_w_out[i].astype(BF16), x, mod, g_post, layer, 1, sgu=True)
        x = _ffn_sublayer(x, mod, g_pre, g_post, w_gate, w_up, w_down, layer, 2, 1)
    return x
```

```python
import functools

import jax
import jax.numpy as jnp
from jax import lax
from jax.experimental import pallas as pl
from jax.experimental.pallas import tpu as pltpu

NORM_EPS = 1e-6
LOG2_E = 1.4426950408889634
FFN_RES_WEIGHT = 0.5
MIXER_RES_WEIGHT = 1.0

SSD_HEADS = 32
SSD_HEAD_DIM = 64
SSD_WIDTH = SSD_HEADS * SSD_HEAD_DIM
SSD_GROUPS = 4
SSD_STATE = 128
SSD_CONV = 4
SSD_CHUNK = 128
SSD_TAIL = 16
SSD_STEP_CHUNKS = 4
SSD_BC_WIDTH = SSD_GROUPS * SSD_STATE
SSD_CONV_CH = SSD_WIDTH + 2 * SSD_BC_WIDTH
SSD_HEADS_PER_GROUP = SSD_HEADS // SSD_GROUPS
SSD_GROUP_WIDTH = SSD_HEADS_PER_GROUP * SSD_HEAD_DIM

ATT_HEADS = 16
ATT_HEAD_DIM = 128
ATT_WIDTH = ATT_HEADS * ATT_HEAD_DIM
ATT_BLOCK = 128
DILATED_PATTERNS = ((128, 1), (512, 4), (2048, 16))
ROPE_THETA = 10000.0

SGU_WIDTH = 4096
SGU_GROUPS = 8
SGU_CHUNK = 128
SGU_GROUP_WIDTH = SGU_WIDTH // SGU_GROUPS

LANES = 128
SUBLANES = 8
VMEM_LIMIT_BYTES = 56 * 1024 * 1024

MIX_ROWS = 512
MIX_STEPS = 2
FFN_ROWS = 1024
FFN_TILE = 512
FFN_NORM_CHUNKS = 8
NORM_ROWS = 16
PROJ_ROWS_HYB = 1024
PROJ_ROWS_GELU = 1024
PROJ_TILE = 1024
PROJ_TILE_GELU = 2048
PROJ_BLOCK_HYB = 512
PROJ_BLOCK_GELU = 256
PROJ_NORM_CHUNKS = 8
MOD_TILE = 2048
ATT_GROUP_FIRST = 16
ATT_GROUP_LATER = 15

BF16 = jnp.bfloat16
F32 = jnp.float32


def _params(*semantics):
    return pltpu.CompilerParams(dimension_semantics=semantics, vmem_limit_bytes=VMEM_LIMIT_BYTES)


def _rms(x):
    return x * lax.rsqrt(jnp.mean(x * x, axis=-1, keepdims=True) + NORM_EPS)


def _silu(x):
    h = 0.5 * x
    return h + h * jnp.tanh(h)


def _dot(a, b):
    return jnp.dot(a, b, preferred_element_type=F32)


def _dot_exact(a, b):
    return jnp.dot(a, b, preferred_element_type=F32, precision=lax.Precision.HIGHEST)


def _zero_after(done, never, width):
    total = jnp.zeros((SUBLANES, LANES), F32)
    for res in done:
        for r in range(0, res.shape[0], SUBLANES):
            for c in range(0, res.shape[1], LANES):
                total = total + res[r:r + SUBLANES, c:c + LANES]
    row = jnp.sum(jnp.where(never, total, 0.0), axis=0, keepdims=True)
    return jnp.tile(row, (1, width // LANES))


def _dot_nt(a, b):
    return lax.dot_general(a, b, (((1,), (1,)), ((), ())), preferred_element_type=F32)


def _mod_body(c_ref, w_ref, b_ref, o_ref):
    ca = _silu(c_ref[...]).astype(BF16)
    o_ref[...] = _dot(ca, w_ref[...].astype(BF16)) + b_ref[...]


def _modulation(c, w_mod, b_mod):
    depth, d, n = w_mod.shape
    b = c.shape[0]
    return pl.pallas_call(
        _mod_body,
        grid=(depth, n // MOD_TILE),
        in_specs=[
            pl.BlockSpec((b, d), lambda l, j: (0, 0)),
            pl.BlockSpec((None, d, MOD_TILE), lambda l, j: (l, 0, j)),
            pl.BlockSpec((None, 1, MOD_TILE), lambda l, j: (l, 0, j)),
        ],
        out_specs=pl.BlockSpec((None, b, MOD_TILE), lambda l, j: (l, 0, j)),
        out_shape=jax.ShapeDtypeStruct((depth, b, n), F32),
        compiler_params=_params("parallel", "parallel"),
        name="modulation",
    )(c, w_mod, b_mod.reshape(depth, 1, n))


def _ffn_body(xn_ref, xp_ref, modn_ref, modp_ref, gpre_ref, gpost_ref, wg_ref, wu_ref, wd_ref, o_ref,
              h0_ref, h1_ref, acc0_ref, acc1_ref, *, n_tiles, n_f):
    g = pl.program_id(0)
    f = pl.program_id(1)
    chunk = FFN_ROWS // FFN_NORM_CHUNKS
    c0 = jnp.minimum(f, FFN_NORM_CHUNKS - 1) * chunk

    @pl.when((g == 0) & (f == 0))
    def _():
        for ref in (h0_ref, h1_ref, acc0_ref, acc1_ref):
            ref[...] = jnp.zeros_like(ref)

    def pre_norm_chunk(h_ref):
        gain = gpre_ref[...] * (1.0 + modn_ref[1:2, :])
        done = []
        for q in range(chunk // NORM_ROWS):
            rows = pl.ds(q * NORM_ROWS, NORM_ROWS)
            tile_rows = pl.ds(pl.multiple_of(c0 + q * NORM_ROWS, NORM_ROWS), NORM_ROWS)
            res = _rms(xn_ref[rows, :]) * gain + modn_ref[0:1, :]
            h_ref[tile_rows, :] = res.astype(BF16)
            done.append(res)
        return done

    def post_norm_chunk(acc_ref):
        gain = (FFN_RES_WEIGHT * (1.0 + modp_ref[2:3, :])) * gpost_ref[...]
        done = []
        for q in range(chunk // NORM_ROWS):
            rows = pl.ds(q * NORM_ROWS, NORM_ROWS)
            tile_rows = pl.ds(pl.multiple_of(c0 + q * NORM_ROWS, NORM_ROWS), NORM_ROWS)
            res = xp_ref[rows, :] + _rms(acc_ref[tile_rows, :]) * gain
            o_ref[rows, :] = res
            done.append(res)
        return done

    def zero_after(done):
        return _zero_after(done, f < 0, FFN_TILE)

    def swiglu_step(h_ref, acc_ref, zero):
        h = h_ref[...]
        a = (_silu(_dot(h, wg_ref[...])) * _dot(h, wu_ref[...]) + zero).astype(BF16)
        acc_ref[...] = jnp.where(f > 0, acc_ref[...], 0.0) + _dot(a, wd_ref[...])

    has_matmul = (g >= 1) & (g <= n_tiles)
    for parity, (h_new, acc_old, h_mid, acc_mid) in enumerate(
            [(h0_ref, acc0_ref, h1_ref, acc1_ref), (h1_ref, acc1_ref, h0_ref, acc0_ref)]):
        @pl.when((lax.rem(g, 2) == parity) & has_matmul)
        def _():
            swiglu_step(h_mid, acc_mid, zero_after(post_norm_chunk(acc_old) + pre_norm_chunk(h_new)))

        @pl.when((lax.rem(g, 2) == parity) & (g == 0) & (f < FFN_NORM_CHUNKS))
        def _():
            pre_norm_chunk(h_new)

        @pl.when((lax.rem(g, 2) == parity) & (g == n_tiles + 1) & (f < FFN_NORM_CHUNKS))
        def _():
            post_norm_chunk(acc_old)


def _ffn_sublayer(x, mod, g_pre, g_post, w_gate, w_up, w_down, layer, sub, idx):
    b, s, d = x.shape
    f_dim = w_gate.shape[-1]
    n_f = f_dim // FFN_TILE
    per_batch = s // FFN_ROWS
    n_tiles = b * per_batch
    chunks = FFN_NORM_CHUNKS
    chunk = FFN_ROWS // chunks
    assert n_f >= chunks
    new_tile = lambda g: jnp.minimum(g, n_tiles - 1)
    old_tile = lambda g: jnp.clip(g - 2, 0, n_tiles - 1)
    new_chunk = lambda g, f: (new_tile(g) * chunks + jnp.minimum(f, chunks - 1), 0, 0)
    old_chunk = lambda g, f: (old_tile(g) * chunks + jnp.minimum(f, chunks - 1), 0, 0)
    out_chunk = lambda g, f: (jnp.where(g < 2, 0, old_chunk(g, f)[0]), 0, 0)
    w_step = lambda g, f: jnp.where(g == 0, 0, jnp.where(g == n_tiles + 1, n_f - 1, f))
    x_chunks = x.reshape(n_tiles * chunks, chunk, d)
    out = pl.pallas_call(
        functools.partial(_ffn_body, n_tiles=n_tiles, n_f=n_f),
        grid=(n_tiles + 2, n_f),
        in_specs=[
            pl.BlockSpec((None, chunk, d), new_chunk),
            pl.BlockSpec((None, chunk, d), old_chunk),
            pl.BlockSpec((None, None, None, 3, d), lambda g, f: (layer, new_tile(g) // per_batch, sub, 0, 0)),
            pl.BlockSpec((None, None, None, 3, d), lambda g, f: (layer, old_tile(g) // per_batch, sub, 0, 0)),
            pl.BlockSpec((None, None, 1, d), lambda g, f: (layer, sub, 0, 0)),
            pl.BlockSpec((None, None, 1, d), lambda g, f: (layer, sub, 0, 0)),
            pl.BlockSpec((None, None, d, FFN_TILE), lambda g, f: (layer, idx, 0, w_step(g, f))),
            pl.BlockSpec((None, None, d, FFN_TILE), lambda g, f: (layer, idx, 0, w_step(g, f))),
            pl.BlockSpec((None, None, FFN_TILE, d), lambda g, f: (layer, idx, w_step(g, f), 0)),
        ],
        out_specs=pl.BlockSpec((None, chunk, d), out_chunk),
        out_shape=jax.ShapeDtypeStruct(x_chunks.shape, F32),
        scratch_shapes=[pltpu.VMEM((FFN_ROWS, d), BF16), pltpu.VMEM((FFN_ROWS, d), BF16),
                        pltpu.VMEM((FFN_ROWS, d), F32), pltpu.VMEM((FFN_ROWS, d), F32)],
        compiler_params=_params("arbitrary", "arbitrary"),
        name="ffn_sublayer",
    )(x_chunks, x_chunks, mod, mod, g_pre, g_post, w_gate, w_up, w_down)
    return out.reshape(b, s, d)


def _gelu_tanh(x):
    return 0.5 * x * (1.0 + jnp.tanh(0.7978845608028654 * (x + 0.044715 * (x * x * x))))


def _in_proj_body(*refs, hyb, n_a, chunks):
    if hyb:
        xn_ref, modn_ref, gpre_ref, wa_ref, wb_ref, wdt_ref, o_ref, dt_ref, h0_ref, h1_ref = refs
    else:
        xn_ref, modn_ref, gpre_ref, wa_ref, bias_ref, o_ref, h0_ref, h1_ref = refs
    g = pl.program_id(0)
    j = pl.program_id(1)
    chunk = h0_ref.shape[0] // chunks
    c0 = jnp.minimum(j, chunks - 1) * chunk

    @pl.when((g == 0) & (j == 0))
    def _():
        h0_ref[...] = jnp.zeros_like(h0_ref)
        h1_ref[...] = jnp.zeros_like(h1_ref)

    def pre_norm_chunk(h_ref):
        gain = gpre_ref[...] * (1.0 + modn_ref[1:2, :])
        done = []
        for q in range(chunk // NORM_ROWS):
            rows = pl.ds(q * NORM_ROWS, NORM_ROWS)
            tile_rows = pl.ds(pl.multiple_of(c0 + q * NORM_ROWS, NORM_ROWS), NORM_ROWS)
            res = _rms(xn_ref[rows, :]) * gain + modn_ref[0:1, :]
            h_ref[tile_rows, :] = res.astype(BF16)
            done.append(res)
        return done

    def project(h_ref, w_ref, h_new):
        h = h_ref[...]
        later = h + _zero_after(pre_norm_chunk(h_new), j < 0, h.shape[1]).astype(BF16)
        block = PROJ_BLOCK_HYB if hyb else PROJ_BLOCK_GELU
        for t in range(o_ref.shape[1] // block):
            cols = slice(t * block, (t + 1) * block)
            y = _dot(h if t == 0 else later, w_ref[:, cols])
            if not hyb:
                y = _gelu_tanh(y + bias_ref[:, cols])
            o_ref[:, cols] = y.astype(o_ref.dtype)

    has_matmul = g >= 1
    for parity, (h_new, h_mid) in enumerate([(h0_ref, h1_ref), (h1_ref, h0_ref)]):
        mine = lax.rem(g, 2) == parity
        if hyb:
            @pl.when(mine & has_matmul & (j == 0))
            def _():
                project(h_mid, wa_ref, h_new)
                dt_ref[...] = _dot(h_mid[...], wdt_ref[...])

            @pl.when(mine & has_matmul & (j > 0) & (j < n_a))
            def _():
                project(h_mid, wa_ref, h_new)

            @pl.when(mine & has_matmul & (j >= n_a))
            def _():
                project(h_mid, wb_ref, h_new)
        else:
            @pl.when(mine & has_matmul)
            def _():
                project(h_mid, wa_ref, h_new)

        @pl.when(mine & jnp.logical_not(has_matmul) & (j < chunks))
        def _():
            pre_norm_chunk(h_new)


def _in_proj(x, mod, g_pre, weights, layer, sub, hyb, rows, tile):
    b, s, d = x.shape
    per_batch = s // rows
    n_tiles = b * per_batch
    w_a = weights[0]
    n_a = w_a.shape[1] // tile
    n_steps = n_a + (weights[1].shape[1] // tile if hyb else 0)
    chunks = min(PROJ_NORM_CHUNKS, n_steps)
    chunk = rows // chunks
    new_tile = lambda g: jnp.minimum(g, n_tiles - 1)
    mid_tile = lambda g: jnp.maximum(g - 1, 0)
    col = lambda g, j: jnp.where(g == 0, 0, j)
    in_specs = [
        pl.BlockSpec((None, chunk, d), lambda g, j: (new_tile(g) * chunks + jnp.minimum(j, chunks - 1), 0, 0)),
        pl.BlockSpec((None, None, None, 3, d), lambda g, j: (layer, new_tile(g) // per_batch, sub, 0, 0)),
        pl.BlockSpec((None, None, 1, d), lambda g, j: (layer, sub, 0, 0)),
        pl.BlockSpec((d, tile), lambda g, j: (0, jnp.minimum(col(g, j), n_a - 1))),
    ]
    out_specs = [pl.BlockSpec((None, rows, tile), lambda g, j: (mid_tile(g), 0, col(g, j)))]
    out_shape = [jax.ShapeDtypeStruct((n_tiles, rows, n_steps * tile), BF16)]
    if hyb:
        in_specs += [
            pl.BlockSpec((d, tile), lambda g, j: (0, jnp.maximum(col(g, j) - n_a, 0))),
            pl.BlockSpec((d, LANES), lambda g, j: (0, 0)),
        ]
        out_specs.append(pl.BlockSpec((None, rows, LANES), lambda g, j: (mid_tile(g), 0, 0)))
        out_shape.append(jax.ShapeDtypeStruct((n_tiles, rows, LANES), F32))
    else:
        in_specs.append(pl.BlockSpec((1, tile), lambda g, j: (0, col(g, j))))
    outs = pl.pallas_call(
        functools.partial(_in_proj_body, hyb=hyb, n_a=n_a, chunks=chunks),
        grid=(n_tiles + 1, n_steps),
        in_specs=in_specs,
        out_specs=out_specs,
        out_shape=out_shape,
        scratch_shapes=[pltpu.VMEM((rows, d), BF16)] * 2,
        compiler_params=_params("arbitrary", "arbitrary"),
        name="in_proj",
    )(x.reshape(n_tiles * chunks, chunk, d), mod, g_pre, *weights)
    return [o.reshape(b, s, o.shape[-1]) for o in outs]


def _sgu_gate_chunk(u, v, lng_ref, lnb_ref, ws_ref, bs_ref):
    L = SGU_CHUNK
    v = v.astype(F32)
    mu = jnp.mean(v, axis=-1, keepdims=True)
    vc = v - mu
    var = jnp.mean(vc * vc, axis=-1, keepdims=True)
    vn = (vc * lax.rsqrt(var + NORM_EPS) * lng_ref[...] + lnb_ref[...]).astype(BF16)
    row = lax.broadcasted_iota(jnp.int32, (L, L), 0)
    col = lax.broadcasted_iota(jnp.int32, (L, L), 1)
    causal = row >= col
    parts = []
    for g in range(SGU_GROUPS):
        gs = slice(g * SGU_GROUP_WIDTH, (g + 1) * SGU_GROUP_WIDTH)
        w = jnp.where(causal, ws_ref[g], 0.0).astype(BF16)
        mixed = _dot(w, vn[:, gs]) + bs_ref[:, g:g + 1]
        parts.append((u[:, gs].astype(F32) * mixed).astype(BF16))
    return jnp.concatenate(parts, axis=1)


def _mixer_out_body(*refs, sgu, n_tiles):
    if sgu:
        (u_ref, v_ref, lng_ref, lnb_ref, ws_ref, bs_ref, w_ref, xp_ref, modp_ref, gpost_ref, o_ref,
         lhs0_ref, lhs1_ref, acc0_ref, acc1_ref) = refs
    else:
        a_ref, b_ref, w_ref, xp_ref, modp_ref, gpost_ref, o_ref, acc0_ref, acc1_ref = refs
        lhs0_ref = lhs1_ref = None
    g = pl.program_id(0)
    k = pl.program_id(1)
    chunk = MIX_ROWS // MIX_STEPS
    half = w_ref.shape[0] // MIX_STEPS

    @pl.when((g == 0) & (k == 0))
    def _():
        for ref in (lhs0_ref, lhs1_ref, acc0_ref, acc1_ref):
            if ref is not None:
                ref[...] = jnp.zeros_like(ref)

    def post_norm_chunk(acc_ref):
        gain = (MIXER_RES_WEIGHT * (1.0 + modp_ref[2:3, :])) * gpost_ref[...]
        done = []
        for q in range(chunk // NORM_ROWS):
            rows = pl.ds(q * NORM_ROWS, NORM_ROWS)
            tile_rows = pl.ds(pl.multiple_of(k * chunk + q * NORM_ROWS, NORM_ROWS), NORM_ROWS)
            res = xp_ref[rows, :] + _rms(acc_ref[tile_rows, :]) * gain
            o_ref[rows, :] = res
            done.append(res)
        return done

    def gate_chunk(lhs_ref):
        for j in range(chunk // SGU_CHUNK):
            rows = slice(j * SGU_CHUNK, (j + 1) * SGU_CHUNK)
            gated = _sgu_gate_chunk(u_ref[rows, :], v_ref[rows, :], lng_ref, lnb_ref, ws_ref, bs_ref)
            tile_rows = pl.ds(pl.multiple_of(k * chunk + j * SGU_CHUNK, SGU_CHUNK), SGU_CHUNK)
            for part in range(MIX_STEPS):
                lhs_ref[part, tile_rows, :] = gated[:, part * half:(part + 1) * half]

    def matmul_step(lhs_ref, acc_ref, done):
        lhs = lhs_ref[k] if sgu else jnp.where(k == 0, a_ref[...], b_ref[...])
        later = lhs + _zero_after(done, k < 0, half).astype(BF16)
        rows = pl.ds(pl.multiple_of(k * half, half), half)
        mid = acc_ref.shape[1] // 2
        for cols, operand in ((slice(0, mid), lhs), (slice(mid, 2 * mid), later)):
            acc_ref[:, cols] = jnp.where(k > 0, acc_ref[:, cols], 0.0) + _dot(operand, w_ref[rows, cols])

    has_matmul = (g >= 1) & (g <= n_tiles)
    for parity, (lhs_new, acc_old, lhs_mid, acc_mid) in enumerate(
            [(lhs0_ref, acc0_ref, lhs1_ref, acc1_ref), (lhs1_ref, acc1_ref, lhs0_ref, acc0_ref)]):
        @pl.when((lax.rem(g, 2) == parity) & has_matmul)
        def _():
            matmul_step(lhs_mid, acc_mid, post_norm_chunk(acc_old))
            if sgu:
                gate_chunk(lhs_new)

        if sgu:
            @pl.when((lax.rem(g, 2) == parity) & (g == 0))
            def _():
                gate_chunk(lhs_new)

        @pl.when((lax.rem(g, 2) == parity) & (g == n_tiles + 1))
        def _():
            post_norm_chunk(acc_old)


def _mixer_out(operands, w, x, mod, g_post, layer, sub, sgu):
    b, s, d = x.shape
    half = w.shape[0] // MIX_STEPS
    per_batch = s // MIX_ROWS
    n_tiles = b * per_batch
    steps = MIX_STEPS
    chunk = MIX_ROWS // steps
    new_tile = lambda g: jnp.minimum(g, n_tiles - 1)
    mid_tile = lambda g: jnp.clip(g - 1, 0, n_tiles - 1)
    old_tile = lambda g: jnp.clip(g - 2, 0, n_tiles - 1)
    new_chunk = lambda g, k: new_tile(g) * steps + k
    old_chunk = lambda g, k: old_tile(g) * steps + k
    out_chunk = lambda g, k: (jnp.where(g < 2, 0, old_chunk(g, k)), 0, 0)
    x_chunks = x.reshape(n_tiles * steps, chunk, d)
    common_specs = [
        pl.BlockSpec((steps * half, d), lambda g, k: (0, 0), pipeline_mode=pl.Buffered(1)),
        pl.BlockSpec((None, chunk, d), lambda g, k: (old_chunk(g, k), 0, 0)),
        pl.BlockSpec((None, None, None, 3, d), lambda g, k: (layer, old_tile(g) // per_batch, sub, 0, 0)),
        pl.BlockSpec((None, None, 1, d), lambda g, k: (layer, sub, 0, 0)),
    ]
    acc = [pltpu.VMEM((MIX_ROWS, d), F32)] * 2
    if sgu:
        zz, ln_g, ln_b, w_spatial, b_spatial = operands
        zz_chunks = zz.reshape(n_tiles * steps, chunk, 2 * SGU_WIDTH)
        vec = pl.BlockSpec((1, SGU_WIDTH), lambda g, k: (0, 0))
        lhs_specs = [
            pl.BlockSpec((None, chunk, SGU_WIDTH), lambda g, k: (new_chunk(g, k), 0, 0)),
            pl.BlockSpec((None, chunk, SGU_WIDTH), lambda g, k: (new_chunk(g, k), 0, 1)),
            vec, vec,
            pl.BlockSpec((SGU_GROUPS, SGU_CHUNK, SGU_CHUNK), lambda g, k: (0, 0, 0)),
            pl.BlockSpec((SGU_CHUNK, SGU_GROUPS), lambda g, k: (0, 0)),
        ]
        lhs_args = [zz_chunks, zz_chunks, ln_g.reshape(1, SGU_WIDTH), ln_b.reshape(1, SGU_WIDTH), w_spatial,
                    b_spatial.T]
        scratch = [pltpu.VMEM((steps, MIX_ROWS, half), BF16)] * 2 + acc
    else:
        y_a, y_b = operands
        tile = pl.BlockSpec((None, MIX_ROWS, half), lambda g, k: (mid_tile(g), 0, 0))
        lhs_specs = [tile, tile]
        lhs_args = [y_a.reshape(n_tiles, MIX_ROWS, half), y_b.reshape(n_tiles, MIX_ROWS, half)]
        scratch = acc
    out = pl.pallas_call(
        functools.partial(_mixer_out_body, sgu=sgu, n_tiles=n_tiles),
        grid=(n_tiles + 2, steps),
        in_specs=lhs_specs + common_specs,
        out_specs=pl.BlockSpec((None, chunk, d), out_chunk),
        out_shape=jax.ShapeDtypeStruct(x_chunks.shape, F32),
        scratch_shapes=scratch,
        compiler_params=_params("arbitrary", "arbitrary"),
        name="mixer_out",
    )(*lhs_args, w, x_chunks, mod, g_post)
    return out.reshape(b, s, d)


def _rope_body(pos_ref, freq_ref, cos_ref, sin_ref):
    ang = pos_ref[...].astype(F32) * freq_ref[...]
    lane = lax.broadcasted_iota(jnp.int32, ang.shape, 1)
    cos_ref[...] = jnp.cos(ang)
    sin_ref[...] = jnp.where(lane < ATT_HEAD_DIM // 2, -1.0, 1.0) * jnp.sin(ang)


def _rope_tables(positions):
    b, s = positions.shape
    half = ATT_HEAD_DIM // 2
    inv_freq = ROPE_THETA ** (-jnp.arange(half, dtype=F32) / half)
    freq = jnp.concatenate([inv_freq, inv_freq]).reshape(1, ATT_HEAD_DIM)
    tile = 512
    spec = pl.BlockSpec((None, tile, ATT_HEAD_DIM), lambda bi, i: (bi, i, 0))
    return pl.pallas_call(
        _rope_body,
        grid=(b, s // tile),
        in_specs=[
            pl.BlockSpec((None, tile, 1), lambda bi, i: (bi, i, 0)),
            pl.BlockSpec((1, ATT_HEAD_DIM), lambda bi, i: (0, 0)),
        ],
        out_specs=[spec, spec],
        out_shape=[jax.ShapeDtypeStruct((b, s, ATT_HEAD_DIM), F32)] * 2,
        compiler_params=_params("parallel", "parallel"),
        name="rope_tables",
    )(positions.reshape(b, s, 1), freq)


def _ssd_body(xs_ref, bm_ref, cm_ref, z_ref, dt_ref, cw_ref, cb_ref, dtb_ref, alog_ref, dskip_ref, ng_ref,
              o_ref, tail_ref, state_ref):
    L = SSD_CHUNK
    T = SSD_TAIL

    @pl.when(pl.program_id(1) == 0)
    def _():
        tail_ref[...] = jnp.zeros_like(tail_ref)
        state_ref[...] = jnp.zeros_like(state_ref)

    srow = lax.broadcasted_iota(jnp.int32, ((SSD_CONV - 1) * L, L + T), 0)
    scol = lax.broadcasted_iota(jnp.int32, ((SSD_CONV - 1) * L, L + T), 1)
    lag = (srow >> (L.bit_length() - 1)) + 1
    t_in = srow & (L - 1)
    shift = (scol == jnp.where(t_in >= lag, t_in - lag, t_in - lag + (L + T))).astype(BF16)

    def conv_silu(raw, lo, hi):
        lagged = _dot(shift, jnp.concatenate([raw, tail_ref[:, lo:hi]], axis=0))
        acc = cb_ref[:, lo:hi] + raw.astype(F32) * cw_ref[SSD_CONV - 1:SSD_CONV, lo:hi]
        for k in range(1, SSD_CONV):
            acc = acc + lagged[(k - 1) * L:k * L] * cw_ref[SSD_CONV - 1 - k:SSD_CONV - k, lo:hi]
        tail_ref[:, lo:hi] = raw[L - T:L]
        return _silu(acc)

    def one_chunk(rows):
        xs = conv_silu(xs_ref[rows, :], 0, SSD_WIDTH)
        bm = conv_silu(bm_ref[rows, :], SSD_WIDTH, SSD_WIDTH + SSD_BC_WIDTH)
        cm = conv_silu(cm_ref[rows, :], SSD_WIDTH + SSD_BC_WIDTH, SSD_CONV_CH)

        dt_in = dt_ref[rows, :] + dtb_ref[...]
        dt = jnp.maximum(dt_in, 0.0) + jnp.log1p(jnp.exp(-jnp.abs(dt_in)))
        adt = dt * (-jnp.exp(alog_ref[...]))
        row = lax.broadcasted_iota(jnp.int32, (L, L), 0)
        col = lax.broadcasted_iota(jnp.int32, (L, L), 1)
        causal = row >= col
        acs = _dot_exact(causal.astype(F32), adt)
        acs_t = acs.T
        dt_t = dt.T
        acs_last = acs[L - 1:L, :]

        hrow = lax.broadcasted_iota(jnp.int32, (LANES, SSD_WIDTH), 0)
        hcol = lax.broadcasted_iota(jnp.int32, (LANES, SSD_WIDTH), 1)
        expand = (hrow == (hcol >> (SSD_HEAD_DIM.bit_length() - 1))).astype(BF16)
        stacked = jnp.concatenate([jnp.exp(acs), jnp.exp(acs_last - acs) * dt], axis=0)
        high = stacked.astype(BF16)
        rest = (stacked - high.astype(F32)).astype(BF16)
        wide = _dot(high, expand) + _dot(rest, expand)
        decay_in_w, dt_decay_out_w = wide[0:L], wide[L:2 * L]
        chunk_decay_w = decay_in_w[L - 1:L, :]

        lane = lax.broadcasted_iota(jnp.int32, (L, LANES), 1)
        first_head = lane < SSD_HEAD_DIM

        y_parts = []
        for g in range(SSD_GROUPS):
            gs = slice(g * SSD_GROUP_WIDTH, (g + 1) * SSD_GROUP_WIDTH)
            bg = bm[:, g * SSD_STATE:(g + 1) * SSD_STATE]
            cg = cm[:, g * SSD_STATE:(g + 1) * SSD_STATE].astype(BF16)
            cb = _dot_nt(cg, bg.astype(BF16))
            state = state_ref[g]
            y_off = _dot(cg, state.astype(BF16)) * decay_in_w[:, gs]
            y_diag = []
            for j in range(SSD_HEADS_PER_GROUP // 2):
                h0 = g * SSD_HEADS_PER_GROUP + 2 * j
                ms = []
                for h in (h0, h0 + 1):
                    seg = acs[:, h:h + 1] - acs_t[h:h + 1, :]
                    ms.append(cb * jnp.exp(jnp.where(causal, seg, -jnp.inf)) * dt_t[h:h + 1, :])
                lhs = jnp.concatenate(ms, axis=1).astype(BF16)
                xp = xs[:, h0 * SSD_HEAD_DIM:(h0 + 2) * SSD_HEAD_DIM]
                rhs = jnp.concatenate([jnp.where(first_head, xp, 0.0), jnp.where(first_head, 0.0, xp)], axis=0)
                y_diag.append(_dot(lhs, rhs.astype(BF16)))
            y_parts.append(jnp.concatenate(y_diag, axis=1) + y_off)
            contrib = _dot(bg.T.astype(BF16), (xs[:, gs] * dt_decay_out_w[:, gs]).astype(BF16))
            state_ref[g] = state * chunk_decay_w[:, gs] + contrib

        y = jnp.concatenate(y_parts, axis=1) + xs * dskip_ref[...]
        y = y * _silu(z_ref[rows, :].astype(F32))
        o_ref[rows, :] = (_rms(y) * ng_ref[...]).astype(o_ref.dtype)

    for c in range(SSD_STEP_CHUNKS):
        one_chunk(slice(c * L, (c + 1) * L))


def _ssd_mixer(proj, dt_raw, conv_w, conv_b, dt_bias, a_log, d_skip, norm_g):
    b, s, _ = proj.shape
    assert proj.dtype == BF16
    L = SSD_CHUNK * SSD_STEP_CHUNKS
    z_blk = 0
    xs_blk = SSD_WIDTH // SSD_WIDTH
    bm_blk = (2 * SSD_WIDTH) // SSD_BC_WIDTH
    cm_blk = bm_blk + 1
    pad = LANES - SSD_HEADS
    small = lambda a: pl.BlockSpec(a.shape, lambda bi, c: (0, 0))
    dt_bias_p = jnp.pad(dt_bias, (0, pad)).reshape(1, LANES)
    a_log_p = jnp.pad(a_log, (0, pad)).reshape(1, LANES)
    d_skip_w = jnp.repeat(d_skip, SSD_HEAD_DIM).reshape(1, SSD_WIDTH)
    conv_b2 = conv_b.reshape(1, SSD_CONV_CH)
    norm_g2 = norm_g.reshape(1, SSD_WIDTH)
    return pl.pallas_call(
        _ssd_body,
        grid=(b, s // L),
        in_specs=[
            pl.BlockSpec((None, L, SSD_WIDTH), lambda bi, c: (bi, c, xs_blk)),
            pl.BlockSpec((None, L, SSD_BC_WIDTH), lambda bi, c: (bi, c, bm_blk)),
            pl.BlockSpec((None, L, SSD_BC_WIDTH), lambda bi, c: (bi, c, cm_blk)),
            pl.BlockSpec((None, L, SSD_WIDTH), lambda bi, c: (bi, c, z_blk)),
            pl.BlockSpec((None, L, LANES), lambda bi, c: (bi, c, 0)),
            small(conv_w), small(conv_b2), small(dt_bias_p), small(a_log_p), small(d_skip_w), small(norm_g2),
        ],
        out_specs=pl.BlockSpec((None, L, SSD_WIDTH), lambda bi, c: (bi, c, 0)),
        out_shape=jax.ShapeDtypeStruct((b, s, SSD_WIDTH), BF16),
        scratch_shapes=[
            pltpu.VMEM((SSD_TAIL, SSD_CONV_CH), BF16),
            pltpu.VMEM((SSD_GROUPS, SSD_STATE, SSD_GROUP_WIDTH), F32),
        ],
        compiler_params=_params("parallel", "arbitrary"),
        name="ssd_mixer",
    )(proj, proj, proj, proj, dt_raw, conv_w, conv_b2, dt_bias_p, a_log_p, d_skip_w, norm_g2)


def _attn_body(q_ref, k_ref, v_ref, cos_ref, sin_ref, o_ref, qkv_ref, acc_ref, m_ref, l_ref, *, seq):
    blk = ATT_BLOCK
    half = ATT_HEAD_DIM // 2
    dils = [d for _, d in DILATED_PATTERNS]
    step = dils[1]
    assert dils == [1, step, step * step] and all(w // d == blk for w, d in DILATED_PATTERNS)
    sub = seq // step
    assert seq // dils[2] == blk

    cos = cos_ref[...]
    sin = sin_ref[...]
    q = q_ref[...].astype(F32)
    k = k_ref[...].astype(F32)
    qkv_ref[0, 0] = (q * cos + pltpu.roll(q, half, 1) * sin) * (ATT_HEAD_DIM ** -0.5 * LOG2_E)
    qkv_ref[0, 1] = k * cos + pltpu.roll(k, half, 1) * sin
    qkv_ref[0, 2] = v_ref[...].astype(F32)
    for t in range(3):
        for r in range(step):
            qkv_ref[1, t, pl.ds(r * sub, sub), :] = qkv_ref[0, t, pl.ds(r, sub, stride=step), :]
    for t in range(3):
        for r in range(step):
            for a in range(step):
                qkv_ref[2, t, pl.ds((r + step * a) * blk, blk), :] = qkv_ref[1, t, pl.ds(r * sub + a, blk, stride=step), :]

    row = lax.broadcasted_iota(jnp.int32, (blk, blk), 0)
    col = lax.broadcasted_iota(jnp.int32, (blk, blk), 1)
    cur_ok = col <= row
    prev_ok = col >= row

    def attend(p, starts, with_prev):
        def load(t, st):
            return qkv_ref[p, t, pl.ds(st, blk), :].astype(BF16)

        def window(t, st):
            return jnp.concatenate([load(t, st - blk), load(t, st)], axis=0) if with_prev else load(t, st)

        qb = jnp.stack([load(0, st) for st in starts])
        kk = jnp.stack([window(1, st) for st in starts])
        keys = kk.shape[1]
        vv = jnp.stack([jnp.concatenate([window(2, st), jnp.ones((keys, LANES), BF16)], axis=1) for st in starts])
        ok = jnp.concatenate([prev_ok, cur_ok], axis=1) if with_prev else cur_ok
        s = lax.dot_general(qb, kk, (((2,), (2,)), ((0,), (0,))), preferred_element_type=F32)
        s = jnp.where(ok[None], s, -jnp.inf)
        m = jnp.max(s, axis=2, keepdims=True)
        e = jnp.exp2(s - m)
        acc = lax.dot_general(e.astype(BF16), vv, (((2,), (1,)), ((0,), (0,))), preferred_element_type=F32)
        for i, st in enumerate(starts):
            rows = pl.ds(st, blk)
            acc_ref[p, rows, :] = acc[i, :, 0:ATT_HEAD_DIM]
            l_ref[p, rows, :] = acc[i, :, ATT_HEAD_DIM:]
            m_ref[p, rows, :] = jnp.broadcast_to(m[i], (blk, LANES))

    def groups(starts, size):
        return [starts[i:i + size] for i in range(0, len(starts), size)]

    for p, dil in enumerate(dils):
        class_rows = seq // dil
        firsts = [r * class_rows for r in range(dil)]
        laters = [r * class_rows + n * blk for r in range(dil) for n in range(1, class_rows // blk)]
        for g in groups(firsts, ATT_GROUP_FIRST):
            attend(p, g, False)
        for g in groups(laters, ATT_GROUP_LATER):
            attend(p, g, True)

    def merged(dst, dst_rows, src, src_rows):
        m_a, m_b = m_ref[dst, dst_rows, :], m_ref[src, src_rows, :]
        top = jnp.maximum(m_a, m_b)
        w_a, w_b = jnp.exp2(m_a - top), jnp.exp2(m_b - top)
        acc = w_a * acc_ref[dst, dst_rows, :] + w_b * acc_ref[src, src_rows, :]
        return top, acc, w_a * l_ref[dst, dst_rows, :] + w_b * l_ref[src, src_rows, :]

    for r in range(step):
        for a in range(step):
            mid_rows = pl.ds(r * sub + a, blk, stride=step)
            top, acc, l = merged(1, mid_rows, 2, pl.ds((r + step * a) * blk, blk))
            m_ref[1, mid_rows, :] = top
            acc_ref[1, mid_rows, :] = acc
            l_ref[1, mid_rows, :] = l
    for r in range(step):
        for n in range(sub // blk):
            nat_rows = pl.ds(r + n * blk * step, blk, stride=step)
            _, acc, l = merged(0, nat_rows, 1, pl.ds(r * sub + n * blk, blk))
            acc_ref[0, nat_rows, :] = acc / l
    o_ref[...] = acc_ref[0].astype(o_ref.dtype)


def _dilated_attention(proj, cos2, sin2):
    b, s, _ = proj.shape
    q_blk = (2 * SSD_WIDTH + 2 * SSD_BC_WIDTH) // ATT_HEAD_DIM
    k_blk = q_blk + ATT_HEADS
    v_blk = k_blk + ATT_HEADS
    n_pat = len(DILATED_PATTERNS)
    head = lambda base: pl.BlockSpec((None, s, ATT_HEAD_DIM), lambda bi, h: (bi, 0, base + h))
    table = pl.BlockSpec((None, s, ATT_HEAD_DIM), lambda bi, h: (bi, 0, 0))
    return pl.pallas_call(
        functools.partial(_attn_body, seq=s),
        grid=(b, ATT_HEADS),
        in_specs=[head(q_blk), head(k_blk), head(v_blk), table, table],
        out_specs=pl.BlockSpec((None, s, ATT_HEAD_DIM), lambda bi, h: (bi, 0, h)),
        out_shape=jax.ShapeDtypeStruct((b, s, ATT_WIDTH), BF16),
        scratch_shapes=[
            pltpu.VMEM((n_pat, 3, s, ATT_HEAD_DIM), F32),
            pltpu.VMEM((n_pat, s, ATT_HEAD_DIM), F32),
            pltpu.VMEM((n_pat, s, LANES), F32),
            pltpu.VMEM((n_pat, s, LANES), F32),
        ],
        compiler_params=_params("parallel", "arbitrary"),
        name="dilated_attention",
    )(proj, proj, proj, cos2, sin2)


def kernel(x, c, positions, w_mod, b_mod, norm_pre, norm_post, ffn_w_gate, ffn_w_up, ffn_w_down, hyb_w_in, hyb_conv_w, hyb_conv_b, hyb_dt_bias, hyb_a_log, hyb_d_skip, hyb_norm_g, hyb_w_out, sgu_w_in, sgu_b_in, sgu_ln_g, sgu_ln_b, sgu_w_spatial, sgu_b_spatial, sgu_w_out):
    depth = w_mod.shape[0]
    b, s, d = x.shape
    n_sub = norm_pre.shape[1]

    mod = _modulation(c, w_mod, b_mod).reshape(depth, b, n_sub, 3, d)
    g_pre = norm_pre.reshape(depth, n_sub, 1, d)
    g_post = norm_post.reshape(depth, n_sub, 1, d)
    w_gate = ffn_w_gate.astype(BF16)
    w_up = ffn_w_up.astype(BF16)
    w_down = ffn_w_down.astype(BF16)

    for layer in range(depth):
        i = layer // 2
        x = _ffn_sublayer(x, mod, g_pre, g_post, w_gate, w_up, w_down, layer, 0, 0)
        if layer % 2 == 0:
            w_in = hyb_w_in[i]
            dt_lo = SSD_WIDTH + SSD_CONV_CH
            dt_hi = dt_lo + SSD_HEADS
            w_dt = jnp.pad(w_in[:, dt_lo:dt_hi], ((0, 0), (0, LANES - SSD_HEADS))).astype(BF16)
            proj, dt_raw = _in_proj(x, mod, g_pre, (w_in[:, :dt_lo].astype(BF16), w_in[:, dt_hi:].astype(BF16), w_dt),
                                    layer, 1, hyb=True, rows=PROJ_ROWS_HYB, tile=PROJ_TILE)
            cos2, sin2 = _rope_tables(positions)
            y_a = _ssd_mixer(proj, dt_raw, hyb_conv_w[i], hyb_conv_b[i], hyb_dt_bias[i], hyb_a_log[i],
                             hyb_d_skip[i], hyb_norm_g[i])
            y_b = _dilated_attention(proj, cos2, sin2)
            x = _mixer_out((y_a, y_b), hyb_w_out[i].astype(BF16), x, mod, g_post, layer, 1, sgu=False)
        else:
            zz, = _in_proj(x, mod, g_pre, (sgu_w_in[i].astype(BF16), sgu_b_in[i].reshape(1, -1)), layer, 1,
                           hyb=False, rows=PROJ_ROWS_GELU, tile=PROJ_TILE_GELU)
            x = _mixer_out((zz, sgu_ln_g[i], sgu_ln_b[i], sgu_w_spatial[i], sgu_b_spatial[i]),
                           sgu_w_out[i].astype(BF16), x, mod, g_post, layer, 1, sgu=True)
        x = _ffn_sublayer(x, mod, g_pre, g_post, w_gate, w_up, w_down, layer, 2, 1)
    return x
```
